```python
import jax, jax.numpy as jnp
from jax import lax
import numpy as np

D_MODEL = 2048
BATCH = 1
SEQ = 16384
DEPTH = 1

CTX_LEN = 256
GRID_W = 64
CHUNK = 128
RET_HEAD_DIM = 128
RET_HEADS = D_MODEL // RET_HEAD_DIM
RET_QK = RET_HEADS * RET_HEAD_DIM
RET_V = RET_HEADS * RET_HEAD_DIM
SSD_INNER = 2 * D_MODEL
SSD_HEAD_DIM = 64
SSD_HEADS = SSD_INNER // SSD_HEAD_DIM
SSD_STATE = 128
SSD_GROUPS = 8
SSD_CONV = 5
SSD_CONV_CH = SSD_INNER + 2 * SSD_GROUPS * SSD_STATE
D_FF = 4 * D_MODEL
ROPE_BASE = 10000.0
EPS = 1e-6
IN_SIZES = (RET_QK, RET_QK, RET_V, RET_V, SSD_INNER, SSD_CONV_CH, 2 * SSD_HEADS, 2 * D_MODEL)
IN_COLS = sum(IN_SIZES)
IN_SPLITS = tuple(int(s) for s in np.cumsum(IN_SIZES)[:-1])

kernel_name = "hybrid_retention_ssd_prefix_dit"


def rms_norm(t, w):
    tf = t.astype(jnp.float32)
    y = tf * lax.rsqrt(jnp.mean(tf * tf, axis=-1, keepdims=True) + EPS)
    return (y * w.astype(jnp.float32)).astype(t.dtype)


def head_layer_norm(y):
    yf = y.astype(jnp.float32)
    mu = jnp.mean(yf, axis=-1, keepdims=True)
    d = yf - mu
    return d * lax.rsqrt(jnp.mean(d * d, axis=-1, keepdims=True) + EPS)


def rope_angles(L):
    rows = L // GRID_W
    row = jnp.repeat(jnp.arange(rows), GRID_W).astype(jnp.float32)
    col = jnp.tile(jnp.arange(GRID_W), rows).astype(jnp.float32)
    n_freq = RET_HEAD_DIM // 4
    freqs = ROPE_BASE ** (-jnp.arange(n_freq, dtype=jnp.float32) / n_freq)
    return row[:, None] * freqs, col[:, None] * freqs


def rotate(t, ang):
    cos = jnp.cos(ang)[None, :, None, :]
    sin = jnp.sin(ang)[None, :, None, :]
    t1, t2 = jnp.split(t, 2, axis=-1)
    return jnp.concatenate([t1 * cos - t2 * sin, t1 * sin + t2 * cos], axis=-1).astype(t.dtype)


def axial_rope(t, ang_row, ang_col):
    half = RET_HEAD_DIM // 2
    return jnp.concatenate([rotate(t[..., :half], ang_row), rotate(t[..., half:], ang_col)], axis=-1)


def chunked_scan(q, k, v, log_a, h0, strict, need_output):
    bsz, L, G, N = q.shape
    H, P = v.shape[2], v.shape[3]
    r = H // G
    nc = L // CHUNK
    q = q.reshape(bsz, nc, CHUNK, G, N)
    k = k.reshape(bsz, nc, CHUNK, G, N)
    v = v.reshape(bsz, nc, CHUNK, G, r, P)
    acs = jnp.cumsum(log_a.astype(jnp.float32).reshape(bsz, nc, CHUNK, G, r), axis=2)
    to_end = jnp.exp(acs[:, :, -1:] - acs)
    chunk_state = jnp.einsum('bcjgn,bcjgrp->bcgrpn', k, v * to_end[..., None]).astype(jnp.float32)
    chunk_decay = jnp.exp(acs[:, :, -1])

    def step(h, inp):
        st, dec = inp
        return dec[..., None, None] * h + st, h

    h_last, h_in = lax.scan(step, h0.astype(jnp.float32).reshape(bsz, G, r, P, N),
                            (jnp.moveaxis(chunk_state, 1, 0), jnp.moveaxis(chunk_decay, 1, 0)))
    h_last = h_last.reshape(bsz, H, P, N)
    if not need_output:
        return None, h_last
    h_in = jnp.moveaxis(h_in, 0, 1)
    y_cross = jnp.einsum('bcign,bcgrpn->bcigrp', q, h_in) * jnp.exp(acs)[..., None]
    acs_t = jnp.moveaxis(acs, 2, -1)
    mask = jnp.tril(jnp.ones((CHUNK, CHUNK), dtype=bool), -1 if strict else 0)
    decay = jnp.exp(jnp.where(mask, acs_t[..., :, None] - acs_t[..., None, :], -jnp.inf))
    scores = jnp.einsum('bcign,bcjgn->bcgij', q, k)
    y_intra = jnp.einsum('bcgrij,bcjgrp->bcigrp', scores[:, :, :, None] * decay, v)
    return (y_intra + y_cross).reshape(bsz, L, H, P), h_last


def bidir_scan(q, k, v_f, v_b, la_f, la_b, h0_f, h0_b, need_output):
    y_f, h_f = chunked_scan(q, k, v_f, la_f, h0_f, False, need_output)
    fl = lambda t: jnp.flip(t, axis=1)
    y_b, h_b = chunked_scan(fl(q), fl(k), fl(v_b), fl(la_b), h0_b, True, need_output)
    y = y_f + fl(y_b) if need_output else None
    return y, h_f, h_b


def centred_dwconv(t, w, b):
    pad = SSD_CONV // 2
    L = t.shape[1]
    tp = jnp.pad(t, ((0, 0), (pad, pad), (0, 0)))
    out = b
    for j in range(SSD_CONV):
        out = out + tp[:, j:j + L] * w[j]
    return out


def token_mixers(u, w_in, conv_w, conv_b, ret_decay_logit, ret_norm_w, ssd_a_log, ssd_dt_bias, ssd_d,
                 ssd_norm_w, w_ret_out, w_ssd_out, rope, init, need_output):
    bsz, L, _ = u.shape
    q, k, v, g_ret, z, xbc, dt_raw, branch_logits = jnp.split(u @ w_in, IN_SPLITS, axis=-1)
    q = q.reshape(bsz, L, RET_HEADS, RET_HEAD_DIM)
    k = k.reshape(bsz, L, RET_HEADS, RET_HEAD_DIM) * (RET_HEAD_DIM ** -0.5)
    if rope is not None:
        q = axial_rope(q, *rope)
        k = axial_rope(k, *rope)
    v = v.reshape(bsz, L, RET_HEADS, RET_HEAD_DIM)
    log_gamma = jax.nn.log_sigmoid(ret_decay_logit.astype(jnp.float32))
    la_rf = jnp.broadcast_to(log_gamma[0], (bsz, L, RET_HEADS))
    la_rb = jnp.broadcast_to(log_gamma[1], (bsz, L, RET_HEADS))
    y_ret, r_f, r_b = bidir_scan(q, k, v, v, la_rf, la_rb, init[0], init[1], need_output)
    xbc = jax.nn.silu(centred_dwconv(xbc, conv_w, conv_b))
    xs, b_in, c_out = jnp.split(xbc, [SSD_INNER, SSD_INNER + SSD_GROUPS * SSD_STATE], axis=-1)
    xs = xs.reshape(bsz, L, SSD_HEADS, SSD_HEAD_DIM)
    b_in = b_in.reshape(bsz, L, SSD_GROUPS, SSD_STATE)
    c_out = c_out.reshape(bsz, L, SSD_GROUPS, SSD_STATE)
    dt = jax.nn.softplus(dt_raw.astype(jnp.float32).reshape(bsz, L, 2, SSD_HEADS) + ssd_dt_bias.astype(jnp.float32))
    a = -jnp.exp(ssd_a_log.astype(jnp.float32))
    y_ssd, s_f, s_b = bidir_scan(c_out, b_in, xs * dt[:, :, 0, :, None], xs * dt[:, :, 1, :, None],
                                 dt[:, :, 0] * a[0], dt[:, :, 1] * a[1], init[2], init[3], need_output)
    states = (r_f, r_b, s_f, s_b)
    if not need_output:
        return None, states
    yr = head_layer_norm(y_ret).reshape(bsz, L, RET_V) * ret_norm_w
    yr = (yr * jax.nn.silu(g_ret.astype(jnp.float32))).astype(u.dtype) @ w_ret_out
    ys = (y_ssd + xs * ssd_d[:, None]).reshape(bsz, L, SSD_INNER)
    ys = rms_norm((ys * jax.nn.silu(z.astype(jnp.float32))).astype(u.dtype), ssd_norm_w) @ w_ssd_out
    g_r, g_s = jnp.split(branch_logits, 2, axis=-1)
    m = jax.nn.sigmoid(g_r) * yr + jax.nn.sigmoid(g_s) * ys
    return m.astype(u.dtype), states


def sq_relu_mlp(t, w1, w2):
    return jnp.square(jax.nn.relu(t @ w1)) @ w2


def setup_inputs(seed: int = 0) -> dict:
    key = jax.random.key(seed)
    ks = jax.random.split(key, 32)
    f32 = jnp.float32
    nrm = lambda kk, shape, fan_in: jax.random.normal(kk, shape, f32) * (fan_in ** -0.5)
    gain = lambda kk, shape: 1.0 + 0.02 * jax.random.normal(kk, shape, f32)
    p = 2.0 ** (-5.0 - jnp.arange(RET_HEADS, dtype=f32))
    logit0 = jnp.log1p(-p) - jnp.log(p)
    ret_decay_logit = jnp.stack([logit0, logit0[::-1]])[None] + 0.01 * jax.random.normal(ks[9], (DEPTH, 2, RET_HEADS), f32)
    dt0 = jnp.exp(jax.random.uniform(ks[12], (DEPTH, 2, SSD_HEADS), f32, jnp.log(1e-3), jnp.log(1e-1)))
    return {
        "x": jax.random.normal(ks[0], (BATCH, SEQ, D_MODEL), f32),
        "c": jax.random.normal(ks[1], (BATCH, D_MODEL), f32),
        "ctx": jax.random.normal(ks[2], (BATCH, CTX_LEN, D_MODEL), f32),
        "c_ctx": jax.random.normal(ks[3], (D_MODEL,), f32),
        "w_mod": 0.5 * nrm(ks[4], (DEPTH, D_MODEL, 6 * D_MODEL), D_MODEL),
        "b_mod": 0.02 * jax.random.normal(ks[5], (DEPTH, 6 * D_MODEL), f32),
        "norm1_w": gain(ks[6], (DEPTH, D_MODEL)),
        "w_in": nrm(ks[7], (DEPTH, D_MODEL, IN_COLS), D_MODEL),
        "conv_w": nrm(ks[8], (DEPTH, SSD_CONV, SSD_CONV_CH), SSD_CONV),
        "conv_b": 0.02 * jax.random.normal(ks[10], (DEPTH, SSD_CONV_CH), f32),
        "ret_decay_logit": ret_decay_logit,
        "ret_norm_w": gain(ks[11], (DEPTH, RET_V)),
        "ssd_a_log": jnp.log(jax.random.uniform(ks[13], (DEPTH, 2, SSD_HEADS), f32, 1.0, 16.0)),
        "ssd_dt_bias": dt0 + jnp.log(-jnp.expm1(-dt0)),
        "ssd_d": 1.0 + 0.1 * jax.random.normal(ks[14], (DEPTH, SSD_HEADS), f32),
        "ssd_norm_w": gain(ks[15], (DEPTH, SSD_INNER)),
        "w_ret_out": nrm(ks[16], (DEPTH, RET_V, D_MODEL), RET_V),
        "w_ssd_out": nrm(ks[17], (DEPTH, SSD_INNER, D_MODEL), SSD_INNER),
        "w_o": nrm(ks[18], (DEPTH, D_MODEL, D_MODEL), D_MODEL),
        "norm2_w": gain(ks[19], (DEPTH, D_MODEL)),
        "w_mlp1": nrm(ks[20], (DEPTH, D_MODEL, D_FF), D_MODEL),
        "w_mlp2": nrm(ks[21], (DEPTH, D_FF, D_MODEL), D_FF),
        "final_norm_w": gain(ks[22], (D_MODEL,)),
    }


def reference(x, c, ctx, c_ctx, w_mod, b_mod, norm1_w, w_in, conv_w, conv_b, ret_decay_logit, ret_norm_w,
              ssd_a_log, ssd_dt_bias, ssd_d, ssd_norm_w, w_ret_out, w_ssd_out, w_o, norm2_w, w_mlp1, w_mlp2,
              final_norm_w):
    bsz, L, _ = x.shape
    rope = rope_angles(L)
    zeros_r = jnp.zeros((bsz, RET_HEADS, RET_HEAD_DIM, RET_HEAD_DIM), jnp.float32)
    zeros_s = jnp.zeros((bsz, SSD_HEADS, SSD_HEAD_DIM, SSD_STATE), jnp.float32)
    ctx_init = (zeros_r, zeros_r, zeros_s, zeros_s)
    h, hc = x, ctx
    for layer in range(DEPTH):
        last = layer == DEPTH - 1
        mod = jax.nn.silu(c) @ w_mod[layer] + b_mod[layer]
        mod_c = jax.nn.silu(c_ctx) @ w_mod[layer] + b_mod[layer]
        sh_a, sc_a, g_a, sh_f, sc_f, g_f = jnp.split(mod[:, None, :], 6, axis=-1)
        csh_a, csc_a, cg_a, csh_f, csc_f, cg_f = jnp.split(mod_c, 6, axis=-1)
        mix_w = (w_in[layer], conv_w[layer], conv_b[layer], ret_decay_logit[layer], ret_norm_w[layer],
                 ssd_a_log[layer], ssd_dt_bias[layer], ssd_d[layer], ssd_norm_w[layer],
                 w_ret_out[layer], w_ssd_out[layer])
        uc = rms_norm(hc, norm1_w[layer]) * (1.0 + csc_a) + csh_a
        mc, ctx_states = token_mixers(uc, *mix_w, None, ctx_init, not last)
        u = rms_norm(h, norm1_w[layer]) * (1.0 + sc_a) + sh_a
        m, _ = token_mixers(u, *mix_w, rope, ctx_states, True)
        h = h + g_a * (m @ w_o[layer])
        f = rms_norm(h, norm2_w[layer]) * (1.0 + sc_f) + sh_f
        h = h + g_f * sq_relu_mlp(f, w_mlp1[layer], w_mlp2[layer])
        if not last:
            hc = hc + cg_a * (mc @ w_o[layer])
            fc = rms_norm(hc, norm2_w[layer]) * (1.0 + csc_f) + csh_f
            hc = hc + cg_f * sq_relu_mlp(fc, w_mlp1[layer], w_mlp2[layer])
    return rms_norm(h, final_norm_w)
```

```python
import functools
import math

import jax
import jax.numpy as jnp
from jax import lax
from jax.experimental import pallas as pl
from jax.experimental.pallas import tpu as pltpu

f32 = jnp.float32
bf16 = jnp.bfloat16

D = 2048
T = 128
GRID_W = 64
HD = 128
RH = D // HD
SI = 2 * D
P = 64
SH = SI // P
NS = 128
NG = 8
R = SH // NG
GW = R * P
KC = 5
DFF = 4 * D
ROPE_BASE = 10000.0
EPS = 1e-6
LANES = 128
HALO = 8
VMEM_LIMIT = 48 * 1024 * 1024


def _cp(sem):
    return pltpu.CompilerParams(dimension_semantics=sem, vmem_limit_bytes=VMEM_LIMIT)


def _nt(a, b):
    return lax.dot_general(a, b, (((1,), (1,)), ((), ())), preferred_element_type=f32)


def _dot(a, b):
    return jnp.dot(a, b, preferred_element_type=f32)


def _sigmoid(x):
    return 1.0 / (1.0 + jnp.exp(-x))


def _silu(x):
    return x * _sigmoid(x)


def _softplus(x):
    return jnp.maximum(x, 0.0) + jnp.log1p(jnp.exp(-jnp.abs(x)))


def _log_sigmoid(x):
    return jnp.minimum(x, 0.0) - jnp.log1p(jnp.exp(-jnp.abs(x)))


def _cumsum_rows(la, tri):
    a1 = la.astype(bf16)
    r1 = la - a1.astype(f32)
    a2 = r1.astype(bf16)
    r2 = r1 - a2.astype(f32)
    a3 = r2.astype(bf16)
    return _dot(tri, a1) + _dot(tri, a2) + _dot(tri, a3)


def _tri_incl():
    ii = lax.broadcasted_iota(jnp.int32, (T, T), 0)
    jj = lax.broadcasted_iota(jnp.int32, (T, T), 1)
    return jnp.where(jj <= ii, 1.0, 0.0).astype(bf16)


def _rope_tab_kernel(cos_ref, sin_ref, *, tm):
    i = pl.program_id(0)
    pos = i * tm + lax.broadcasted_iota(jnp.int32, (tm, LANES), 0)
    lane = lax.broadcasted_iota(jnp.int32, (tm, LANES), 1)
    row = (pos // GRID_W).astype(f32)
    col = (pos % GRID_W).astype(f32)
    fidx = (lane % (HD // 4)).astype(f32)
    freqs = jnp.exp(fidx * (-math.log(ROPE_BASE) / (HD // 4)))
    ang = jnp.where(lane < HD // 2, row, col) * freqs
    s = jnp.sin(ang)
    cos_ref[...] = jnp.cos(ang)
    sin_ref[...] = jnp.where((lane % (HD // 2)) < HD // 4, -s, s)


def _rope_tables(L):
    tm = min(L, 512)
    return pl.pallas_call(
        functools.partial(_rope_tab_kernel, tm=tm),
        grid=(L // tm,),
        out_specs=[pl.BlockSpec((tm, LANES), lambda i: (i, 0))] * 2,
        out_shape=[jax.ShapeDtypeStruct((L, LANES), f32)] * 2,
        compiler_params=_cp(("parallel",)),
        name="rope_tables",
    )()


def _ret_tab_kernel(lg_ref, w_ref, ef_ref, eb_ref, wf_ref, wb_ref, dec_ref):
    lf = _log_sigmoid(lg_ref[0])
    lb = _log_sigmoid(lg_ref[1])
    ii = lax.broadcasted_iota(jnp.int32, (T, T), 0)
    jj = lax.broadcasted_iota(jnp.int32, (T, T), 1)
    dl = (ii - jj).astype(f32)
    w_ref[...] = jnp.exp(jnp.where(jj <= ii, dl * lf, -dl * lb))
    idx = lax.broadcasted_iota(jnp.int32, (T, LANES), 0).astype(f32)
    ef_ref[...] = jnp.exp((idx + 1.0) * lf)
    eb_ref[...] = jnp.exp((T - idx) * lb)
    wf_ref[...] = jnp.exp((T - 1.0 - idx) * lf)
    wb_ref[...] = jnp.exp(idx * lb)
    srow = lax.broadcasted_iota(jnp.int32, (8, LANES), 0)
    dec_ref[...] = jnp.where(srow == 0, jnp.exp(T * lf), jnp.exp(T * lb))


def _ret_tables(logit):
    lg = jnp.broadcast_to(logit[:, :, None, None], (2, RH, 1, LANES))
    tab = pl.BlockSpec((None, T, LANES), lambda h: (h, 0, 0))
    return pl.pallas_call(
        _ret_tab_kernel,
        grid=(RH,),
        in_specs=[pl.BlockSpec((2, None, 1, LANES), lambda h: (0, h, 0, 0))],
        out_specs=[tab, tab, tab, tab, tab, pl.BlockSpec((None, 8, LANES), lambda h: (h, 0, 0))],
        out_shape=[jax.ShapeDtypeStruct((RH, T, T), f32)] + [jax.ShapeDtypeStruct((RH, T, LANES), f32)] * 4
        + [jax.ShapeDtypeStruct((RH, 8, LANES), f32)],
        compiler_params=_cp(("parallel",)),
        name="ret_tables",
    )(lg)


def _mod_kernel(a_ref, w_ref, b_ref, o_ref):
    a = _silu(a_ref[...])
    o_ref[...] = _dot(a, w_ref[...]) + b_ref[...]


def _modulation(a8, w, b):
    tn = 1024
    n = w.shape[1]
    return pl.pallas_call(
        _mod_kernel,
        grid=(n // tn,),
        in_specs=[pl.BlockSpec((8, D), lambda j: (0, 0)),
                  pl.BlockSpec((D, tn), lambda j: (0, j)),
                  pl.BlockSpec((1, tn), lambda j: (0, j))],
        out_specs=pl.BlockSpec((8, tn), lambda j: (0, j)),
        out_shape=jax.ShapeDtypeStruct((8, n), f32),
        compiler_params=_cp(("parallel",)),
        name="modulation",
    )(a8, w, b)


def _rms_mod(x, nw, sc, sh):
    y = x * lax.rsqrt(jnp.mean(x * x, axis=-1, keepdims=True) + EPS)
    return (y * nw) * (1.0 + sc) + sh


def _norm_kernel(x_ref, nw_ref, sc_ref, sh_ref, o_ref):
    o_ref[...] = _rms_mod(x_ref[...], nw_ref[...], sc_ref[...], sh_ref[...]).astype(o_ref.dtype)


def _norm_mod(x, nw, sc, sh):
    L = x.shape[0]
    tm = min(L, 256)
    vec = pl.BlockSpec((1, D), lambda i: (0, 0))
    return pl.pallas_call(
        _norm_kernel,
        grid=(L // tm,),
        in_specs=[pl.BlockSpec((tm, D), lambda i: (i, 0)), vec, vec, vec],
        out_specs=pl.BlockSpec((tm, D), lambda i: (i, 0)),
        out_shape=jax.ShapeDtypeStruct((L, D), bf16),
        compiler_params=_cp(("parallel",)),
        name="norm_mod",
    )(x, nw, sc, sh)


def _proj_kernel(u_ref, w_ref, *rest, bw, tn, rope, scale_from):
    acc = _dot(u_ref[...], w_ref[...])
    if rope:
        cos_ref, sin_ref, o_ref = rest
        scale = jnp.where(pl.program_id(1) >= scale_from, HD ** -0.5, 1.0).astype(f32)
        cos = cos_ref[...]
        sin = sin_ref[...]
        lane = lax.broadcasted_iota(jnp.int32, cos.shape, 1)
        first = (lane % (HD // 2)) < HD // 4
        for hh in range(tn // HD):
            t = acc[:, hh * HD:(hh + 1) * HD] * scale
            partner = jnp.where(first, pltpu.roll(t, HD - HD // 4, 1), pltpu.roll(t, HD // 4, 1))
            o_ref[hh] = (t * cos + partner * sin).astype(o_ref.dtype)
    else:
        (o_ref,) = rest
        if bw:
            for hh in range(tn // bw):
                o_ref[hh] = acc[:, hh * bw:(hh + 1) * bw].astype(o_ref.dtype)
        else:
            o_ref[...] = acc.astype(o_ref.dtype)


def _project(u, w, *, bw, tn, out_dtype=bf16, rope=None, scale_from=0, name):
    M = u.shape[0]
    N = w.shape[1]
    tm = min(M, 1024)
    in_specs = [pl.BlockSpec((tm, D), lambda i, j: (i, 0)), pl.BlockSpec((D, tn), lambda i, j: (0, j))]
    args = [u, w]
    if rope is not None:
        in_specs += [pl.BlockSpec((tm, LANES), lambda i, j: (i, 0))] * 2
        args += list(rope)
    if bw:
        out_spec = pl.BlockSpec((tn // bw, tm, bw), lambda i, j: (j, i, 0))
        out_shape = jax.ShapeDtypeStruct((N // bw, M, bw), out_dtype)
    else:
        out_spec = pl.BlockSpec((tm, tn), lambda i, j: (i, j))
        out_shape = jax.ShapeDtypeStruct((M, N), out_dtype)
    return pl.pallas_call(
        functools.partial(_proj_kernel, bw=bw, tn=tn, rope=rope is not None, scale_from=scale_from),
        grid=(M // tm, N // tn),
        in_specs=in_specs,
        out_specs=out_spec,
        out_shape=out_shape,
        compiler_params=_cp(("parallel", "parallel")),
        name=name,
    )(*args)


def _conv_silu(ext_ref, main, prev, nxt, cw, cb, has_prev, has_next, width):
    ext_ref[0:HALO, 0:width] = jnp.where(has_prev, prev.astype(f32), 0.0)
    ext_ref[HALO:HALO + T, 0:width] = main.astype(f32)
    ext_ref[HALO + T:2 * HALO + T, 0:width] = jnp.where(has_next, nxt.astype(f32), 0.0)
    acc = cb
    for j in range(KC):
        acc = acc + ext_ref[pl.ds(HALO - KC // 2 + j, T), 0:width] * cw[j:j + 1, :]
    return _silu(acc)


def _decay_prologue(dt_ref, bias_ref, arow_ref):
    dt = _softplus(dt_ref[...] + bias_ref[...])
    la = dt * arow_ref[...]
    acs = _cumsum_rows(la, _tri_incl())
    tot = acs[T - 1:T, :]
    return dt, la, acs, tot


def _transpose_blocks(x):
    n = x.shape[1] // LANES
    return jnp.concatenate([x[:, b * LANES:(b + 1) * LANES].T for b in range(n)], axis=0)


def _ssd_state_update(hs_ref, g, xc, bc16, w_rows, dec_rows):
    xT = _transpose_blocks(xc)
    lhs = jnp.concatenate(
        [(xT[r * P:(r + 1) * P, :] * w_rows[r:r + 1, :]).astype(bf16) for r in range(R)], axis=0)
    upd = _dot(lhs, bc16)
    old = hs_ref[g]
    hs_ref[g] = jnp.concatenate(
        [old[r * P:(r + 1) * P, :] * dec_rows[r:r + 1, :] + upd[r * P:(r + 1) * P, :] for r in range(R)], axis=0)


def _ret_state_update(hr_ref, h, v16, k16, wcol, dec_row):
    vw = (v16.astype(f32) * wcol).T.astype(bf16)
    hr_ref[h] = hr_ref[h] * dec_row + _dot(vw, k16)


def _state_kernel(*refs, nc, fwd, emit, write_conv):
    it = iter(refs)
    k_ref, v_ref = next(it), next(it)
    xs_ref, xs_p, xs_n = next(it), next(it), next(it)
    b_ref, b_p, b_n = next(it), next(it), next(it)
    if write_conv:
        c_ref, c_p, c_n = next(it), next(it), next(it)
    dt_ref = next(it)
    cwx, cbx, cwb, cbb = next(it), next(it), next(it), next(it)
    if write_conv:
        cwc, cbc = next(it), next(it)
    bias_ref, arow_ref = next(it), next(it)
    rw_ref, rdec_ref = next(it), next(it)
    h0r_ref, h0s_ref = next(it), next(it)
    if emit:
        gr_ref, gs_ref = next(it), next(it)
    if write_conv:
        xo_ref, bo_ref, co_ref = next(it), next(it), next(it)
    hfr_ref, hfs_ref = next(it), next(it)
    hr, hs, ext, wT_s, decT_s = next(it), next(it), next(it), next(it), next(it)

    s = pl.program_id(0)
    c = s if fwd else nc - 1 - s
    has_prev = c > 0
    has_next = c < nc - 1

    @pl.when(s == 0)
    def _():
        hr[...] = h0r_ref[...]
        hs[...] = h0s_ref[...]

    if emit:
        gr_ref[0] = hr[...].astype(bf16)
        gs_ref[0] = hs[...].astype(bf16)

    dt, la, acs, tot = _decay_prologue(dt_ref, bias_ref, arow_ref)
    wexp = jnp.exp(tot - acs) if fwd else jnp.exp(acs - la)
    wT_s[...] = (wexp * dt).T
    decT_s[...] = jnp.broadcast_to(jnp.exp(tot), (T, 2 * SH)).T
    off = 0 if fwd else SH

    def grp(g, carry):
        xc = _conv_silu(ext, xs_ref[g], xs_p[g], xs_n[g], cwx[g], cbx[g], has_prev, has_next, GW)
        bc = _conv_silu(ext, b_ref[g], b_p[g], b_n[g], cwb[g], cbb[g], has_prev, has_next, NS)
        xc16 = xc.astype(bf16)
        bc16 = bc.astype(bf16)
        if write_conv:
            cc = _conv_silu(ext, c_ref[g], c_p[g], c_n[g], cwc[g], cbc[g], has_prev, has_next, NS)
            xo_ref[g] = xc16
            bo_ref[g] = bc16
            co_ref[g] = cc.astype(bf16)
        r0 = pl.multiple_of(off + g * R, R)
        _ssd_state_update(hs, g, xc16.astype(f32), bc16, wT_s[pl.ds(r0, R), :], decT_s[pl.ds(r0, R), :])
        return carry

    lax.fori_loop(0, NG, grp, 0)

    dsel = 0 if fwd else 1

    def head(h, carry):
        _ret_state_update(hr, h, v_ref[h], k_ref[h], rw_ref[h], rdec_ref[h][dsel:dsel + 1, :])
        return carry

    lax.fori_loop(0, RH, head, 0)

    @pl.when(s == nc - 1)
    def _():
        hfr_ref[...] = hr[...]
        hfs_ref[...] = hs[...]


def _state_sweep(L, *, fwd, emit, write_conv, k, k_blk, v, v_blk, xs, xs_blk, bsrc, b_blk, c_blk, dt,
                 convw, bias, arow, rw, rdec, h0r, h0s):
    nc = L // T
    tb = T // HALO
    nrb = L // HALO

    def cidx(s):
        return s if fwd else nc - 1 - s

    def main(lead, n, width):
        return pl.BlockSpec((n, T, width), lambda s: (lead, cidx(s), 0))

    def prev(lead, n, width):
        return pl.BlockSpec((n, HALO, width), lambda s: (lead, jnp.maximum(cidx(s) * tb - 1, 0), 0))

    def nxt(lead, n, width):
        return pl.BlockSpec((n, HALO, width), lambda s: (lead, jnp.minimum((cidx(s) + 1) * tb, nrb - 1), 0))

    def whole(a):
        nd = a.ndim
        return pl.BlockSpec(a.shape, lambda s: (0,) * nd)

    cwx, cbx, cwb, cbb, cwc, cbc = convw
    in_specs = [main(k_blk, RH, HD), main(v_blk, RH, HD),
                main(xs_blk, NG, GW), prev(xs_blk, NG, GW), nxt(xs_blk, NG, GW),
                main(b_blk, NG, NS), prev(b_blk, NG, NS), nxt(b_blk, NG, NS)]
    args = [k, v, xs, xs, xs, bsrc, bsrc, bsrc]
    if write_conv:
        in_specs += [main(c_blk, NG, NS), prev(c_blk, NG, NS), nxt(c_blk, NG, NS)]
        args += [bsrc, bsrc, bsrc]
    in_specs += [pl.BlockSpec((T, 2 * SH), lambda s: (cidx(s), 0))]
    args += [dt]
    small = [cwx, cbx, cwb, cbb] + ([cwc, cbc] if write_conv else []) + [bias, arow, rw, rdec, h0r, h0s]
    in_specs += [whole(a) for a in small]
    args += small

    out_specs, out_shape = [], []
    if emit:
        out_specs += [pl.BlockSpec((1, RH, HD, HD), lambda s: (cidx(s), 0, 0, 0)),
                      pl.BlockSpec((1, NG, GW, NS), lambda s: (cidx(s), 0, 0, 0))]
        out_shape += [jax.ShapeDtypeStruct((nc, RH, HD, HD), bf16), jax.ShapeDtypeStruct((nc, NG, GW, NS), bf16)]
    if write_conv:
        out_specs += [pl.BlockSpec((NG, T, GW), lambda s: (0, cidx(s), 0)),
                      pl.BlockSpec((NG, T, NS), lambda s: (0, cidx(s), 0)),
                      pl.BlockSpec((NG, T, NS), lambda s: (0, cidx(s), 0))]
        out_shape += [jax.ShapeDtypeStruct((NG, L, GW), bf16), jax.ShapeDtypeStruct((NG, L, NS), bf16),
                      jax.ShapeDtypeStruct((NG, L, NS), bf16)]
    out_specs += [pl.BlockSpec((RH, HD, HD), lambda s: (0, 0, 0)), pl.BlockSpec((NG, GW, NS), lambda s: (0, 0, 0))]
    out_shape += [jax.ShapeDtypeStruct((RH, HD, HD), f32), jax.ShapeDtypeStruct((NG, GW, NS), f32)]

    return pl.pallas_call(
        functools.partial(_state_kernel, nc=nc, fwd=fwd, emit=emit, write_conv=write_conv),
        grid=(nc,),
        in_specs=in_specs,
        out_specs=out_specs,
        out_shape=out_shape,
        scratch_shapes=[pltpu.VMEM((RH, HD, HD), f32), pltpu.VMEM((NG, GW, NS), f32),
                        pltpu.VMEM((T + 2 * HALO, GW), f32),
                        pltpu.VMEM((2 * SH, T), f32), pltpu.VMEM((2 * SH, T), f32)],
        compiler_params=_cp(("arbitrary",)),
        name="state_sweep_" + ("f" if fwd else "b") + ("_emit" if emit else ""),
    )(*args)


def _fwd_kernel(q_ref, k_ref, v_ref, gr_in_ref, z_ref, xs_ref, b_ref, c_ref, dt_ref, gret_ref, gssd_ref,
                wret_ref, ef_ref, eb_ref, wf_ref, rdec_ref, bias_ref, arow_ref, dexp_ref, snw_ref, rnw_ref,
                h0r_ref, h0s_ref, yr_ref, ys_ref,
                hr, hs, ypre, a1_s, e1_s, a1T_s, dtT_s, wT_s, decT_s):
    s = pl.program_id(0)

    @pl.when(s == 0)
    def _():
        hr[...] = h0r_ref[...]
        hs[...] = h0s_ref[...]

    dt, la, acs, tot = _decay_prologue(dt_ref, bias_ref, arow_ref)
    lane = lax.broadcasted_iota(jnp.int32, (T, 2 * SH), 1)
    is_f = lane < SH
    a1 = jnp.where(is_f, acs, acs - la)
    e1 = jnp.exp(jnp.where(is_f, acs, tot - (acs - la)))
    a1_s[...] = a1
    e1_s[...] = e1
    a1T_s[...] = a1.T
    dtT_s[...] = dt.T
    wT_s[...] = (jnp.exp(tot - acs) * dt).T
    decT_s[...] = jnp.broadcast_to(jnp.exp(tot), (T, 2 * SH)).T

    ii = lax.broadcasted_iota(jnp.int32, (T, T), 0)
    jj = lax.broadcasted_iota(jnp.int32, (T, T), 1)
    lower = jj <= ii

    def grp(g, ssq):
        c16 = c_ref[g]
        b16 = b_ref[g]
        sc = _nt(c16, b16)
        cf32 = c16.astype(f32)
        shift = (2 * SH - g * R) % (2 * SH)
        a1r = pltpu.roll(a1_s[...], shift, 1)
        e1r = pltpu.roll(e1_s[...], shift, 1)
        r0 = pl.multiple_of(g * R, R)
        cf_rows = a1T_s[pl.ds(r0, R), :]
        cb_rows = a1T_s[pl.ds(SH + r0, R), :]
        df_rows = dtT_s[pl.ds(r0, R), :]
        db_rows = dtT_s[pl.ds(SH + r0, R), :]
        xg = xs_ref[g]
        zg = z_ref[g].astype(f32)
        dg = dexp_ref[g]
        hcur = hs[g].astype(bf16)
        gnext = gssd_ref[0, g]
        ys_parts = []
        for r in range(R):
            arg = jnp.where(lower, a1r[:, r:r + 1] - cf_rows[r:r + 1, :], cb_rows[r:r + 1, :] - a1r[:, SH + r:SH + r + 1])
            wm = jnp.exp(arg) * jnp.where(lower, df_rows[r:r + 1, :], db_rows[r:r + 1, :])
            m16 = (sc * wm).astype(bf16)
            xh = xg[:, r * P:(r + 1) * P]
            y = _dot(m16, xh)
            lc = jnp.concatenate([(cf32 * e1r[:, r:r + 1]).astype(bf16),
                                  (cf32 * e1r[:, SH + r:SH + r + 1]).astype(bf16)], axis=1)
            hcat = jnp.concatenate([hcur[r * P:(r + 1) * P, :], gnext[r * P:(r + 1) * P, :]], axis=1)
            y = y + _nt(lc, hcat)
            ys_parts.append(y)
        yg = jnp.concatenate(ys_parts, axis=1)
        yg = (yg + xg.astype(f32) * dg) * _silu(zg)
        ypre[g] = yg
        ssq = ssq + jnp.sum(yg * yg, axis=1, keepdims=True)
        _ssd_state_update(hs, g, xg.astype(f32), b16, wT_s[pl.ds(r0, R), :], decT_s[pl.ds(r0, R), :])
        return ssq

    ssq = lax.fori_loop(0, NG, grp, jnp.zeros((T, 1), f32))
    rs = lax.rsqrt(ssq * (1.0 / SI) + EPS)

    def fin(g, carry):
        ys_ref[g] = ((ypre[g] * rs) * snw_ref[g]).astype(ys_ref.dtype)
        return carry

    lax.fori_loop(0, NG, fin, 0)

    def head(h, carry):
        q16 = q_ref[h]
        k16 = k_ref[h]
        v16 = v_ref[h]
        m16 = (_nt(q16, k16) * wret_ref[h]).astype(bf16)
        y = _dot(m16, v16)
        qf = q16.astype(f32)
        lc = jnp.concatenate([(qf * ef_ref[h]).astype(bf16), (qf * eb_ref[h]).astype(bf16)], axis=1)
        hcat = jnp.concatenate([hr[h].astype(bf16), gret_ref[0, h]], axis=1)
        y = y + _nt(lc, hcat)
        mu = jnp.mean(y, axis=-1, keepdims=True)
        d = y - mu
        yn = d * lax.rsqrt(jnp.mean(d * d, axis=-1, keepdims=True) + EPS)
        gg = gr_in_ref[h].astype(f32)
        yr_ref[h] = ((yn * rnw_ref[h]) * _silu(gg)).astype(yr_ref.dtype)
        _ret_state_update(hr, h, v16, k16, wf_ref[h], rdec_ref[h][0:1, :])
        return carry

    lax.fori_loop(0, RH, head, 0)


def _fwd_sweep(L, qk, vg, zx, xc, bc, cc, dt, gret, gssd, tabs, bias, arow, dexp, snw, rnw, h0r, h0s):
    nc = L // T
    wret, ef, eb, wf, _, rdec = tabs

    def blk(lead, n, width):
        return pl.BlockSpec((n, T, width), lambda s: (lead, s, 0))

    def whole(a):
        nd = a.ndim
        return pl.BlockSpec(a.shape, lambda s: (0,) * nd)

    small = [wret, ef, eb, wf, rdec, bias, arow, dexp, snw, rnw, h0r, h0s]
    in_specs = [blk(0, RH, HD), blk(1, RH, HD), blk(0, RH, HD), blk(1, RH, HD),
                blk(0, NG, GW), blk(0, NG, GW), blk(0, NG, NS), blk(0, NG, NS),
                pl.BlockSpec((T, 2 * SH), lambda s: (s, 0)),
                pl.BlockSpec((1, RH, HD, HD), lambda s: (s, 0, 0, 0)),
                pl.BlockSpec((1, NG, GW, NS), lambda s: (s, 0, 0, 0))] + [whole(a) for a in small]
    return pl.pallas_call(
        _fwd_kernel,
        grid=(nc,),
        in_specs=in_specs,
        out_specs=[blk(0, RH, HD), blk(0, NG, GW)],
        out_shape=[jax.ShapeDtypeStruct((RH, L, HD), bf16), jax.ShapeDtypeStruct((NG, L, GW), bf16)],
        scratch_shapes=[pltpu.VMEM((RH, HD, HD), f32), pltpu.VMEM((NG, GW, NS), f32), pltpu.VMEM((NG, T, GW), f32),
                        pltpu.VMEM((T, 2 * SH), f32), pltpu.VMEM((T, 2 * SH), f32)]
        + [pltpu.VMEM((2 * SH, T), f32)] * 4,
        compiler_params=_cp(("arbitrary",)),
        name="fwd_sweep",
    )(qk, qk, vg, vg, zx, xc, bc, cc, dt, gret, gssd, *small)


def _branch_out_kernel(yr_ref, ys_ref, wr_ref, ws_ref, gr_ref, gs_ref, o_ref):
    acc_r = _dot(jnp.concatenate([yr_ref[h] for h in range(RH)], axis=1), wr_ref[...])
    acc_s = _dot(jnp.concatenate([ys_ref[g] for g in range(NG)], axis=1), ws_ref[...])
    m = _sigmoid(gr_ref[...].astype(f32)) * acc_r + _sigmoid(gs_ref[...].astype(f32)) * acc_s
    o_ref[...] = m.astype(o_ref.dtype)


def _branch_out(yr, ys, wr, ws, gates):
    L = yr.shape[1]
    tm, tn = min(L, 512), 512
    nj = D // tn
    return pl.pallas_call(
        _branch_out_kernel,
        grid=(L // tm, nj),
        in_specs=[pl.BlockSpec((RH, tm, HD), lambda i, j: (0, i, 0)),
                  pl.BlockSpec((NG, tm, GW), lambda i, j: (0, i, 0)),
                  pl.BlockSpec((D, tn), lambda i, j: (0, j)),
                  pl.BlockSpec((SI, tn), lambda i, j: (0, j)),
                  pl.BlockSpec((tm, tn), lambda i, j: (i, j)),
                  pl.BlockSpec((tm, tn), lambda i, j: (i, nj + j))],
        out_specs=pl.BlockSpec((tm, tn), lambda i, j: (i, j)),
        out_shape=jax.ShapeDtypeStruct((L, D), bf16),
        compiler_params=_cp(("parallel", "parallel")),
        name="branch_out",
    )(yr, ys, wr, ws, gates, gates)


def _resid_kernel(m_ref, w_ref, x_ref, g_ref, o_ref):
    o_ref[...] = x_ref[...] + g_ref[...] * _dot(m_ref[...], w_ref[...])


def _mix_residual(m, w, x, gate):
    L = m.shape[0]
    tm, tn = min(L, 1024), 512
    return pl.pallas_call(
        _resid_kernel,
        grid=(L // tm, D // tn),
        in_specs=[pl.BlockSpec((tm, D), lambda i, j: (i, 0)),
                  pl.BlockSpec((D, tn), lambda i, j: (0, j)),
                  pl.BlockSpec((tm, tn), lambda i, j: (i, j)),
                  pl.BlockSpec((1, tn), lambda i, j: (0, j))],
        out_specs=pl.BlockSpec((tm, tn), lambda i, j: (i, j)),
        out_shape=jax.ShapeDtypeStruct((L, D), f32),
        compiler_params=_cp(("parallel", "parallel")),
        name="mix_residual",
    )(m, w, x, gate)


def _mlp1_kernel(h_ref, nw_ref, sc_ref, sh_ref, w_ref, o_ref, f_s):
    @pl.when(pl.program_id(1) == 0)
    def _():
        f_s[...] = _rms_mod(h_ref[...], nw_ref[...], sc_ref[...], sh_ref[...]).astype(bf16)

    a = jnp.maximum(_dot(f_s[...], w_ref[...]), 0.0)
    o_ref[...] = (a * a).astype(o_ref.dtype)


def _mlp1(h, nw, sc, sh, w):
    L = h.shape[0]
    tm, tn = min(L, 512), 512
    vec = pl.BlockSpec((1, D), lambda i, j: (0, 0))
    return pl.pallas_call(
        _mlp1_kernel,
        grid=(L // tm, DFF // tn),
        in_specs=[pl.BlockSpec((tm, D), lambda i, j: (i, 0)), vec, vec, vec,
                  pl.BlockSpec((D, tn), lambda i, j: (0, j))],
        out_specs=pl.BlockSpec((tm, tn), lambda i, j: (i, j)),
        out_shape=jax.ShapeDtypeStruct((L, DFF), bf16),
        scratch_shapes=[pltpu.VMEM((tm, D), bf16)],
        compiler_params=_cp(("parallel", "arbitrary")),
        name="mlp1",
    )(h, nw, sc, sh, w)


def _mlp2_kernel(a_ref, w_ref, h_ref, g_ref, fw_ref, o_ref, acc, *, nk):
    kk = pl.program_id(1)

    @pl.when(kk == 0)
    def _():
        acc[...] = jnp.zeros_like(acc)

    acc[...] += _dot(a_ref[...], w_ref[...])

    @pl.when(kk == nk - 1)
    def _():
        h2 = h_ref[...] + g_ref[...] * acc[...]
        y = h2 * lax.rsqrt(jnp.mean(h2 * h2, axis=-1, keepdims=True) + EPS)
        o_ref[...] = y * fw_ref[...]


def _mlp2_final(a, w, h, gate, fw):
    L = a.shape[0]
    tm, tk = min(L, 512), 1024
    nk = DFF // tk
    vec = pl.BlockSpec((1, D), lambda i, k: (0, 0))
    return pl.pallas_call(
        functools.partial(_mlp2_kernel, nk=nk),
        grid=(L // tm, nk),
        in_specs=[pl.BlockSpec((tm, tk), lambda i, k: (i, k)),
                  pl.BlockSpec((tk, D), lambda i, k: (k, 0)),
                  pl.BlockSpec((tm, D), lambda i, k: (i, 0)), vec, vec],
        out_specs=pl.BlockSpec((tm, D), lambda i, k: (i, 0)),
        out_shape=jax.ShapeDtypeStruct((L, D), f32),
        scratch_shapes=[pltpu.VMEM((tm, D), f32)],
        compiler_params=_cp(("parallel", "arbitrary")),
        name="mlp2_final",
    )(a, w, h, gate, fw)


def _split_conv(conv_w, conv_b):
    def grp(w, b, width):
        return (jnp.transpose(w.reshape(KC, NG, width), (1, 0, 2)), b.reshape(NG, 1, width))
    cwx, cbx = grp(conv_w[:, :SI], conv_b[:SI], GW)
    cwb, cbb = grp(conv_w[:, SI:SI + NG * NS], conv_b[SI:SI + NG * NS], NS)
    cwc, cbc = grp(conv_w[:, SI + NG * NS:], conv_b[SI + NG * NS:], NS)
    return cwx, cbx, cwb, cbb, cwc, cbc


def kernel(x, c, ctx, c_ctx, w_mod, b_mod, norm1_w, w_in, conv_w, conv_b, ret_decay_logit, ret_norm_w,
           ssd_a_log, ssd_dt_bias, ssd_d, ssd_norm_w, w_ret_out, w_ssd_out, w_o, norm2_w, w_mlp1, w_mlp2,
           final_norm_w):
    depth = w_mod.shape[0]
    assert depth == 1 and x.shape[0] == 1 and x.shape[2] == D
    L = x.shape[1]
    Lc = ctx.shape[1]
    assert L % T == 0 and Lc % T == 0 and L % GRID_W == 0
    xl = x[0]
    xcx = ctx[0]
    ly = 0

    a8 = jnp.zeros((8, D), f32).at[0].set(c[0]).at[1].set(c_ctx)
    mod = _modulation(a8, w_mod[ly], b_mod[ly][None, :])
    sh_a, sc_a, g_a, sh_f, sc_f, g_f = [mod[0:1, i * D:(i + 1) * D] for i in range(6)]
    csh_a, csc_a = mod[1:2, 0:D], mod[1:2, D:2 * D]

    wi = w_in[ly]
    o_q, o_k, o_v, o_g, o_z, o_x = 0, D, 2 * D, 3 * D, 4 * D, 4 * D + SI
    o_b = o_x + SI
    o_c = o_b + NG * NS
    o_dt = o_c + NG * NS
    o_gate = o_dt + 2 * SH
    w_qk = wi[:, o_q:o_v].astype(bf16)
    w_k = wi[:, o_k:o_v].astype(bf16)
    w_vgbc = jnp.concatenate([wi[:, o_v:o_z], wi[:, o_b:o_dt]], axis=1).astype(bf16)
    w_vb = jnp.concatenate([wi[:, o_v:o_g], wi[:, o_b:o_c]], axis=1).astype(bf16)
    w_zx = wi[:, o_z:o_b].astype(bf16)
    w_x = wi[:, o_x:o_b].astype(bf16)
    w_dt = wi[:, o_dt:o_gate].astype(bf16)
    w_gate = wi[:, o_gate:].astype(bf16)

    convw = _split_conv(conv_w[ly], conv_b[ly])
    bias = ssd_dt_bias[ly].reshape(1, 2 * SH)
    arow = (-jnp.exp(ssd_a_log[ly])).reshape(1, 2 * SH)
    dexp = jnp.repeat(ssd_d[ly], P).reshape(NG, 1, GW)
    snw = ssd_norm_w[ly].reshape(NG, 1, GW)
    rnw = ret_norm_w[ly].reshape(RH, 1, HD)
    nw1 = norm1_w[ly][None, :]

    tabs = _ret_tables(ret_decay_logit[ly])
    _, _, _, wf_t, wb_t, rdec = tabs
    zr = jnp.zeros((RH, HD, HD), f32)
    zs = jnp.zeros((NG, GW, NS), f32)

    uc = _norm_mod(xcx, nw1, csc_a, csh_a)
    ident = (jnp.ones((Lc, LANES), f32), jnp.zeros((Lc, LANES), f32))
    kc = _project(uc, w_k, bw=HD, tn=512, rope=ident, scale_from=0, name="ctx_k")
    vbc = _project(uc, w_vb, bw=HD, tn=512, name="ctx_vb")
    xsc = _project(uc, w_x, bw=GW, tn=GW, name="ctx_x")
    dtc = _project(uc, w_dt, bw=0, tn=2 * SH, out_dtype=f32, name="ctx_dt")
    common = dict(k=kc, k_blk=0, v=vbc, v_blk=0, xs=xsc, xs_blk=0, bsrc=vbc, b_blk=RH // NG, c_blk=0, dt=dtc,
                  convw=convw, bias=bias, arow=arow, rdec=rdec, h0r=zr, h0s=zs)
    cfr, cfs = _state_sweep(Lc, fwd=True, emit=False, write_conv=False, rw=wf_t, **common)
    cbr, cbs = _state_sweep(Lc, fwd=False, emit=False, write_conv=False, rw=wb_t, **common)

    u = _norm_mod(xl, nw1, sc_a, sh_a)
    rope = _rope_tables(L)
    qk = _project(u, w_qk, bw=HD, tn=512, rope=rope, scale_from=D // 512, name="lat_qk")
    vg = _project(u, w_vgbc, bw=HD, tn=512, name="lat_vgbc")
    zx = _project(u, w_zx, bw=GW, tn=GW, name="lat_zx")
    gates = _project(u, w_gate, bw=0, tn=512, name="lat_gates")
    dtl = _project(u, w_dt, bw=0, tn=2 * SH, out_dtype=f32, name="lat_dt")

    gret, gssd, xcv, bcv, ccv, _, _ = _state_sweep(
        L, fwd=False, emit=True, write_conv=True, k=qk, k_blk=1, v=vg, v_blk=0, xs=zx, xs_blk=1, bsrc=vg,
        b_blk=2 * RH // NG, c_blk=2 * RH // NG + 1, dt=dtl, convw=convw, bias=bias, arow=arow, rw=wb_t, rdec=rdec,
        h0r=cbr, h0s=cbs)
    yr, ys = _fwd_sweep(L, qk, vg, zx, xcv, bcv, ccv, dtl, gret, gssd, tabs, bias, arow, dexp, snw, rnw, cfr, cfs)

    m = _branch_out(yr, ys, w_ret_out[ly].astype(bf16), w_ssd_out[ly].astype(bf16), gates)
    h1 = _mix_residual(m, w_o[ly].astype(bf16), xl, g_a)
    a = _mlp1(h1, norm2_w[ly][None, :], sc_f, sh_f, w_mlp1[ly].astype(bf16))
    out = _mlp2_final(a, w_mlp2[ly].astype(bf16), h1, g_f, final_norm_w[None, :])
    return out[None]
```

```python
import functools
import math

import jax
import jax.numpy as jnp
from jax import lax
from jax.experimental import pallas as pl
from jax.experimental.pallas import tpu as pltpu

f32 = jnp.float32
bf16 = jnp.bfloat16

D = 2048
T = 128
GRID_W = 64
HD = 128
RH = D // HD
SI = 2 * D
P = 64
SH = SI // P
NS = 128
NG = 8
R = SH // NG
GW = R * P
KC = 5
DFF = 4 * D
ROPE_BASE = 10000.0
EPS = 1e-6
LANES = 128
HALO = 8
VMEM_LIMIT = 48 * 1024 * 1024


def _cp(sem):
    return pltpu.CompilerParams(dimension_semantics=sem, vmem_limit_bytes=VMEM_LIMIT)


def _nt(a, b):
    return lax.dot_general(a, b, (((1,), (1,)), ((), ())), preferred_element_type=f32)


def _dot(a, b):
    return jnp.dot(a, b, preferred_element_type=f32)


def _sigmoid(x):
    return 1.0 / (1.0 + jnp.exp(-x))


def _silu(x):
    return x * _sigmoid(x)


def _softplus(x):
    return jnp.maximum(x, 0.0) + jnp.log1p(jnp.exp(-jnp.abs(x)))


def _log_sigmoid(x):
    return jnp.minimum(x, 0.0) - jnp.log1p(jnp.exp(-jnp.abs(x)))


def _cumsum_rows(la, tri):
    a1 = la.astype(bf16)
    r1 = la - a1.astype(f32)
    a2 = r1.astype(bf16)
    r2 = r1 - a2.astype(f32)
    a3 = r2.astype(bf16)
    return _dot(tri, a1) + _dot(tri, a2) + _dot(tri, a3)


def _tri_incl():
    ii = lax.broadcasted_iota(jnp.int32, (T, T), 0)
    jj = lax.broadcasted_iota(jnp.int32, (T, T), 1)
    return jnp.where(jj <= ii, 1.0, 0.0).astype(bf16)


def _rope_tab_kernel(cos_ref, sin_ref, *, tm):
    i = pl.program_id(0)
    pos = i * tm + lax.broadcasted_iota(jnp.int32, (tm, LANES), 0)
    lane = lax.broadcasted_iota(jnp.int32, (tm, LANES), 1)
    row = (pos // GRID_W).astype(f32)
    col = (pos % GRID_W).astype(f32)
    fidx = (lane % (HD // 4)).astype(f32)
    freqs = jnp.exp(fidx * (-math.log(ROPE_BASE) / (HD // 4)))
    ang = jnp.where(lane < HD // 2, row, col) * freqs
    s = jnp.sin(ang)
    cos_ref[...] = jnp.cos(ang)
    sin_ref[...] = jnp.where((lane % (HD // 2)) < HD // 4, -s, s)


def _rope_tables(L):
    tm = min(L, 512)
    return pl.pallas_call(
        functools.partial(_rope_tab_kernel, tm=tm),
        grid=(L // tm,),
        out_specs=[pl.BlockSpec((tm, LANES), lambda i: (i, 0))] * 2,
        out_shape=[jax.ShapeDtypeStruct((L, LANES), f32)] * 2,
        compiler_params=_cp(("parallel",)),
        name="rope_tables",
    )()


def _ret_tab_kernel(lg_ref, w_ref, ef_ref, eb_ref, wf_ref, wb_ref, dec_ref):
    lf = _log_sigmoid(lg_ref[0])
    lb = _log_sigmoid(lg_ref[1])
    ii = lax.broadcasted_iota(jnp.int32, (T, T), 0)
    jj = lax.broadcasted_iota(jnp.int32, (T, T), 1)
    dl = (ii - jj).astype(f32)
    w_ref[...] = jnp.exp(jnp.where(jj <= ii, dl * lf, -dl * lb))
    idx = lax.broadcasted_iota(jnp.int32, (T, LANES), 0).astype(f32)
    ef_ref[...] = jnp.exp((idx + 1.0) * lf)
    eb_ref[...] = jnp.exp((T - idx) * lb)
    wf_ref[...] = jnp.exp((T - 1.0 - idx) * lf)
    wb_ref[...] = jnp.exp(idx * lb)
    srow = lax.broadcasted_iota(jnp.int32, (8, LANES), 0)
    dec_ref[...] = jnp.where(srow == 0, jnp.exp(T * lf), jnp.exp(T * lb))


def _ret_tables(logit):
    lg = jnp.broadcast_to(logit[:, :, None, None], (2, RH, 1, LANES))
    tab = pl.BlockSpec((None, T, LANES), lambda h: (h, 0, 0))
    return pl.pallas_call(
        _ret_tab_kernel,
        grid=(RH,),
        in_specs=[pl.BlockSpec((2, None, 1, LANES), lambda h: (0, h, 0, 0))],
        out_specs=[tab, tab, tab, tab, tab, pl.BlockSpec((None, 8, LANES), lambda h: (h, 0, 0))],
        out_shape=[jax.ShapeDtypeStruct((RH, T, T), f32)] + [jax.ShapeDtypeStruct((RH, T, LANES), f32)] * 4
        + [jax.ShapeDtypeStruct((RH, 8, LANES), f32)],
        compiler_params=_cp(("parallel",)),
        name="ret_tables",
    )(lg)


def _mod_kernel(a_ref, w_ref, b_ref, o_ref):
    a = _silu(a_ref[...])
    o_ref[...] = _dot(a, w_ref[...]) + b_ref[...]


def _modulation(a8, w, b):
    tn = 1024
    n = w.shape[1]
    return pl.pallas_call(
        _mod_kernel,
        grid=(n // tn,),
        in_specs=[pl.BlockSpec((8, D), lambda j: (0, 0)),
                  pl.BlockSpec((D, tn), lambda j: (0, j)),
                  pl.BlockSpec((1, tn), lambda j: (0, j))],
        out_specs=pl.BlockSpec((8, tn), lambda j: (0, j)),
        out_shape=jax.ShapeDtypeStruct((8, n), f32),
        compiler_params=_cp(("parallel",)),
        name="modulation",
    )(a8, w, b)


def _rms_mod(x, nw, sc, sh):
    y = x * lax.rsqrt(jnp.mean(x * x, axis=-1, keepdims=True) + EPS)
    return (y * nw) * (1.0 + sc) + sh


def _norm_kernel(x_ref, nw_ref, sc_ref, sh_ref, o_ref):
    o_ref[...] = _rms_mod(x_ref[...], nw_ref[...], sc_ref[...], sh_ref[...]).astype(o_ref.dtype)


def _norm_mod(x, nw, sc, sh):
    L = x.shape[0]
    tm = min(L, 256)
    vec = pl.BlockSpec((1, D), lambda i: (0, 0))
    return pl.pallas_call(
        _norm_kernel,
        grid=(L // tm,),
        in_specs=[pl.BlockSpec((tm, D), lambda i: (i, 0)), vec, vec, vec],
        out_specs=pl.BlockSpec((tm, D), lambda i: (i, 0)),
        out_shape=jax.ShapeDtypeStruct((L, D), bf16),
        compiler_params=_cp(("parallel",)),
        name="norm_mod",
    )(x, nw, sc, sh)


def _proj_kernel(u_ref, w_ref, *rest, bw, tn, rope, scale_from):
    acc = _dot(u_ref[...], w_ref[...])
    if rope:
        cos_ref, sin_ref, o_ref = rest
        scale = jnp.where(pl.program_id(1) >= scale_from, HD ** -0.5, 1.0).astype(f32)
        cos = cos_ref[...]
        sin = sin_ref[...]
        lane = lax.broadcasted_iota(jnp.int32, cos.shape, 1)
        first = (lane % (HD // 2)) < HD // 4
        for hh in range(tn // HD):
            t = acc[:, hh * HD:(hh + 1) * HD] * scale
            partner = jnp.where(first, pltpu.roll(t, HD - HD // 4, 1), pltpu.roll(t, HD // 4, 1))
            o_ref[hh] = (t * cos + partner * sin).astype(o_ref.dtype)
    else:
        (o_ref,) = rest
        if bw:
            for hh in range(tn // bw):
                o_ref[hh] = acc[:, hh * bw:(hh + 1) * bw].astype(o_ref.dtype)
        else:
            o_ref[...] = acc.astype(o_ref.dtype)


def _project(u, w, *, bw, tn, out_dtype=bf16, rope=None, scale_from=0, name):
    M = u.shape[0]
    N = w.shape[1]
    tm = min(M, 1024)
    in_specs = [pl.BlockSpec((tm, D), lambda i, j: (i, 0)), pl.BlockSpec((D, tn), lambda i, j: (0, j))]
    args = [u, w]
    if rope is not None:
        in_specs += [pl.BlockSpec((tm, LANES), lambda i, j: (i, 0))] * 2
        args += list(rope)
    if bw:
        out_spec = pl.BlockSpec((tn // bw, tm, bw), lambda i, j: (j, i, 0))
        out_shape = jax.ShapeDtypeStruct((N // bw, M, bw), out_dtype)
    else:
        out_spec = pl.BlockSpec((tm, tn), lambda i, j: (i, j))
        out_shape = jax.ShapeDtypeStruct((M, N), out_dtype)
    return pl.pallas_call(
        functools.partial(_proj_kernel, bw=bw, tn=tn, rope=rope is not None, scale_from=scale_from),
        grid=(M // tm, N // tn),
        in_specs=in_specs,
        out_specs=out_spec,
        out_shape=out_shape,
        compiler_params=_cp(("parallel", "parallel")),
        name=name,
    )(*args)


def _conv_silu(ext_ref, main, prev, nxt, cw, cb, has_prev, has_next, width):
    ext_ref[0:HALO, 0:width] = jnp.where(has_prev, prev.astype(f32), 0.0)
    ext_ref[HALO:HALO + T, 0:width] = main.astype(f32)
    ext_ref[HALO + T:2 * HALO + T, 0:width] = jnp.where(has_next, nxt.astype(f32), 0.0)
    acc = cb
    for j in range(KC):
        acc = acc + ext_ref[pl.ds(HALO - KC // 2 + j, T), 0:width] * cw[j:j + 1, :]
    return _silu(acc)


def _decay_prologue(dt_ref, bias_ref, arow_ref):
    dt = _softplus(dt_ref[...] + bias_ref[...])
    la = dt * arow_ref[...]
    acs = _cumsum_rows(la, _tri_incl())
    tot = acs[T - 1:T, :]
    return dt, la, acs, tot


def _transpose_blocks(x):
    n = x.shape[1] // LANES
    return jnp.concatenate([x[:, b * LANES:(b + 1) * LANES].T for b in range(n)], axis=0)


def _ssd_state_update(hs_ref, g, xc, bc16, w_rows, dec_rows):
    xT = _transpose_blocks(xc)
    lhs = jnp.concatenate(
        [(xT[r * P:(r + 1) * P, :] * w_rows[r:r + 1, :]).astype(bf16) for r in range(R)], axis=0)
    upd = _dot(lhs, bc16)
    old = hs_ref[g]
    hs_ref[g] = jnp.concatenate(
        [old[r * P:(r + 1) * P, :] * dec_rows[r:r + 1, :] + upd[r * P:(r + 1) * P, :] for r in range(R)], axis=0)


def _ret_state_update(hr_ref, h, v16, k16, wcol, dec_row):
    vw = (v16.astype(f32) * wcol).T.astype(bf16)
    hr_ref[h] = hr_ref[h] * dec_row + _dot(vw, k16)


def _state_kernel(*refs, nc, fwd, emit, write_conv):
    it = iter(refs)
    k_ref, v_ref = next(it), next(it)
    xs_ref, xs_p, xs_n = next(it), next(it), next(it)
    b_ref, b_p, b_n = next(it), next(it), next(it)
    if write_conv:
        c_ref, c_p, c_n = next(it), next(it), next(it)
    dt_ref = next(it)
    cwx, cbx, cwb, cbb = next(it), next(it), next(it), next(it)
    if write_conv:
        cwc, cbc = next(it), next(it)
    bias_ref, arow_ref = next(it), next(it)
    rw_ref, rdec_ref = next(it), next(it)
    h0r_ref, h0s_ref = next(it), next(it)
    if emit:
        gr_ref, gs_ref = next(it), next(it)
    if write_conv:
        xo_ref, bo_ref, co_ref = next(it), next(it), next(it)
    hfr_ref, hfs_ref = next(it), next(it)
    hr, hs, ext, wT_s, decT_s = next(it), next(it), next(it), next(it), next(it)

    s = pl.program_id(0)
    c = s if fwd else nc - 1 - s
    has_prev = c > 0
    has_next = c < nc - 1

    @pl.when(s == 0)
    def _():
        hr[...] = h0r_ref[...]
        hs[...] = h0s_ref[...]

    if emit:
        gr_ref[0] = hr[...].astype(bf16)
        gs_ref[0] = hs[...].astype(bf16)

    dt, la, acs, tot = _decay_prologue(dt_ref, bias_ref, arow_ref)
    wexp = jnp.exp(tot - acs) if fwd else jnp.exp(acs - la)
    wT_s[...] = (wexp * dt).T
    decT_s[...] = jnp.broadcast_to(jnp.exp(tot), (T, 2 * SH)).T
    off = 0 if fwd else SH

    def grp(g, carry):
        xc = _conv_silu(ext, xs_ref[g], xs_p[g], xs_n[g], cwx[g], cbx[g], has_prev, has_next, GW)
        bc = _conv_silu(ext, b_ref[g], b_p[g], b_n[g], cwb[g], cbb[g], has_prev, has_next, NS)
        xc16 = xc.astype(bf16)
        bc16 = bc.astype(bf16)
        if write_conv:
            cc = _conv_silu(ext, c_ref[g], c_p[g], c_n[g], cwc[g], cbc[g], has_prev, has_next, NS)
            xo_ref[g] = xc16
            bo_ref[g] = bc16
            co_ref[g] = cc.astype(bf16)
        r0 = pl.multiple_of(off + g * R, R)
        _ssd_state_update(hs, g, xc16.astype(f32), bc16, wT_s[pl.ds(r0, R), :], decT_s[pl.ds(r0, R), :])
        return carry

    lax.fori_loop(0, NG, grp, 0)

    dsel = 0 if fwd else 1

    def head(h, carry):
        _ret_state_update(hr, h, v_ref[h], k_ref[h], rw_ref[h], rdec_ref[h][dsel:dsel + 1, :])
        return carry

    lax.fori_loop(0, RH, head, 0, unroll=4)

    @pl.when(s == nc - 1)
    def _():
        hfr_ref[...] = hr[...]
        hfs_ref[...] = hs[...]


def _state_sweep(L, *, fwd, emit, write_conv, k, k_blk, v, v_blk, xs, xs_blk, bsrc, b_blk, c_blk, dt,
                 convw, bias, arow, rw, rdec, h0r, h0s):
    nc = L // T
    tb = T // HALO
    nrb = L // HALO

    def cidx(s):
        return s if fwd else nc - 1 - s

    def main(lead, n, width):
        return pl.BlockSpec((n, T, width), lambda s: (lead, cidx(s), 0))

    def prev(lead, n, width):
        return pl.BlockSpec((n, HALO, width), lambda s: (lead, jnp.maximum(cidx(s) * tb - 1, 0), 0))

    def nxt(lead, n, width):
        return pl.BlockSpec((n, HALO, width), lambda s: (lead, jnp.minimum((cidx(s) + 1) * tb, nrb - 1), 0))

    def whole(a):
        nd = a.ndim
        return pl.BlockSpec(a.shape, lambda s: (0,) * nd)

    cwx, cbx, cwb, cbb, cwc, cbc = convw
    in_specs = [main(k_blk, RH, HD), main(v_blk, RH, HD),
                main(xs_blk, NG, GW), prev(xs_blk, NG, GW), nxt(xs_blk, NG, GW),
                main(b_blk, NG, NS), prev(b_blk, NG, NS), nxt(b_blk, NG, NS)]
    args = [k, v, xs, xs, xs, bsrc, bsrc, bsrc]
    if write_conv:
        in_specs += [main(c_blk, NG, NS), prev(c_blk, NG, NS), nxt(c_blk, NG, NS)]
        args += [bsrc, bsrc, bsrc]
    in_specs += [pl.BlockSpec((T, 2 * SH), lambda s: (cidx(s), 0))]
    args += [dt]
    small = [cwx, cbx, cwb, cbb] + ([cwc, cbc] if write_conv else []) + [bias, arow, rw, rdec, h0r, h0s]
    in_specs += [whole(a) for a in small]
    args += small

    out_specs, out_shape = [], []
    if emit:
        out_specs += [pl.BlockSpec((1, RH, HD, HD), lambda s: (cidx(s), 0, 0, 0)),
                      pl.BlockSpec((1, NG, GW, NS), lambda s: (cidx(s), 0, 0, 0))]
        out_shape += [jax.ShapeDtypeStruct((nc, RH, HD, HD), bf16), jax.ShapeDtypeStruct((nc, NG, GW, NS), bf16)]
    if write_conv:
        out_specs += [pl.BlockSpec((NG, T, GW), lambda s: (0, cidx(s), 0)),
                      pl.BlockSpec((NG, T, NS), lambda s: (0, cidx(s), 0)),
                      pl.BlockSpec((NG, T, NS), lambda s: (0, cidx(s), 0))]
        out_shape += [jax.ShapeDtypeStruct((NG, L, GW), bf16), jax.ShapeDtypeStruct((NG, L, NS), bf16),
                      jax.ShapeDtypeStruct((NG, L, NS), bf16)]
    out_specs += [pl.BlockSpec((RH, HD, HD), lambda s: (0, 0, 0)), pl.BlockSpec((NG, GW, NS), lambda s: (0, 0, 0))]
    out_shape += [jax.ShapeDtypeStruct((RH, HD, HD), f32), jax.ShapeDtypeStruct((NG, GW, NS), f32)]

    return pl.pallas_call(
        functools.partial(_state_kernel, nc=nc, fwd=fwd, emit=emit, write_conv=write_conv),
        grid=(nc,),
        in_specs=in_specs,
        out_specs=out_specs,
        out_shape=out_shape,
        scratch_shapes=[pltpu.VMEM((RH, HD, HD), f32), pltpu.VMEM((NG, GW, NS), f32),
                        pltpu.VMEM((T + 2 * HALO, GW), f32),
                        pltpu.VMEM((2 * SH, T), f32), pltpu.VMEM((2 * SH, T), f32)],
        compiler_params=_cp(("arbitrary",)),
        name="state_sweep_" + ("f" if fwd else "b") + ("_emit" if emit else ""),
    )(*args)


def _fwd_kernel(q_ref, k_ref, v_ref, gr_in_ref, z_ref, xs_ref, b_ref, c_ref, dt_ref, gret_ref, gssd_ref,
                wret_ref, ef_ref, eb_ref, wf_ref, rdec_ref, bias_ref, arow_ref, dexp_ref, snw_ref, rnw_ref,
                h0r_ref, h0s_ref, yr_ref, ys_ref,
                hr, hs, ypre, a1_s, e1_s, a1T_s, dtT_s, wT_s, decT_s):
    s = pl.program_id(0)

    @pl.when(s == 0)
    def _():
        hr[...] = h0r_ref[...]
        hs[...] = h0s_ref[...]

    dt, la, acs, tot = _decay_prologue(dt_ref, bias_ref, arow_ref)
    lane = lax.broadcasted_iota(jnp.int32, (T, 2 * SH), 1)
    is_f = lane < SH
    a1 = jnp.where(is_f, acs, acs - la)
    e1 = jnp.exp(jnp.where(is_f, acs, tot - (acs - la)))
    a1_s[...] = a1
    e1_s[...] = e1
    a1T_s[...] = a1.T
    dtT_s[...] = dt.T
    wT_s[...] = (jnp.exp(tot - acs) * dt).T
    decT_s[...] = jnp.broadcast_to(jnp.exp(tot), (T, 2 * SH)).T

    ii = lax.broadcasted_iota(jnp.int32, (T, T), 0)
    jj = lax.broadcasted_iota(jnp.int32, (T, T), 1)
    lower = jj <= ii

    def grp(g, ssq):
        c16 = c_ref[g]
        b16 = b_ref[g]
        sc = _nt(c16, b16)
        cf32 = c16.astype(f32)
        shift = (2 * SH - g * R) % (2 * SH)
        a1r = pltpu.roll(a1_s[...], shift, 1)
        e1r = pltpu.roll(e1_s[...], shift, 1)
        r0 = pl.multiple_of(g * R, R)
        cf_rows = a1T_s[pl.ds(r0, R), :]
        cb_rows = a1T_s[pl.ds(SH + r0, R), :]
        df_rows = dtT_s[pl.ds(r0, R), :]
        db_rows = dtT_s[pl.ds(SH + r0, R), :]
        xg = xs_ref[g]
        zg = z_ref[g].astype(f32)
        dg = dexp_ref[g]
        hcur = hs[g].astype(bf16)
        gnext = gssd_ref[0, g]
        ys_parts = []
        for r in range(R):
            arg = jnp.where(lower, a1r[:, r:r + 1] - cf_rows[r:r + 1, :], cb_rows[r:r + 1, :] - a1r[:, SH + r:SH + r + 1])
            wm = jnp.exp(arg) * jnp.where(lower, df_rows[r:r + 1, :], db_rows[r:r + 1, :])
            m16 = (sc * wm).astype(bf16)
            xh = xg[:, r * P:(r + 1) * P]
            y = _dot(m16, xh)
            lc = jnp.concatenate([(cf32 * e1r[:, r:r + 1]).astype(bf16),
                                  (cf32 * e1r[:, SH + r:SH + r + 1]).astype(bf16)], axis=1)
            hcat = jnp.concatenate([hcur[r * P:(r + 1) * P, :], gnext[r * P:(r + 1) * P, :]], axis=1)
            y = y + _nt(lc, hcat)
            ys_parts.append(y)
        yg = jnp.concatenate(ys_parts, axis=1)
        yg = (yg + xg.astype(f32) * dg) * _silu(zg)
        ypre[g] = yg
        ssq = ssq + jnp.sum(yg * yg, axis=1, keepdims=True)
        _ssd_state_update(hs, g, xg.astype(f32), b16, wT_s[pl.ds(r0, R), :], decT_s[pl.ds(r0, R), :])
        return ssq

    ssq = lax.fori_loop(0, NG, grp, jnp.zeros((T, 1), f32))
    rs = lax.rsqrt(ssq * (1.0 / SI) + EPS)

    def fin(g, carry):
        ys_ref[g] = ((ypre[g] * rs) * snw_ref[g]).astype(ys_ref.dtype)
        return carry

    lax.fori_loop(0, NG, fin, 0)

    def head(h, carry):
        q16 = q_ref[h]
        k16 = k_ref[h]
        v16 = v_ref[h]
        m16 = (_nt(q16, k16) * wret_ref[h]).astype(bf16)
        y = _dot(m16, v16)
        qf = q16.astype(f32)
        lc = jnp.concatenate([(qf * ef_ref[h]).astype(bf16), (qf * eb_ref[h]).astype(bf16)], axis=1)
        hcat = jnp.concatenate([hr[h].astype(bf16), gret_ref[0, h]], axis=1)
        y = y + _nt(lc, hcat)
        mu = jnp.mean(y, axis=-1, keepdims=True)
        d = y - mu
        yn = d * lax.rsqrt(jnp.mean(d * d, axis=-1, keepdims=True) + EPS)
        gg = gr_in_ref[h].astype(f32)
        yr_ref[h] = ((yn * rnw_ref[h]) * _silu(gg)).astype(yr_ref.dtype)
        _ret_state_update(hr, h, v16, k16, wf_ref[h], rdec_ref[h][0:1, :])
        return carry

    lax.fori_loop(0, RH, head, 0, unroll=4)


def _fwd_sweep(L, qk, vg, zx, xc, bc, cc, dt, gret, gssd, tabs, bias, arow, dexp, snw, rnw, h0r, h0s):
    nc = L // T
    wret, ef, eb, wf, _, rdec = tabs

    def blk(lead, n, width):
        return pl.BlockSpec((n, T, width), lambda s: (lead, s, 0))

    def whole(a):
        nd = a.ndim
        return pl.BlockSpec(a.shape, lambda s: (0,) * nd)

    small = [wret, ef, eb, wf, rdec, bias, arow, dexp, snw, rnw, h0r, h0s]
    in_specs = [blk(0, RH, HD), blk(1, RH, HD), blk(0, RH, HD), blk(1, RH, HD),
                blk(0, NG, GW), blk(0, NG, GW), blk(0, NG, NS), blk(0, NG, NS),
                pl.BlockSpec((T, 2 * SH), lambda s: (s, 0)),
                pl.BlockSpec((1, RH, HD, HD), lambda s: (s, 0, 0, 0)),
                pl.BlockSpec((1, NG, GW, NS), lambda s: (s, 0, 0, 0))] + [whole(a) for a in small]
    return pl.pallas_call(
        _fwd_kernel,
        grid=(nc,),
        in_specs=in_specs,
        out_specs=[blk(0, RH, HD), blk(0, NG, GW)],
        out_shape=[jax.ShapeDtypeStruct((RH, L, HD), bf16), jax.ShapeDtypeStruct((NG, L, GW), bf16)],
        scratch_shapes=[pltpu.VMEM((RH, HD, HD), f32), pltpu.VMEM((NG, GW, NS), f32), pltpu.VMEM((NG, T, GW), f32),
                        pltpu.VMEM((T, 2 * SH), f32), pltpu.VMEM((T, 2 * SH), f32)]
        + [pltpu.VMEM((2 * SH, T), f32)] * 4,
        compiler_params=_cp(("arbitrary",)),
        name="fwd_sweep",
    )(qk, qk, vg, vg, zx, xc, bc, cc, dt, gret, gssd, *small)


def _branch_out_kernel(yr_ref, ys_ref, wr_ref, ws_ref, gr_ref, gs_ref, o_ref):
    acc_r = _dot(jnp.concatenate([yr_ref[h] for h in range(RH)], axis=1), wr_ref[...])
    acc_s = _dot(jnp.concatenate([ys_ref[g] for g in range(NG)], axis=1), ws_ref[...])
    m = _sigmoid(gr_ref[...].astype(f32)) * acc_r + _sigmoid(gs_ref[...].astype(f32)) * acc_s
    o_ref[...] = m.astype(o_ref.dtype)


def _branch_out(yr, ys, wr, ws, gates):
    L = yr.shape[1]
    tm, tn = min(L, 512), 512
    nj = D // tn
    return pl.pallas_call(
        _branch_out_kernel,
        grid=(L // tm, nj),
        in_specs=[pl.BlockSpec((RH, tm, HD), lambda i, j: (0, i, 0)),
                  pl.BlockSpec((NG, tm, GW), lambda i, j: (0, i, 0)),
                  pl.BlockSpec((D, tn), lambda i, j: (0, j)),
                  pl.BlockSpec((SI, tn), lambda i, j: (0, j)),
                  pl.BlockSpec((tm, tn), lambda i, j: (i, j)),
                  pl.BlockSpec((tm, tn), lambda i, j: (i, nj + j))],
        out_specs=pl.BlockSpec((tm, tn), lambda i, j: (i, j)),
        out_shape=jax.ShapeDtypeStruct((L, D), bf16),
        compiler_params=_cp(("parallel", "parallel")),
        name="branch_out",
    )(yr, ys, wr, ws, gates, gates)


def _resid_kernel(m_ref, w_ref, x_ref, g_ref, o_ref):
    o_ref[...] = x_ref[...] + g_ref[...] * _dot(m_ref[...], w_ref[...])


def _mix_residual(m, w, x, gate):
    L = m.shape[0]
    tm, tn = min(L, 1024), 1024
    return pl.pallas_call(
        _resid_kernel,
        grid=(L // tm, D // tn),
        in_specs=[pl.BlockSpec((tm, D), lambda i, j: (i, 0)),
                  pl.BlockSpec((D, tn), lambda i, j: (0, j)),
                  pl.BlockSpec((tm, tn), lambda i, j: (i, j)),
                  pl.BlockSpec((1, tn), lambda i, j: (0, j))],
        out_specs=pl.BlockSpec((tm, tn), lambda i, j: (i, j)),
        out_shape=jax.ShapeDtypeStruct((L, D), f32),
        compiler_params=_cp(("parallel", "parallel")),
        name="mix_residual",
    )(m, w, x, gate)


def _mlp1_kernel(h_ref, nw_ref, sc_ref, sh_ref, w_ref, o_ref, f_s):
    @pl.when(pl.program_id(1) == 0)
    def _():
        f_s[...] = _rms_mod(h_ref[...], nw_ref[...], sc_ref[...], sh_ref[...]).astype(bf16)

    a = jnp.maximum(_dot(f_s[...], w_ref[...]), 0.0)
    o_ref[...] = (a * a).astype(o_ref.dtype)


def _mlp1(h, nw, sc, sh, w):
    L = h.shape[0]
    tm, tn = min(L, 1024), 1024
    vec = pl.BlockSpec((1, D), lambda i, j: (0, 0))
    return pl.pallas_call(
        _mlp1_kernel,
        grid=(L // tm, DFF // tn),
        in_specs=[pl.BlockSpec((tm, D), lambda i, j: (i, 0)), vec, vec, vec,
                  pl.BlockSpec((D, tn), lambda i, j: (0, j))],
        out_specs=pl.BlockSpec((tm, tn), lambda i, j: (i, j)),
        out_shape=jax.ShapeDtypeStruct((L, DFF), bf16),
        scratch_shapes=[pltpu.VMEM((tm, D), bf16)],
        compiler_params=_cp(("parallel", "arbitrary")),
        name="mlp1",
    )(h, nw, sc, sh, w)


def _mlp2_kernel(a_ref, w_ref, h_ref, g_ref, fw_ref, o_ref, acc, *, nk):
    kk = pl.program_id(1)

    @pl.when(kk == 0)
    def _():
        acc[...] = jnp.zeros_like(acc)

    acc[...] += _dot(a_ref[...], w_ref[...])

    @pl.when(kk == nk - 1)
    def _():
        h2 = h_ref[...] + g_ref[...] * acc[...]
        y = h2 * lax.rsqrt(jnp.mean(h2 * h2, axis=-1, keepdims=True) + EPS)
        o_ref[...] = y * fw_ref[...]


def _mlp2_final(a, w, h, gate, fw):
    L = a.shape[0]
    tm, tk = min(L, 512), 1024
    nk = DFF // tk
    vec = pl.BlockSpec((1, D), lambda i, k: (0, 0))
    return pl.pallas_call(
        functools.partial(_mlp2_kernel, nk=nk),
        grid=(L // tm, nk),
        in_specs=[pl.BlockSpec((tm, tk), lambda i, k: (i, k)),
                  pl.BlockSpec((tk, D), lambda i, k: (k, 0)),
                  pl.BlockSpec((tm, D), lambda i, k: (i, 0)), vec, vec],
        out_specs=pl.BlockSpec((tm, D), lambda i, k: (i, 0)),
        out_shape=jax.ShapeDtypeStruct((L, D), f32),
        scratch_shapes=[pltpu.VMEM((tm, D), f32)],
        compiler_params=_cp(("parallel", "arbitrary")),
        name="mlp2_final",
    )(a, w, h, gate, fw)


def _split_conv(conv_w, conv_b):
    def grp(w, b, width):
        return (jnp.transpose(w.reshape(KC, NG, width), (1, 0, 2)), b.reshape(NG, 1, width))
    cwx, cbx = grp(conv_w[:, :SI], conv_b[:SI], GW)
    cwb, cbb = grp(conv_w[:, SI:SI + NG * NS], conv_b[SI:SI + NG * NS], NS)
    cwc, cbc = grp(conv_w[:, SI + NG * NS:], conv_b[SI + NG * NS:], NS)
    return cwx, cbx, cwb, cbb, cwc, cbc


def kernel(x, c, ctx, c_ctx, w_mod, b_mod, norm1_w, w_in, conv_w, conv_b, ret_decay_logit, ret_norm_w,
           ssd_a_log, ssd_dt_bias, ssd_d, ssd_norm_w, w_ret_out, w_ssd_out, w_o, norm2_w, w_mlp1, w_mlp2,
           final_norm_w):
    depth = w_mod.shape[0]
    assert depth == 1 and x.shape[0] == 1 and x.shape[2] == D
    L = x.shape[1]
    Lc = ctx.shape[1]
    assert L % T == 0 and Lc % T == 0 and L % GRID_W == 0
    xl = x[0]
    xcx = ctx[0]
    ly = 0

    a8 = jnp.zeros((8, D), f32).at[0].set(c[0]).at[1].set(c_ctx)
    mod = _modulation(a8, w_mod[ly], b_mod[ly][None, :])
    sh_a, sc_a, g_a, sh_f, sc_f, g_f = [mod[0:1, i * D:(i + 1) * D] for i in range(6)]
    csh_a, csc_a = mod[1:2, 0:D], mod[1:2, D:2 * D]

    wi = w_in[ly]
    o_q, o_k, o_v, o_g, o_z, o_x = 0, D, 2 * D, 3 * D, 4 * D, 4 * D + SI
    o_b = o_x + SI
    o_c = o_b + NG * NS
    o_dt = o_c + NG * NS
    o_gate = o_dt + 2 * SH
    w_qk = wi[:, o_q:o_v].astype(bf16)
    w_k = wi[:, o_k:o_v].astype(bf16)
    w_vgbc = jnp.concatenate([wi[:, o_v:o_z], wi[:, o_b:o_dt]], axis=1).astype(bf16)
    w_vb = jnp.concatenate([wi[:, o_v:o_g], wi[:, o_b:o_c]], axis=1).astype(bf16)
    w_zx = wi[:, o_z:o_b].astype(bf16)
    w_x = wi[:, o_x:o_b].astype(bf16)
    w_dt = wi[:, o_dt:o_gate].astype(bf16)
    w_gate = wi[:, o_gate:].astype(bf16)

    convw = _split_conv(conv_w[ly], conv_b[ly])
    bias = ssd_dt_bias[ly].reshape(1, 2 * SH)
    arow = (-jnp.exp(ssd_a_log[ly])).reshape(1, 2 * SH)
    dexp = jnp.repeat(ssd_d[ly], P).reshape(NG, 1, GW)
    snw = ssd_norm_w[ly].reshape(NG, 1, GW)
    rnw = ret_norm_w[ly].reshape(RH, 1, HD)
    nw1 = norm1_w[ly][None, :]

    tabs = _ret_tables(ret_decay_logit[ly])
    _, _, _, wf_t, wb_t, rdec = tabs
    zr = jnp.zeros((RH, HD, HD), f32)
    zs = jnp.zeros((NG, GW, NS), f32)

    uc = _norm_mod(xcx, nw1, csc_a, csh_a)
    ident = (jnp.ones((Lc, LANES), f32), jnp.zeros((Lc, LANES), f32))
    kc = _project(uc, w_k, bw=HD, tn=512, rope=ident, scale_from=0, name="ctx_k")
    vbc = _project(uc, w_vb, bw=HD, tn=512, name="ctx_vb")
    xsc = _project(uc, w_x, bw=GW, tn=GW, name="ctx_x")
    dtc = _project(uc, w_dt, bw=0, tn=2 * SH, out_dtype=f32, name="ctx_dt")
    common = dict(k=kc, k_blk=0, v=vbc, v_blk=0, xs=xsc, xs_blk=0, bsrc=vbc, b_blk=RH // NG, c_blk=0, dt=dtc,
                  convw=convw, bias=bias, arow=arow, rdec=rdec, h0r=zr, h0s=zs)
    cfr, cfs = _state_sweep(Lc, fwd=True, emit=False, write_conv=False, rw=wf_t, **common)
    cbr, cbs = _state_sweep(Lc, fwd=False, emit=False, write_conv=False, rw=wb_t, **common)

    u = _norm_mod(xl, nw1, sc_a, sh_a)
    rope = _rope_tables(L)
    qk = _project(u, w_qk, bw=HD, tn=1024, rope=rope, scale_from=D // 1024, name="lat_qk")
    vg = _project(u, w_vgbc, bw=HD, tn=1024, name="lat_vgbc")
    zx = _project(u, w_zx, bw=GW, tn=1024, name="lat_zx")
    gates = _project(u, w_gate, bw=0, tn=1024, name="lat_gates")
    dtl = _project(u, w_dt, bw=0, tn=2 * SH, out_dtype=f32, name="lat_dt")

    gret, gssd, xcv, bcv, ccv, _, _ = _state_sweep(
        L, fwd=False, emit=True, write_conv=True, k=qk, k_blk=1, v=vg, v_blk=0, xs=zx, xs_blk=1, bsrc=vg,
        b_blk=2 * RH // NG, c_blk=2 * RH // NG + 1, dt=dtl, convw=convw, bias=bias, arow=arow, rw=wb_t, rdec=rdec,
        h0r=cbr, h0s=cbs)
    yr, ys = _fwd_sweep(L, qk, vg, zx, xcv, bcv, ccv, dtl, gret, gssd, tabs, bias, arow, dexp, snw, rnw, cfr, cfs)

    m = _branch_out(yr, ys, w_ret_out[ly].astype(bf16), w_ssd_out[ly].astype(bf16), gates)
    h1 = _mix_residual(m, w_o[ly].astype(bf16), xl, g_a)
    a = _mlp1(h1, norm2_w[ly][None, :], sc_f, sh_f, w_mlp1[ly].astype(bf16))
    out = _mlp2_final(a, w_mlp2[ly].astype(bf16), h1, g_f, final_norm_w[None, :])
    return out[None]
```

```python
import functools
import math

import numpy as np
import jax
import jax.numpy as jnp
from jax import lax
from jax.experimental import pallas as pl
from jax.experimental.pallas import tpu as pltpu

f32 = jnp.float32
bf16 = jnp.bfloat16

D = 2048
T = 128
GRID_W = 64
HD = 128
RH = D // HD
SI = 2 * D
P = 64
SH = SI // P
NS = 128
NG = 8
R = SH // NG
GW = R * P
KC = 5
DFF = 4 * D
ROPE_BASE = 10000.0
EPS = 1e-6
LANES = 128
HALO = 16
LOG2E = 1.0 / math.log(2.0)
LOG2_FLOOR = -300.0
ONES_LANE = 96
VMEM_LIMIT = 48 * 1024 * 1024


def _cp(sem):
    return pltpu.CompilerParams(dimension_semantics=sem, vmem_limit_bytes=VMEM_LIMIT)


def _nt(a, b):
    return lax.dot_general(a, b, (((1,), (1,)), ((), ())), preferred_element_type=f32)


def _dot(a, b):
    return jnp.dot(a, b, preferred_element_type=f32)


def _sigmoid(x):
    return 0.5 + 0.5 * jnp.tanh(0.5 * x)


def _silu(x):
    h = 0.5 * x
    return h + h * jnp.tanh(h)


def _softplus(x):
    return jnp.maximum(x, 0.0) + jnp.log1p(jnp.exp(-jnp.abs(x)))


def _log_sigmoid(x):
    return jnp.minimum(x, 0.0) - jnp.log1p(jnp.exp(-jnp.abs(x)))


def _split3(x):
    p1 = x.astype(bf16).astype(f32)
    r1 = x - p1
    p2 = r1.astype(bf16).astype(f32)
    p3 = (r1 - p2).astype(bf16).astype(f32)
    return p1, p2, p3


def _cumsum_rows(la, tri):
    a1, a2, a3 = _split3(la)
    return _dot(tri, a1.astype(bf16)) + _dot(tri, a2.astype(bf16)) + _dot(tri, a3.astype(bf16))


def _selector_constants():
    dsel = np.zeros((R, 2 * SH, 2 * T), np.float32)
    esel = np.zeros((2 * SH, 2 * GW), np.float32)
    for r in range(R):
        for k in range(3):
            dsel[r, k * R + r, :T] = 1.0
            dsel[r, SH + k * R + r, T:] = -1.0
            esel[k * R + r, r * P:(r + 1) * P] = 1.0
            esel[SH + k * R + r, GW + r * P:GW + (r + 1) * P] = 1.0
    return jnp.asarray(dsel, bf16), jnp.asarray(esel, bf16)


def _tri_incl():
    ii = lax.broadcasted_iota(jnp.int32, (T, T), 0)
    jj = lax.broadcasted_iota(jnp.int32, (T, T), 1)
    return jnp.where(jj <= ii, 1.0, 0.0).astype(bf16)


def _rope_tab_kernel(cos_ref, sin_ref, *, tm):
    i = pl.program_id(0)
    pos = i * tm + lax.broadcasted_iota(jnp.int32, (tm, LANES), 0)
    lane = lax.broadcasted_iota(jnp.int32, (tm, LANES), 1)
    row = (pos // GRID_W).astype(f32)
    col = (pos % GRID_W).astype(f32)
    fidx = (lane % (HD // 4)).astype(f32)
    freqs = jnp.exp(fidx * (-math.log(ROPE_BASE) / (HD // 4)))
    ang = jnp.where(lane < HD // 2, row, col) * freqs
    s = jnp.sin(ang)
    cos_ref[...] = jnp.cos(ang)
    sin_ref[...] = jnp.where((lane % (HD // 2)) < HD // 4, -s, s)


def _rope_tables(L):
    tm = min(L, 512)
    return pl.pallas_call(
        functools.partial(_rope_tab_kernel, tm=tm),
        grid=(L // tm,),
        out_specs=[pl.BlockSpec((tm, LANES), lambda i: (i, 0))] * 2,
        out_shape=[jax.ShapeDtypeStruct((L, LANES), f32)] * 2,
        compiler_params=_cp(("parallel",)),
        name="rope_tables",
    )()


def _ret_tab_kernel(lg_ref, w_ref, ef_ref, eb_ref, wf_ref, wb_ref, dec_ref):
    lf = _log_sigmoid(lg_ref[0])
    lb = _log_sigmoid(lg_ref[1])
    ii = lax.broadcasted_iota(jnp.int32, (T, T), 0)
    jj = lax.broadcasted_iota(jnp.int32, (T, T), 1)
    dl = (ii - jj).astype(f32)
    w_ref[...] = jnp.exp(jnp.where(jj <= ii, dl * lf, -dl * lb))
    idx = lax.broadcasted_iota(jnp.int32, (T, LANES), 0).astype(f32)
    ef_ref[...] = jnp.exp((idx + 1.0) * lf)
    eb_ref[...] = jnp.exp((T - idx) * lb)
    wf_ref[...] = jnp.exp((T - 1.0 - idx) * lf)
    wb_ref[...] = jnp.exp(idx * lb)
    srow = lax.broadcasted_iota(jnp.int32, (8, LANES), 0)
    dec_ref[...] = jnp.where(srow == 0, jnp.exp(T * lf), jnp.exp(T * lb))


def _ret_tables(logit):
    lg = jnp.broadcast_to(logit[:, :, None, None], (2, RH, 1, LANES))
    tab = pl.BlockSpec((None, T, LANES), lambda h: (h, 0, 0))
    return pl.pallas_call(
        _ret_tab_kernel,
        grid=(RH,),
        in_specs=[pl.BlockSpec((2, None, 1, LANES), lambda h: (0, h, 0, 0))],
        out_specs=[tab, tab, tab, tab, tab, pl.BlockSpec((None, 8, LANES), lambda h: (h, 0, 0))],
        out_shape=[jax.ShapeDtypeStruct((RH, T, T), f32)] + [jax.ShapeDtypeStruct((RH, T, LANES), f32)] * 4
        + [jax.ShapeDtypeStruct((RH, 8, LANES), f32)],
        compiler_params=_cp(("parallel",)),
        name="ret_tables",
    )(lg)


def _mod_kernel(a_ref, w_ref, b_ref, o_ref):
    a = _silu(a_ref[...])
    o_ref[...] = _dot(a, w_ref[...]) + b_ref[...]


def _modulation(a8, w, b):
    tn = 1024
    n = w.shape[1]
    return pl.pallas_call(
        _mod_kernel,
        grid=(n // tn,),
        in_specs=[pl.BlockSpec((8, D), lambda j: (0, 0)),
                  pl.BlockSpec((D, tn), lambda j: (0, j)),
                  pl.BlockSpec((1, tn), lambda j: (0, j))],
        out_specs=pl.BlockSpec((8, tn), lambda j: (0, j)),
        out_shape=jax.ShapeDtypeStruct((8, n), f32),
        compiler_params=_cp(("parallel",)),
        name="modulation",
    )(a8, w, b)


def _rms_mod(x, nw, sc, sh):
    y = x * lax.rsqrt(jnp.mean(x * x, axis=-1, keepdims=True) + EPS)
    return (y * nw) * (1.0 + sc) + sh


def _norm_kernel(x_ref, nw_ref, sc_ref, sh_ref, o_ref):
    o_ref[...] = _rms_mod(x_ref[...], nw_ref[...], sc_ref[...], sh_ref[...]).astype(o_ref.dtype)


def _norm_mod(x, nw, sc, sh):
    L = x.shape[0]
    tm = min(L, 256)
    vec = pl.BlockSpec((1, D), lambda i: (0, 0))
    return pl.pallas_call(
        _norm_kernel,
        grid=(L // tm,),
        in_specs=[pl.BlockSpec((tm, D), lambda i: (i, 0)), vec, vec, vec],
        out_specs=pl.BlockSpec((tm, D), lambda i: (i, 0)),
        out_shape=jax.ShapeDtypeStruct((L, D), bf16),
        compiler_params=_cp(("parallel",)),
        name="norm_mod",
    )(x, nw, sc, sh)


def _proj_kernel(u_ref, w_ref, *rest, bw, tn, rope, scale_from):
    acc = _dot(u_ref[...], w_ref[...])
    if rope:
        cos_ref, sin_ref, o_ref = rest
        scale = jnp.where(pl.program_id(1) >= scale_from, HD ** -0.5, 1.0).astype(f32)
        cos = cos_ref[...]
        sin = sin_ref[...]
        lane = lax.broadcasted_iota(jnp.int32, cos.shape, 1)
        first = (lane % (HD // 2)) < HD // 4
        for hh in range(tn // HD):
            t = acc[:, hh * HD:(hh + 1) * HD] * scale
            partner = jnp.where(first, pltpu.roll(t, HD - HD // 4, 1), pltpu.roll(t, HD // 4, 1))
            o_ref[hh] = (t * cos + partner * sin).astype(o_ref.dtype)
    else:
        (o_ref,) = rest
        if bw:
            for hh in range(tn // bw):
                o_ref[hh] = acc[:, hh * bw:(hh + 1) * bw].astype(o_ref.dtype)
        else:
            o_ref[...] = acc.astype(o_ref.dtype)


def _project(u, w, *, bw, tn, out_dtype=bf16, rope=None, scale_from=0, name):
    M = u.shape[0]
    N = w.shape[1]
    tm = min(M, 1024)
    in_specs = [pl.BlockSpec((tm, D), lambda i, j: (i, 0)), pl.BlockSpec((D, tn), lambda i, j: (0, j))]
    args = [u, w]
    if rope is not None:
        in_specs += [pl.BlockSpec((tm, LANES), lambda i, j: (i, 0))] * 2
        args += list(rope)
    if bw:
        out_spec = pl.BlockSpec((tn // bw, tm, bw), lambda i, j: (j, i, 0))
        out_shape = jax.ShapeDtypeStruct((N // bw, M, bw), out_dtype)
    else:
        out_spec = pl.BlockSpec((tm, tn), lambda i, j: (i, j))
        out_shape = jax.ShapeDtypeStruct((M, N), out_dtype)
    return pl.pallas_call(
        functools.partial(_proj_kernel, bw=bw, tn=tn, rope=rope is not None, scale_from=scale_from),
        grid=(M // tm, N // tn),
        in_specs=in_specs,
        out_specs=out_spec,
        out_shape=out_shape,
        compiler_params=_cp(("parallel", "parallel")),
        name=name,
    )(*args)


def _shift_matrix():
    rows = lax.broadcasted_iota(jnp.int32, (4 * T, T + 2 * HALO), 0)
    cols = lax.broadcasted_iota(jnp.int32, (4 * T, T + 2 * HALO), 1)
    blk = rows // T
    tap = jnp.where(blk < KC // 2, blk, blk + 1)
    return jnp.where(cols == (rows - blk * T) + HALO + tap - KC // 2, 1.0, 0.0).astype(bf16)


def _conv_silu(shift, main, prev, nxt, cw, cb, has_prev, has_next):
    zero = jnp.zeros_like(prev)
    ext = jnp.concatenate([jnp.where(has_prev, prev, zero), main, jnp.where(has_next, nxt, zero)], axis=0)
    sh = _dot(shift, ext)
    taps = [sh[0:T], sh[T:2 * T], main.astype(f32), sh[2 * T:3 * T], sh[3 * T:4 * T]]
    acc = cb
    for j in range(KC):
        acc = acc + taps[j] * cw[j:j + 1, :]
    return _silu(acc)


def _decay_prologue(dt_ref, bias_ref, arow_ref):
    dt = _softplus(dt_ref[...] + bias_ref[...])
    la = dt * arow_ref[...]
    acs = _cumsum_rows(la, _tri_incl())
    tot = acs[T - 1:T, :]
    return dt, la, acs, tot


def _transpose_blocks(x):
    n = x.shape[1] // LANES
    return jnp.concatenate([x[:, b * LANES:(b + 1) * LANES].T for b in range(n)], axis=0)


def _ssd_state_update(hs_ref, g, xc, bc16, w_rows, dec_rows):
    xT = _transpose_blocks(xc)
    lhs = jnp.concatenate(
        [(xT[r * P:(r + 1) * P, :] * w_rows[r:r + 1, :]).astype(bf16) for r in range(R)], axis=0)
    upd = _dot(lhs, bc16)
    old = hs_ref[g]
    hs_ref[g] = jnp.concatenate(
        [old[r * P:(r + 1) * P, :] * dec_rows[r:r + 1, :] + upd[r * P:(r + 1) * P, :] for r in range(R)], axis=0)


def _ret_state_update(hr_ref, h, v16, k16, wcol, dec_row):
    vw = (v16.astype(f32) * wcol).T.astype(bf16)
    hr_ref[h] = hr_ref[h] * dec_row + _dot(vw, k16)


def _state_kernel(*refs, nc, fwd, emit, write_conv):
    it = iter(refs)
    k_ref, v_ref = next(it), next(it)
    xs_ref, xs_p, xs_n = next(it), next(it), next(it)
    b_ref, b_p, b_n = next(it), next(it), next(it)
    if write_conv:
        c_ref, c_p, c_n = next(it), next(it), next(it)
    dt_ref = next(it)
    cwx, cbx, cwb, cbb = next(it), next(it), next(it), next(it)
    if write_conv:
        cwc, cbc = next(it), next(it)
    bias_ref, arow_ref = next(it), next(it)
    rw_ref, rdec_ref = next(it), next(it)
    h0r_ref, h0s_ref = next(it), next(it)
    if emit:
        gr_ref, gs_ref = next(it), next(it)
    if write_conv:
        xo_ref, bo_ref, co_ref = next(it), next(it), next(it)
    hfr_ref, hfs_ref = next(it), next(it)
    hr, hs, wT_s, decT_s = next(it), next(it), next(it), next(it)

    s = pl.program_id(0)
    c = s if fwd else nc - 1 - s
    has_prev = c > 0
    has_next = c < nc - 1

    @pl.when(s == 0)
    def _():
        hr[...] = h0r_ref[...]
        hs[...] = h0s_ref[...]

    if emit:
        gr_ref[0] = hr[...].astype(bf16)
        gs_ref[0] = hs[...].astype(bf16)

    dt, la, acs, tot = _decay_prologue(dt_ref, bias_ref, arow_ref)
    wexp = jnp.exp(tot - acs) if fwd else jnp.exp(acs - la)
    wT_s[...] = (wexp * dt).T
    decT_s[...] = jnp.broadcast_to(jnp.exp(tot), (T, 2 * SH)).T
    off = 0 if fwd else SH

    shift = _shift_matrix()

    def grp(g, carry):
        xc16 = _conv_silu(shift, xs_ref[g], xs_p[g], xs_n[g], cwx[g], cbx[g], has_prev, has_next).astype(bf16)
        if write_conv:
            def cat(a, b):
                return jnp.concatenate([a, b], axis=1)
            bcc = _conv_silu(shift, cat(b_ref[g], c_ref[g]), cat(b_p[g], c_p[g]), cat(b_n[g], c_n[g]),
                             cat(cwb[g], cwc[g]), cat(cbb[g], cbc[g]), has_prev, has_next).astype(bf16)
            bc16 = bcc[:, :NS]
            xo_ref[g] = xc16
            bo_ref[g] = bc16
            co_ref[g] = bcc[:, NS:]
        else:
            bc16 = _conv_silu(shift, b_ref[g], b_p[g], b_n[g], cwb[g], cbb[g], has_prev, has_next).astype(bf16)
        r0 = pl.multiple_of(off + g * R, R)
        _ssd_state_update(hs, g, xc16.astype(f32), bc16, wT_s[pl.ds(r0, R), :], decT_s[pl.ds(r0, R), :])
        return carry

    lax.fori_loop(0, NG, grp, 0, unroll=2)

    dsel = 0 if fwd else 1

    def head(h, carry):
        _ret_state_update(hr, h, v_ref[h], k_ref[h], rw_ref[h], rdec_ref[h][dsel:dsel + 1, :])
        return carry

    lax.fori_loop(0, RH, head, 0, unroll=True)

    @pl.when(s == nc - 1)
    def _():
        hfr_ref[...] = hr[...]
        hfs_ref[...] = hs[...]


def _state_sweep(L, *, fwd, emit, write_conv, k, k_blk, v, v_blk, xs, xs_blk, bsrc, b_blk, c_blk, dt,
                 convw, bias, arow, rw, rdec, h0r, h0s):
    nc = L // T
    tb = T // HALO
    nrb = L // HALO

    def cidx(s):
        return s if fwd else nc - 1 - s

    def main(lead, n, width):
        return pl.BlockSpec((n, T, width), lambda s: (lead, cidx(s), 0))

    def prev(lead, n, width):
        return pl.BlockSpec((n, HALO, width), lambda s: (lead, jnp.maximum(cidx(s) * tb - 1, 0), 0))

    def nxt(lead, n, width):
        return pl.BlockSpec((n, HALO, width), lambda s: (lead, jnp.minimum((cidx(s) + 1) * tb, nrb - 1), 0))

    def whole(a):
        nd = a.ndim
        return pl.BlockSpec(a.shape, lambda s: (0,) * nd)

    cwx, cbx, cwb, cbb, cwc, cbc = convw
    in_specs = [main(k_blk, RH, HD), main(v_blk, RH, HD),
                main(xs_blk, NG, GW), prev(xs_blk, NG, GW), nxt(xs_blk, NG, GW),
                main(b_blk, NG, NS), prev(b_blk, NG, NS), nxt(b_blk, NG, NS)]
    args = [k, v, xs, xs, xs, bsrc, bsrc, bsrc]
    if write_conv:
        in_specs += [main(c_blk, NG, NS), prev(c_blk, NG, NS), nxt(c_blk, NG, NS)]
        args += [bsrc, bsrc, bsrc]
    in_specs += [pl.BlockSpec((T, 2 * SH), lambda s: (cidx(s), 0))]
    args += [dt]
    small = [cwx, cbx, cwb, cbb] + ([cwc, cbc] if write_conv else []) + [bias, arow, rw, rdec, h0r, h0s]
    in_specs += [whole(a) for a in small]
    args += small

    out_specs, out_shape = [], []
    if emit:
        out_specs += [pl.BlockSpec((1, RH, HD, HD), lambda s: (cidx(s), 0, 0, 0)),
                      pl.BlockSpec((1, NG, GW, NS), lambda s: (cidx(s), 0, 0, 0))]
        out_shape += [jax.ShapeDtypeStruct((nc, RH, HD, HD), bf16), jax.ShapeDtypeStruct((nc, NG, GW, NS), bf16)]
    if write_conv:
        out_specs += [pl.BlockSpec((NG, T, GW), lambda s: (0, cidx(s), 0)),
                      pl.BlockSpec((NG, T, NS), lambda s: (0, cidx(s), 0)),
                      pl.BlockSpec((NG, T, NS), lambda s: (0, cidx(s), 0))]
        out_shape += [jax.ShapeDtypeStruct((NG, L, GW), bf16), jax.ShapeDtypeStruct((NG, L, NS), bf16),
                      jax.ShapeDtypeStruct((NG, L, NS), bf16)]
    out_specs += [pl.BlockSpec((RH, HD, HD), lambda s: (0, 0, 0)), pl.BlockSpec((NG, GW, NS), lambda s: (0, 0, 0))]
    out_shape += [jax.ShapeDtypeStruct((RH, HD, HD), f32), jax.ShapeDtypeStruct((NG, GW, NS), f32)]

    return pl.pallas_call(
        functools.partial(_state_kernel, nc=nc, fwd=fwd, emit=emit, write_conv=write_conv),
        grid=(nc,),
        in_specs=in_specs,
        out_specs=out_specs,
        out_shape=out_shape,
        scratch_shapes=[pltpu.VMEM((RH, HD, HD), f32), pltpu.VMEM((NG, GW, NS), f32),
                        pltpu.VMEM((2 * SH, T), f32), pltpu.VMEM((2 * SH, T), f32)],
        compiler_params=_cp(("arbitrary",)),
        name="state_sweep_" + ("f" if fwd else "b") + ("_emit" if emit else ""),
    )(*args)


def _fwd_kernel(q_ref, k_ref, v_ref, gr_in_ref, z_ref, xs_ref, b_ref, c_ref, dt_ref, gret_ref, gssd_ref,
                wret_ref, ef_ref, eb_ref, wf_ref, rdec_ref, bias_ref, arow_ref, dexp_ref, snw_ref, rnw_ref,
                h0r_ref, h0s_ref, dsel_ref, esel_ref, yr_ref, ys_ref,
                hr, hs, ypre, ap_s, ep_s, apT_s, wT_s, decT_s):
    s = pl.program_id(0)

    @pl.when(s == 0)
    def _():
        hr[...] = h0r_ref[...]
        hs[...] = h0s_ref[...]

    dt, la, acs, tot = _decay_prologue(dt_ref, bias_ref, arow_ref)
    lane = lax.broadcasted_iota(jnp.int32, (T, 2 * SH), 1)
    is_f = lane < SH
    a1 = jnp.where(is_f, acs, acs - la)
    e1 = jnp.exp(jnp.where(is_f, acs, tot - (acs - la)))
    a2 = a1 * LOG2E
    ldt = jnp.maximum(jnp.log2(dt), LOG2_FLOOR)
    for k, part in enumerate(_split3(a2)):
        ap_s[k] = part
    for k, part in enumerate(_split3(jnp.where(is_f, ldt - a2, ldt + a2))):
        apT_s[k] = part.T
    for k, part in enumerate(_split3(e1)):
        ep_s[k] = part
    wT_s[...] = (jnp.exp(tot - acs) * dt).T
    decT_s[...] = jnp.broadcast_to(jnp.exp(tot), (T, 2 * SH)).T

    ii = lax.broadcasted_iota(jnp.int32, (T, T), 0)
    jj = lax.broadcasted_iota(jnp.int32, (T, T), 1)
    lower = jj <= ii
    part_masks = [((lane >= k * R) & (lane < (k + 1) * R)) | ((lane >= SH + k * R) & (lane < SH + (k + 1) * R))
                  for k in range(3)]
    ones_lanes = (lane >= ONES_LANE) & (lane < ONES_LANE + 3)
    row16 = lax.broadcasted_iota(jnp.int32, (16, T), 0)
    lane_p = lax.broadcasted_iota(jnp.int32, (T, 2 * P), 1)

    def pack(parts_ref, g, with_ones):
        acc = jnp.where(ones_lanes, 1.0, 0.0) if with_ones else jnp.zeros((T, 2 * SH), f32)
        for k in range(3):
            shift = (2 * SH - g * R + k * R) % (2 * SH)
            acc = acc + jnp.where(part_masks[k], pltpu.roll(parts_ref[k], shift, 1), 0.0)
        return acc.astype(bf16)

    def grp(g, ssq):
        c16 = c_ref[g]
        b16 = b_ref[g]
        sc = _nt(c16, b16)
        r0 = pl.multiple_of(g * R, R)
        lhs_a = pack(ap_s, g, True)
        lhs_e = pack(ep_s, g, False)
        cf_rows = [apT_s[k, pl.ds(r0, R), :] for k in range(3)]
        cb_rows = [apT_s[k, pl.ds(SH + r0, R), :] for k in range(3)]
        xg = xs_ref[g]
        zg = z_ref[g].astype(f32)
        dg = dexp_ref[g]
        hcat = jnp.concatenate([hs[g].astype(bf16), gssd_ref[0, g]], axis=0)
        ycross = _nt(c16, hcat)
        efb = _dot(lhs_e, esel_ref[...])
        ms = []
        for r in range(R):
            lo = jnp.zeros((16, T), f32)
            up = jnp.zeros((16, T), f32)
            for k in range(3):
                lo = jnp.where(row16 == k, cf_rows[k][r:r + 1, :], lo)
                up = jnp.where(row16 == k, cb_rows[k][r:r + 1, :], up)
            dyn = jnp.concatenate([lo, up], axis=1).astype(bf16)
            rhs = jnp.concatenate([dsel_ref[r, 0:ONES_LANE, :], dyn, dsel_ref[r, ONES_LANE + 16:, :]], axis=0)
            dmat = _dot(lhs_a, rhs)
            wm = jnp.exp2(jnp.where(lower, dmat[:, :T], dmat[:, T:]))
            ms.append((sc * wm).astype(bf16))
        ys_parts = []
        for t in range(R // 2):
            xp = xg[:, t * 2 * P:(t + 1) * 2 * P]
            zero = jnp.zeros_like(xp)
            rhs = jnp.concatenate([jnp.where(lane_p < P, xp, zero), jnp.where(lane_p >= P, xp, zero)], axis=0)
            ys_parts.append(_dot(jnp.concatenate([ms[2 * t], ms[2 * t + 1]], axis=1), rhs))
        yg = jnp.concatenate(ys_parts, axis=1)
        yg = yg + ycross[:, :GW] * efb[:, :GW] + ycross[:, GW:] * efb[:, GW:]
        yg = (yg + xg.astype(f32) * dg) * _silu(zg)
        ypre[g] = yg
        ssq = ssq + jnp.sum(yg * yg, axis=1, keepdims=True)
        _ssd_state_update(hs, g, xg.astype(f32), b16, wT_s[pl.ds(r0, R), :], decT_s[pl.ds(r0, R), :])
        return ssq

    ssq = lax.fori_loop(0, NG, grp, jnp.zeros((T, 1), f32), unroll=4)
    rs = lax.rsqrt(ssq * (1.0 / SI) + EPS)

    def fin(g, carry):
        ys_ref[g] = ((ypre[g] * rs) * snw_ref[g]).astype(ys_ref.dtype)
        return carry

    lax.fori_loop(0, NG, fin, 0)

    def head(h, carry):
        q16 = q_ref[h]
        k16 = k_ref[h]
        v16 = v_ref[h]
        m16 = (_nt(q16, k16) * wret_ref[h]).astype(bf16)
        y = _dot(m16, v16)
        qf = q16.astype(f32)
        lc = jnp.concatenate([(qf * ef_ref[h]).astype(bf16), (qf * eb_ref[h]).astype(bf16)], axis=1)
        hcat = jnp.concatenate([hr[h].astype(bf16), gret_ref[0, h]], axis=1)
        y = y + _nt(lc, hcat)
        mu = jnp.mean(y, axis=-1, keepdims=True)
        d = y - mu
        yn = d * lax.rsqrt(jnp.mean(d * d, axis=-1, keepdims=True) + EPS)
        gg = gr_in_ref[h].astype(f32)
        yr_ref[h] = ((yn * rnw_ref[h]) * _silu(gg)).astype(yr_ref.dtype)
        _ret_state_update(hr, h, v16, k16, wf_ref[h], rdec_ref[h][0:1, :])
        return carry

    lax.fori_loop(0, RH, head, 0, unroll=True)


def _fwd_sweep(L, qk, vg, zx, xc, bc, cc, dt, gret, gssd, tabs, bias, arow, dexp, snw, rnw, h0r, h0s):
    nc = L // T
    wret, ef, eb, wf, _, rdec = tabs

    def blk(lead, n, width):
        return pl.BlockSpec((n, T, width), lambda s: (lead, s, 0))

    def whole(a):
        nd = a.ndim
        return pl.BlockSpec(a.shape, lambda s: (0,) * nd)

    dsel, esel = _selector_constants()
    small = [wret, ef, eb, wf, rdec, bias, arow, dexp, snw, rnw, h0r, h0s, dsel, esel]
    in_specs = [blk(0, RH, HD), blk(1, RH, HD), blk(0, RH, HD), blk(1, RH, HD),
                blk(0, NG, GW), blk(0, NG, GW), blk(0, NG, NS), blk(0, NG, NS),
                pl.BlockSpec((T, 2 * SH), lambda s: (s, 0)),
                pl.BlockSpec((1, RH, HD, HD), lambda s: (s, 0, 0, 0)),
                pl.BlockSpec((1, NG, GW, NS), lambda s: (s, 0, 0, 0))] + [whole(a) for a in small]
    return pl.pallas_call(
        _fwd_kernel,
        grid=(nc,),
        in_specs=in_specs,
        out_specs=[blk(0, RH, HD), blk(0, NG, GW)],
        out_shape=[jax.ShapeDtypeStruct((RH, L, HD), bf16), jax.ShapeDtypeStruct((NG, L, GW), bf16)],
        scratch_shapes=[pltpu.VMEM((RH, HD, HD), f32), pltpu.VMEM((NG, GW, NS), f32), pltpu.VMEM((NG, T, GW), f32),
                        pltpu.VMEM((3, T, 2 * SH), f32), pltpu.VMEM((3, T, 2 * SH), f32),
                        pltpu.VMEM((3, 2 * SH, T), f32)]
        + [pltpu.VMEM((2 * SH, T), f32)] * 2,
        compiler_params=_cp(("arbitrary",)),
        name="fwd_sweep",
    )(qk, qk, vg, vg, zx, xc, bc, cc, dt, gret, gssd, *small)


def _branch_out_kernel(yr_ref, ys_ref, wr_ref, ws_ref, gr_ref, gs_ref, o_ref):
    acc_r = _dot(jnp.concatenate([yr_ref[h] for h in range(RH)], axis=1), wr_ref[...])
    acc_s = _dot(jnp.concatenate([ys_ref[g] for g in range(NG)], axis=1), ws_ref[...])
    m = _sigmoid(gr_ref[...].astype(f32)) * acc_r + _sigmoid(gs_ref[...].astype(f32)) * acc_s
    o_ref[...] = m.astype(o_ref.dtype)


def _branch_out(yr, ys, wr, ws, gates):
    L = yr.shape[1]
    tm, tn = min(L, 512), 512
    nj = D // tn
    return pl.pallas_call(
        _branch_out_kernel,
        grid=(L // tm, nj),
        in_specs=[pl.BlockSpec((RH, tm, HD), lambda i, j: (0, i, 0)),
                  pl.BlockSpec((NG, tm, GW), lambda i, j: (0, i, 0)),
                  pl.BlockSpec((D, tn), lambda i, j: (0, j)),
                  pl.BlockSpec((SI, tn), lambda i, j: (0, j)),
                  pl.BlockSpec((tm, tn), lambda i, j: (i, j)),
                  pl.BlockSpec((tm, tn), lambda i, j: (i, nj + j))],
        out_specs=pl.BlockSpec((tm, tn), lambda i, j: (i, j)),
        out_shape=jax.ShapeDtypeStruct((L, D), bf16),
        compiler_params=_cp(("parallel", "parallel")),
        name="branch_out",
    )(yr, ys, wr, ws, gates, gates)


def _resid_kernel(m_ref, w_ref, x_ref, g_ref, o_ref):
    o_ref[...] = x_ref[...] + g_ref[...] * _dot(m_ref[...], w_ref[...])


def _mix_residual(m, w, x, gate):
    L = m.shape[0]
    tm, tn = min(L, 1024), 1024
    return pl.pallas_call(
        _resid_kernel,
        grid=(L // tm, D // tn),
        in_specs=[pl.BlockSpec((tm, D), lambda i, j: (i, 0)),
                  pl.BlockSpec((D, tn), lambda i, j: (0, j)),
                  pl.BlockSpec((tm, tn), lambda i, j: (i, j)),
                  pl.BlockSpec((1, tn), lambda i, j: (0, j))],
        out_specs=pl.BlockSpec((tm, tn), lambda i, j: (i, j)),
        out_shape=jax.ShapeDtypeStruct((L, D), f32),
        compiler_params=_cp(("parallel", "parallel")),
        name="mix_residual",
    )(m, w, x, gate)


def _mlp1_kernel(h_ref, nw_ref, sc_ref, sh_ref, w_ref, o_ref, f_s):
    @pl.when(pl.program_id(1) == 0)
    def _():
        f_s[...] = _rms_mod(h_ref[...], nw_ref[...], sc_ref[...], sh_ref[...]).astype(bf16)

    a = jnp.maximum(_dot(f_s[...], w_ref[...]), 0.0)
    o_ref[...] = (a * a).astype(o_ref.dtype)


def _mlp1(h, nw, sc, sh, w):
    L = h.shape[0]
    tm, tn = min(L, 1024), 1024
    vec = pl.BlockSpec((1, D), lambda i, j: (0, 0))
    return pl.pallas_call(
        _mlp1_kernel,
        grid=(L // tm, DFF // tn),
        in_specs=[pl.BlockSpec((tm, D), lambda i, j: (i, 0)), vec, vec, vec,
                  pl.BlockSpec((D, tn), lambda i, j: (0, j))],
        out_specs=pl.BlockSpec((tm, tn), lambda i, j: (i, j)),
        out_shape=jax.ShapeDtypeStruct((L, DFF), bf16),
        scratch_shapes=[pltpu.VMEM((tm, D), bf16)],
        compiler_params=_cp(("parallel", "arbitrary")),
        name="mlp1",
    )(h, nw, sc, sh, w)


def _mlp2_kernel(a_ref, w_ref, h_ref, g_ref, fw_ref, o_ref, acc, *, nk):
    kk = pl.program_id(1)

    @pl.when(kk == 0)
    def _():
        acc[...] = jnp.zeros_like(acc)

    acc[...] += _dot(a_ref[...], w_ref[...])

    @pl.when(kk == nk - 1)
    def _():
        h2 = h_ref[...] + g_ref[...] * acc[...]
        y = h2 * lax.rsqrt(jnp.mean(h2 * h2, axis=-1, keepdims=True) + EPS)
        o_ref[...] = y * fw_ref[...]


def _mlp2_final(a, w, h, gate, fw):
    L = a.shape[0]
    tm, tk = min(L, 512), 1024
    nk = DFF // tk
    vec = pl.BlockSpec((1, D), lambda i, k: (0, 0))
    return pl.pallas_call(
        functools.partial(_mlp2_kernel, nk=nk),
        grid=(L // tm, nk),
        in_specs=[pl.BlockSpec((tm, tk), lambda i, k: (i, k)),
                  pl.BlockSpec((tk, D), lambda i, k: (k, 0)),
                  pl.BlockSpec((tm, D), lambda i, k: (i, 0)), vec, vec],
        out_specs=pl.BlockSpec((tm, D), lambda i, k: (i, 0)),
        out_shape=jax.ShapeDtypeStruct((L, D), f32),
        scratch_shapes=[pltpu.VMEM((tm, D), f32)],
        compiler_params=_cp(("parallel", "arbitrary")),
        name="mlp2_final",
    )(a, w, h, gate, fw)


def _split_conv(conv_w, conv_b):
    def grp(w, b, width):
        return (jnp.transpose(w.reshape(KC, NG, width), (1, 0, 2)), b.reshape(NG, 1, width))
    cwx, cbx = grp(conv_w[:, :SI], conv_b[:SI], GW)
    cwb, cbb = grp(conv_w[:, SI:SI + NG * NS], conv_b[SI:SI + NG * NS], NS)
    cwc, cbc = grp(conv_w[:, SI + NG * NS:], conv_b[SI + NG * NS:], NS)
    return cwx, cbx, cwb, cbb, cwc, cbc


def kernel(x, c, ctx, c_ctx, w_mod, b_mod, norm1_w, w_in, conv_w, conv_b, ret_decay_logit, ret_norm_w,
           ssd_a_log, ssd_dt_bias, ssd_d, ssd_norm_w, w_ret_out, w_ssd_out, w_o, norm2_w, w_mlp1, w_mlp2,
           final_norm_w):
    depth = w_mod.shape[0]
    assert depth == 1 and x.shape[0] == 1 and x.shape[2] == D
    L = x.shape[1]
    Lc = ctx.shape[1]
    assert L % T == 0 and Lc % T == 0 and L % GRID_W == 0
    xl = x[0]
    xcx = ctx[0]
    ly = 0

    a8 = jnp.zeros((8, D), f32).at[0].set(c[0]).at[1].set(c_ctx)
    mod = _modulation(a8, w_mod[ly], b_mod[ly][None, :])
    sh_a, sc_a, g_a, sh_f, sc_f, g_f = [mod[0:1, i * D:(i + 1) * D] for i in range(6)]
    csh_a, csc_a = mod[1:2, 0:D], mod[1:2, D:2 * D]

    wi = w_in[ly]
    o_q, o_k, o_v, o_g, o_z, o_x = 0, D, 2 * D, 3 * D, 4 * D, 4 * D + SI
    o_b = o_x + SI
    o_c = o_b + NG * NS
    o_dt = o_c + NG * NS
    o_gate = o_dt + 2 * SH
    w_qk = wi[:, o_q:o_v].astype(bf16)
    w_k = wi[:, o_k:o_v].astype(bf16)
    w_vgbc = jnp.concatenate([wi[:, o_v:o_z], wi[:, o_b:o_dt]], axis=1).astype(bf16)
    w_vb = jnp.concatenate([wi[:, o_v:o_g], wi[:, o_b:o_c]], axis=1).astype(bf16)
    w_zx = wi[:, o_z:o_b].astype(bf16)
    w_x = wi[:, o_x:o_b].astype(bf16)
    w_dt = wi[:, o_dt:o_gate].astype(bf16)
    w_gate = wi[:, o_gate:].astype(bf16)

    convw = _split_conv(conv_w[ly], conv_b[ly])
    bias = ssd_dt_bias[ly].reshape(1, 2 * SH)
    arow = (-jnp.exp(ssd_a_log[ly])).reshape(1, 2 * SH)
    dexp = jnp.repeat(ssd_d[ly], P).reshape(NG, 1, GW)
    snw = ssd_norm_w[ly].reshape(NG, 1, GW)
    rnw = ret_norm_w[ly].reshape(RH, 1, HD)
    nw1 = norm1_w[ly][None, :]

    tabs = _ret_tables(ret_decay_logit[ly])
    _, _, _, wf_t, wb_t, rdec = tabs
    zr = jnp.zeros((RH, HD, HD), f32)
    zs = jnp.zeros((NG, GW, NS), f32)

    uc = _norm_mod(xcx, nw1, csc_a, csh_a)
    ident = (jnp.ones((Lc, LANES), f32), jnp.zeros((Lc, LANES), f32))
    kc = _project(uc, w_k, bw=HD, tn=512, rope=ident, scale_from=0, name="ctx_k")
    vbc = _project(uc, w_vb, bw=HD, tn=512, name="ctx_vb")
    xsc = _project(uc, w_x, bw=GW, tn=GW, name="ctx_x")
    dtc = _project(uc, w_dt, bw=0, tn=2 * SH, out_dtype=f32, name="ctx_dt")
    common = dict(k=kc, k_blk=0, v=vbc, v_blk=0, xs=xsc, xs_blk=0, bsrc=vbc, b_blk=RH // NG, c_blk=0, dt=dtc,
                  convw=convw, bias=bias, arow=arow, rdec=rdec, h0r=zr, h0s=zs)
    cfr, cfs = _state_sweep(Lc, fwd=True, emit=False, write_conv=False, rw=wf_t, **common)
    cbr, cbs = _state_sweep(Lc, fwd=False, emit=False, write_conv=False, rw=wb_t, **common)

    u = _norm_mod(xl, nw1, sc_a, sh_a)
    rope = _rope_tables(L)
    qk = _project(u, w_qk, bw=HD, tn=1024, rope=rope, scale_from=D // 1024, name="lat_qk")
    vg = _project(u, w_vgbc, bw=HD, tn=1024, name="lat_vgbc")
    zx = _project(u, w_zx, bw=GW, tn=1024, name="lat_zx")
    gates = _project(u, w_gate, bw=0, tn=1024, name="lat_gates")
    dtl = _project(u, w_dt, bw=0, tn=2 * SH, out_dtype=f32, name="lat_dt")

    gret, gssd, xcv, bcv, ccv, _, _ = _state_sweep(
        L, fwd=False, emit=True, write_conv=True, k=qk, k_blk=1, v=vg, v_blk=0, xs=zx, xs_blk=1, bsrc=vg,
        b_blk=2 * RH // NG, c_blk=2 * RH // NG + 1, dt=dtl, convw=convw, bias=bias, arow=arow, rw=wb_t, rdec=rdec,
        h0r=cbr, h0s=cbs)
    yr, ys = _fwd_sweep(L, qk, vg, zx, xcv, bcv, ccv, dtl, gret, gssd, tabs, bias, arow, dexp, snw, rnw, cfr, cfs)

    m = _branch_out(yr, ys, w_ret_out[ly].astype(bf16), w_ssd_out[ly].astype(bf16), gates)
    h1 = _mix_residual(m, w_o[ly].astype(bf16), xl, g_a)
    a = _mlp1(h1, norm2_w[ly][None, :], sc_f, sh_f, w_mlp1[ly].astype(bf16))
    out = _mlp2_final(a, w_mlp2[ly].astype(bf16), h1, g_f, final_norm_w[None, :])
    return out[None]
```

```python
import functools
import math

import numpy as np
import jax
import jax.numpy as jnp
from jax import lax
from jax.experimental import pallas as pl
from jax.experimental.pallas import tpu as pltpu

f32 = jnp.float32
bf16 = jnp.bfloat16

D = 2048
T = 128
GRID_W = 64
HD = 128
RH = D // HD
SI = 2 * D
P = 64
SH = SI // P
NS = 128
NG = 8
R = SH // NG
GW = R * P
KC = 5
DFF = 4 * D
ROPE_BASE = 10000.0
EPS = 1e-6
LANES = 128
MXU_N = 256
HALO = 16
LOG2E = 1.0 / math.log(2.0)
LOG2_FLOOR = -300.0
ONES_LANE = 96
VMEM_LIMIT = 48 * 1024 * 1024


def _cp(sem):
    return pltpu.CompilerParams(dimension_semantics=sem, vmem_limit_bytes=VMEM_LIMIT)


def _nt(a, b):
    return lax.dot_general(a, b, (((1,), (1,)), ((), ())), preferred_element_type=f32)


def _dot(a, b):
    return jnp.dot(a, b, preferred_element_type=f32)


def _sigmoid(x):
    return 0.5 + 0.5 * jnp.tanh(0.5 * x)


def _silu(x):
    h = 0.5 * x
    return h + h * jnp.tanh(h)


def _softplus(x):
    return jnp.maximum(x, 0.0) + jnp.log1p(jnp.exp(-jnp.abs(x)))


def _log_sigmoid(x):
    return jnp.minimum(x, 0.0) - jnp.log1p(jnp.exp(-jnp.abs(x)))


def _split3(x):
    p1 = x.astype(bf16).astype(f32)
    r1 = x - p1
    p2 = r1.astype(bf16).astype(f32)
    p3 = (r1 - p2).astype(bf16).astype(f32)
    return p1, p2, p3


def _cumsum_rows(la, tri):
    a1, a2, a3 = _split3(la)
    return _dot(tri, a1.astype(bf16)) + _dot(tri, a2.astype(bf16)) + _dot(tri, a3.astype(bf16))


def _selector_constants():
    dsel = np.zeros((R, 2 * SH, 2 * T), np.float32)
    esel = np.zeros((2 * SH, 2 * GW), np.float32)
    for r in range(R):
        for k in range(3):
            dsel[r, k * R + r, :T] = 1.0
            dsel[r, SH + k * R + r, T:] = -1.0
            esel[k * R + r, r * P:(r + 1) * P] = 1.0
            esel[SH + k * R + r, GW + r * P:GW + (r + 1) * P] = 1.0
    return jnp.asarray(dsel, bf16), jnp.asarray(esel, bf16)


def _tri_incl():
    ii = lax.broadcasted_iota(jnp.int32, (T, T), 0)
    jj = lax.broadcasted_iota(jnp.int32, (T, T), 1)
    return jnp.where(jj <= ii, 1.0, 0.0).astype(bf16)


def _rope_tab_kernel(cos_ref, sin_ref, *, tm):
    i = pl.program_id(0)
    pos = i * tm + lax.broadcasted_iota(jnp.int32, (tm, LANES), 0)
    lane = lax.broadcasted_iota(jnp.int32, (tm, LANES), 1)
    row = (pos // GRID_W).astype(f32)
    col = (pos % GRID_W).astype(f32)
    fidx = (lane % (HD // 4)).astype(f32)
    freqs = jnp.exp(fidx * (-math.log(ROPE_BASE) / (HD // 4)))
    ang = jnp.where((lane % (HD // 2)) < HD // 4, row, col) * freqs
    s = jnp.sin(ang)
    cos_ref[...] = jnp.cos(ang)
    sin_ref[...] = jnp.where(lane < HD // 2, -s, s)


def _rope_perm():
    qd = HD // 4
    return np.concatenate([np.arange(0, qd), np.arange(2 * qd, 3 * qd), np.arange(qd, 2 * qd), np.arange(3 * qd, HD)])


def _rope_tables(L):
    tm = min(L, 512)
    return pl.pallas_call(
        functools.partial(_rope_tab_kernel, tm=tm),
        grid=(L // tm,),
        out_specs=[pl.BlockSpec((tm, LANES), lambda i: (i, 0))] * 2,
        out_shape=[jax.ShapeDtypeStruct((L, LANES), f32)] * 2,
        compiler_params=_cp(("parallel",)),
        name="rope_tables",
    )()


def _ret_tab_kernel(lg_ref, w_ref, ef_ref, eb_ref, wf_ref, wb_ref, dec_ref):
    lf = _log_sigmoid(lg_ref[0])
    lb = _log_sigmoid(lg_ref[1])
    ii = lax.broadcasted_iota(jnp.int32, (T, T), 0)
    jj = lax.broadcasted_iota(jnp.int32, (T, T), 1)
    dl = (ii - jj).astype(f32)
    w_ref[...] = jnp.exp(jnp.where(jj <= ii, dl * lf, -dl * lb))
    idx = lax.broadcasted_iota(jnp.int32, (T, LANES), 0).astype(f32)
    ef_ref[...] = jnp.exp((idx + 1.0) * lf)
    eb_ref[...] = jnp.exp((T - idx) * lb)
    wf_ref[...] = jnp.exp((T - 1.0 - idx) * lf)
    wb_ref[...] = jnp.exp(idx * lb)
    srow = lax.broadcasted_iota(jnp.int32, (8, LANES), 0)
    dec_ref[...] = jnp.where(srow == 0, jnp.exp(T * lf), jnp.exp(T * lb))


def _ret_tables(logit):
    lg = jnp.broadcast_to(logit[:, :, None, None], (2, RH, 1, LANES))
    tab = pl.BlockSpec((None, T, LANES), lambda h: (h, 0, 0))
    return pl.pallas_call(
        _ret_tab_kernel,
        grid=(RH,),
        in_specs=[pl.BlockSpec((2, None, 1, LANES), lambda h: (0, h, 0, 0))],
        out_specs=[tab, tab, tab, tab, tab, pl.BlockSpec((None, 8, LANES), lambda h: (h, 0, 0))],
        out_shape=[jax.ShapeDtypeStruct((RH, T, T), f32)] + [jax.ShapeDtypeStruct((RH, T, LANES), f32)] * 4
        + [jax.ShapeDtypeStruct((RH, 8, LANES), f32)],
        compiler_params=_cp(("parallel",)),
        name="ret_tables",
    )(lg)


def _mod_kernel(a_ref, w_ref, b_ref, o_ref):
    a = _silu(a_ref[...])
    o_ref[...] = _dot(a, w_ref[...]) + b_ref[...]


def _modulation(a8, w, b):
    tn = 1024
    n = w.shape[1]
    return pl.pallas_call(
        _mod_kernel,
        grid=(n // tn,),
        in_specs=[pl.BlockSpec((8, D), lambda j: (0, 0)),
                  pl.BlockSpec((D, tn), lambda j: (0, j)),
                  pl.BlockSpec((1, tn), lambda j: (0, j))],
        out_specs=pl.BlockSpec((8, tn), lambda j: (0, j)),
        out_shape=jax.ShapeDtypeStruct((8, n), f32),
        compiler_params=_cp(("parallel",)),
        name="modulation",
    )(a8, w, b)


def _rms_mod(x, nw, sc, sh):
    y = x * lax.rsqrt(jnp.mean(x * x, axis=-1, keepdims=True) + EPS)
    return (y * nw) * (1.0 + sc) + sh


def _norm_kernel(x_ref, nw_ref, sc_ref, sh_ref, o_ref):
    o_ref[...] = _rms_mod(x_ref[...], nw_ref[...], sc_ref[...], sh_ref[...]).astype(o_ref.dtype)


def _norm_mod(x, nw, sc, sh):
    L = x.shape[0]
    tm = min(L, 256)
    vec = pl.BlockSpec((1, D), lambda i: (0, 0))
    return pl.pallas_call(
        _norm_kernel,
        grid=(L // tm,),
        in_specs=[pl.BlockSpec((tm, D), lambda i: (i, 0)), vec, vec, vec],
        out_specs=pl.BlockSpec((tm, D), lambda i: (i, 0)),
        out_shape=jax.ShapeDtypeStruct((L, D), bf16),
        compiler_params=_cp(("parallel",)),
        name="norm_mod",
    )(x, nw, sc, sh)


def _proj_kernel(u_ref, w_ref, *rest, bw, tn, rope, scale_from):
    if rope:
        cos_ref, sin_ref, o_ref = rest
        scale = jnp.where(pl.program_id(1) >= scale_from, HD ** -0.5, 1.0).astype(f32)
        cos = cos_ref[...] * scale
        sin = sin_ref[...] * scale
    else:
        (o_ref,) = rest
    sw = min(tn, MXU_N)
    for nb in range(tn // sw):
        acc = _dot(u_ref[...], w_ref[:, nb * sw:(nb + 1) * sw])
        if rope:
            for hh in range(sw // HD):
                t = acc[:, hh * HD:(hh + 1) * HD]
                o_ref[nb * (sw // HD) + hh] = (t * cos + pltpu.roll(t, HD // 2, 1) * sin).astype(o_ref.dtype)
        elif bw and bw <= sw:
            for hh in range(sw // bw):
                o_ref[nb * (sw // bw) + hh] = acc[:, hh * bw:(hh + 1) * bw].astype(o_ref.dtype)
        elif bw:
            per = bw // sw
            o_ref[nb // per, :, (nb % per) * sw:(nb % per + 1) * sw] = acc.astype(o_ref.dtype)
        else:
            o_ref[:, nb * sw:(nb + 1) * sw] = acc.astype(o_ref.dtype)


def _project(u, w, *, bw, tn, out_dtype=bf16, rope=None, scale_from=0, name):
    M = u.shape[0]
    N = w.shape[1]
    tm = min(M, 1024)
    in_specs = [pl.BlockSpec((tm, D), lambda i, j: (i, 0)), pl.BlockSpec((D, tn), lambda i, j: (0, j))]
    args = [u, w]
    if rope is not None:
        in_specs += [pl.BlockSpec((tm, LANES), lambda i, j: (i, 0))] * 2
        args += list(rope)
    if bw:
        out_spec = pl.BlockSpec((tn // bw, tm, bw), lambda i, j: (j, i, 0))
        out_shape = jax.ShapeDtypeStruct((N // bw, M, bw), out_dtype)
    else:
        out_spec = pl.BlockSpec((tm, tn), lambda i, j: (i, j))
        out_shape = jax.ShapeDtypeStruct((M, N), out_dtype)
    return pl.pallas_call(
        functools.partial(_proj_kernel, bw=bw, tn=tn, rope=rope is not None, scale_from=scale_from),
        grid=(M // tm, N // tn),
        in_specs=in_specs,
        out_specs=out_spec,
        out_shape=out_shape,
        compiler_params=_cp(("parallel", "parallel")),
        name=name,
    )(*args)


def _shift_matrix():
    rows = lax.broadcasted_iota(jnp.int32, (4 * T, T + 2 * HALO), 0)
    cols = lax.broadcasted_iota(jnp.int32, (4 * T, T + 2 * HALO), 1)
    blk = rows // T
    tap = jnp.where(blk < KC // 2, blk, blk + 1)
    return jnp.where(cols == (rows - blk * T) + HALO + tap - KC // 2, 1.0, 0.0).astype(bf16)


def _conv_silu(shift, main, prev, nxt, cw, cb, has_prev, has_next):
    zero = jnp.zeros_like(prev)
    ext = jnp.concatenate([jnp.where(has_prev, prev, zero), main, jnp.where(has_next, nxt, zero)], axis=0)
    sh = _dot(shift, ext)
    taps = [sh[0:T], sh[T:2 * T], main.astype(f32), sh[2 * T:3 * T], sh[3 * T:4 * T]]
    acc = cb
    for j in range(KC):
        acc = acc + taps[j] * cw[j:j + 1, :]
    return _silu(acc)


def _decay_prologue(dt_ref, bias_ref, arow_ref):
    dt = _softplus(dt_ref[...] + bias_ref[...])
    la = dt * arow_ref[...]
    acs = _cumsum_rows(la, _tri_incl())
    tot = acs[T - 1:T, :]
    return dt, la, acs, tot


def _transpose_blocks(x):
    n = x.shape[1] // LANES
    return jnp.concatenate([x[:, b * LANES:(b + 1) * LANES].T for b in range(n)], axis=0)


def _ssd_state_update(hs_ref, g, xc, bc16, w_rows, dec_rows):
    xT = _transpose_blocks(xc)
    lhs = jnp.concatenate(
        [(xT[r * P:(r + 1) * P, :] * w_rows[r:r + 1, :]).astype(bf16) for r in range(R)], axis=0)
    upd = _dot(lhs, bc16)
    old = hs_ref[g]
    hs_ref[g] = jnp.concatenate(
        [old[r * P:(r + 1) * P, :] * dec_rows[r:r + 1, :] + upd[r * P:(r + 1) * P, :] for r in range(R)], axis=0)


def _ret_state_update(hr_ref, h, v16, k16, wcol, dec_row):
    vw = (v16.astype(f32) * wcol).T.astype(bf16)
    hr_ref[h] = hr_ref[h] * dec_row + _dot(vw, k16)


def _state_kernel(*refs, nc, fwd, emit, write_conv):
    it = iter(refs)
    k_ref, v_ref = next(it), next(it)
    xs_ref, xs_p, xs_n = next(it), next(it), next(it)
    b_ref, b_p, b_n = next(it), next(it), next(it)
    if write_conv:
        c_ref, c_p, c_n = next(it), next(it), next(it)
    dt_ref = next(it)
    cwx, cbx, cwb, cbb = next(it), next(it), next(it), next(it)
    if write_conv:
        cwc, cbc = next(it), next(it)
    bias_ref, arow_ref = next(it), next(it)
    rw_ref, rdec_ref = next(it), next(it)
    h0r_ref, h0s_ref = next(it), next(it)
    if emit:
        gr_ref, gs_ref = next(it), next(it)
    if write_conv:
        xo_ref, bo_ref, co_ref = next(it), next(it), next(it)
    hfr_ref, hfs_ref = next(it), next(it)
    hr, hs, wT_s, decT_s = next(it), next(it), next(it), next(it)

    s = pl.program_id(0)
    c = s if fwd else nc - 1 - s
    has_prev = c > 0
    has_next = c < nc - 1

    @pl.when(s == 0)
    def _():
        hr[...] = h0r_ref[...]
        hs[...] = h0s_ref[...]

    if emit:
        gr_ref[0] = hr[...].astype(bf16)
        gs_ref[0] = hs[...].astype(bf16)

    dt, la, acs, tot = _decay_prologue(dt_ref, bias_ref, arow_ref)
    wexp = jnp.exp(tot - acs) if fwd else jnp.exp(acs - la)
    wT_s[...] = (wexp * dt).T
    decT_s[...] = jnp.broadcast_to(jnp.exp(tot), (T, 2 * SH)).T
    off = 0 if fwd else SH

    shift = _shift_matrix()

    def grp(g, carry):
        xc16 = _conv_silu(shift, xs_ref[g], xs_p[g], xs_n[g], cwx[g], cbx[g], has_prev, has_next).astype(bf16)
        if write_conv:
            def cat(a, b):
                return jnp.concatenate([a, b], axis=1)
            bcc = _conv_silu(shift, cat(b_ref[g], c_ref[g]), cat(b_p[g], c_p[g]), cat(b_n[g], c_n[g]),
                             cat(cwb[g], cwc[g]), cat(cbb[g], cbc[g]), has_prev, has_next).astype(bf16)
            bc16 = bcc[:, :NS]
            xo_ref[g] = xc16
            bo_ref[g] = bc16
            co_ref[g] = bcc[:, NS:]
        else:
            bc16 = _conv_silu(shift, b_ref[g], b_p[g], b_n[g], cwb[g], cbb[g], has_prev, has_next).astype(bf16)
        r0 = pl.multiple_of(off + g * R, R)
        _ssd_state_update(hs, g, xc16.astype(f32), bc16, wT_s[pl.ds(r0, R), :], decT_s[pl.ds(r0, R), :])
        return carry

    lax.fori_loop(0, NG, grp, 0, unroll=True)

    dsel = 0 if fwd else 1

    def head(h, carry):
        _ret_state_update(hr, h, v_ref[h], k_ref[h], rw_ref[h], rdec_ref[h][dsel:dsel + 1, :])
        return carry

    lax.fori_loop(0, RH, head, 0, unroll=True)

    @pl.when(s == nc - 1)
    def _():
        hfr_ref[...] = hr[...]
        hfs_ref[...] = hs[...]


def _state_sweep(L, *, fwd, emit, write_conv, k, k_blk, v, v_blk, xs, xs_blk, bsrc, b_blk, c_blk, dt,
                 convw, bias, arow, rw, rdec, h0r, h0s):
    nc = L // T
    tb = T // HALO
    nrb = L // HALO

    def cidx(s):
        return s if fwd else nc - 1 - s

    def main(lead, n, width):
        return pl.BlockSpec((n, T, width), lambda s: (lead, cidx(s), 0))

    def prev(lead, n, width):
        return pl.BlockSpec((n, HALO, width), lambda s: (lead, jnp.maximum(cidx(s) * tb - 1, 0), 0))

    def nxt(lead, n, width):
        return pl.BlockSpec((n, HALO, width), lambda s: (lead, jnp.minimum((cidx(s) + 1) * tb, nrb - 1), 0))

    def whole(a):
        nd = a.ndim
        return pl.BlockSpec(a.shape, lambda s: (0,) * nd)

    cwx, cbx, cwb, cbb, cwc, cbc = convw
    in_specs = [main(k_blk, RH, HD), main(v_blk, RH, HD),
                main(xs_blk, NG, GW), prev(xs_blk, NG, GW), nxt(xs_blk, NG, GW),
                main(b_blk, NG, NS), prev(b_blk, NG, NS), nxt(b_blk, NG, NS)]
    args = [k, v, xs, xs, xs, bsrc, bsrc, bsrc]
    if write_conv:
        in_specs += [main(c_blk, NG, NS), prev(c_blk, NG, NS), nxt(c_blk, NG, NS)]
        args += [bsrc, bsrc, bsrc]
    in_specs += [pl.BlockSpec((T, 2 * SH), lambda s: (cidx(s), 0))]
    args += [dt]
    small = [cwx, cbx, cwb, cbb] + ([cwc, cbc] if write_conv else []) + [bias, arow, rw, rdec, h0r, h0s]
    in_specs += [whole(a) for a in small]
    args += small

    out_specs, out_shape = [], []
    if emit:
        out_specs += [pl.BlockSpec((1, RH, HD, HD), lambda s: (cidx(s), 0, 0, 0)),
                      pl.BlockSpec((1, NG, GW, NS), lambda s: (cidx(s), 0, 0, 0))]
        out_shape += [jax.ShapeDtypeStruct((nc, RH, HD, HD), bf16), jax.ShapeDtypeStruct((nc, NG, GW, NS), bf16)]
    if write_conv:
        out_specs += [pl.BlockSpec((NG, T, GW), lambda s: (0, cidx(s), 0)),
                      pl.BlockSpec((NG, T, NS), lambda s: (0, cidx(s), 0)),
                      pl.BlockSpec((NG, T, NS), lambda s: (0, cidx(s), 0))]
        out_shape += [jax.ShapeDtypeStruct((NG, L, GW), bf16), jax.ShapeDtypeStruct((NG, L, NS), bf16),
                      jax.ShapeDtypeStruct((NG, L, NS), bf16)]
    out_specs += [pl.BlockSpec((RH, HD, HD), lambda s: (0, 0, 0)), pl.BlockSpec((NG, GW, NS), lambda s: (0, 0, 0))]
    out_shape += [jax.ShapeDtypeStruct((RH, HD, HD), f32), jax.ShapeDtypeStruct((NG, GW, NS), f32)]

    return pl.pallas_call(
        functools.partial(_state_kernel, nc=nc, fwd=fwd, emit=emit, write_conv=write_conv),
        grid=(nc,),
        in_specs=in_specs,
        out_specs=out_specs,
        out_shape=out_shape,
        scratch_shapes=[pltpu.VMEM((RH, HD, HD), f32), pltpu.VMEM((NG, GW, NS), f32),
                        pltpu.VMEM((2 * SH, T), f32), pltpu.VMEM((2 * SH, T), f32)],
        compiler_params=_cp(("arbitrary",)),
        name="state_sweep_" + ("f" if fwd else "b") + ("_emit" if emit else ""),
    )(*args)


def _fwd_kernel(q_ref, k_ref, v_ref, gr_in_ref, z_ref, xs_ref, b_ref, c_ref, dt_ref, gret_ref, gssd_ref,
                wret_ref, ef_ref, eb_ref, wf_ref, rdec_ref, bias_ref, arow_ref, dexp_ref, snw_ref, rnw_ref,
                h0r_ref, h0s_ref, dsel_ref, esel_ref, yr_ref, ys_ref,
                hr, hs, ypre, ap_s, ep_s, apT_s, wT_s, decT_s):
    s = pl.program_id(0)

    @pl.when(s == 0)
    def _():
        hr[...] = h0r_ref[...]
        hs[...] = h0s_ref[...]

    dt, la, acs, tot = _decay_prologue(dt_ref, bias_ref, arow_ref)
    lane = lax.broadcasted_iota(jnp.int32, (T, 2 * SH), 1)
    is_f = lane < SH
    a1 = jnp.where(is_f, acs, acs - la)
    e1 = jnp.exp(jnp.where(is_f, acs, tot - (acs - la)))
    a2 = a1 * LOG2E
    ldt = jnp.maximum(jnp.log2(dt), LOG2_FLOOR)
    for k, part in enumerate(_split3(a2)):
        ap_s[k] = part
    for k, part in enumerate(_split3(jnp.where(is_f, ldt - a2, ldt + a2))):
        apT_s[k] = part.T
    for k, part in enumerate(_split3(e1)):
        ep_s[k] = part
    wT_s[...] = (jnp.exp(tot - acs) * dt).T
    decT_s[...] = jnp.broadcast_to(jnp.exp(tot), (T, 2 * SH)).T

    ii = lax.broadcasted_iota(jnp.int32, (T, T), 0)
    jj = lax.broadcasted_iota(jnp.int32, (T, T), 1)
    lower = jj <= ii
    part_masks = [((lane >= k * R) & (lane < (k + 1) * R)) | ((lane >= SH + k * R) & (lane < SH + (k + 1) * R))
                  for k in range(3)]
    ones_lanes = (lane >= ONES_LANE) & (lane < ONES_LANE + 3)
    row16 = lax.broadcasted_iota(jnp.int32, (16, T), 0)
    lane_p = lax.broadcasted_iota(jnp.int32, (T, 2 * P), 1)

    def pack(parts_ref, g, with_ones):
        acc = jnp.where(ones_lanes, 1.0, 0.0) if with_ones else jnp.zeros((T, 2 * SH), f32)
        for k in range(3):
            shift = (2 * SH - g * R + k * R) % (2 * SH)
            acc = acc + jnp.where(part_masks[k], pltpu.roll(parts_ref[k], shift, 1), 0.0)
        return acc.astype(bf16)

    def grp(g, ssq):
        c16 = c_ref[g]
        b16 = b_ref[g]
        sc = _nt(c16, b16)
        r0 = pl.multiple_of(g * R, R)
        lhs_a = pack(ap_s, g, True)
        lhs_e = pack(ep_s, g, False)
        cf_rows = [apT_s[k, pl.ds(r0, R), :] for k in range(3)]
        cb_rows = [apT_s[k, pl.ds(SH + r0, R), :] for k in range(3)]
        xg = xs_ref[g]
        zg = z_ref[g].astype(f32)
        dg = dexp_ref[g]
        hcat = jnp.concatenate([hs[g].astype(bf16), gssd_ref[0, g]], axis=0)
        ycross = _nt(c16, hcat)
        efb = _dot(lhs_e, esel_ref[...])
        ms = []
        for r in range(R):
            lo = jnp.zeros((16, T), f32)
            up = jnp.zeros((16, T), f32)
            for k in range(3):
                lo = jnp.where(row16 == k, cf_rows[k][r:r + 1, :], lo)
                up = jnp.where(row16 == k, cb_rows[k][r:r + 1, :], up)
            dyn = jnp.concatenate([lo, up], axis=1).astype(bf16)
            rhs = jnp.concatenate([dsel_ref[r, 0:ONES_LANE, :], dyn, dsel_ref[r, ONES_LANE + 16:, :]], axis=0)
            dmat = _dot(lhs_a, rhs)
            wm = jnp.exp2(jnp.where(lower, dmat[:, :T], dmat[:, T:]))
            ms.append((sc * wm).astype(bf16))
        ys_parts = []
        for t in range(R // 2):
            xp = xg[:, t * 2 * P:(t + 1) * 2 * P]
            zero = jnp.zeros_like(xp)
            rhs = jnp.concatenate([jnp.where(lane_p < P, xp, zero), jnp.where(lane_p >= P, xp, zero)], axis=0)
            ys_parts.append(_dot(jnp.concatenate([ms[2 * t], ms[2 * t + 1]], axis=1), rhs))
        yg = jnp.concatenate(ys_parts, axis=1)
        yg = yg + ycross[:, :GW] * efb[:, :GW] + ycross[:, GW:] * efb[:, GW:]
        yg = (yg + xg.astype(f32) * dg) * _silu(zg)
        ypre[g] = yg
        ssq = ssq + jnp.sum(yg * yg, axis=1, keepdims=True)
        _ssd_state_update(hs, g, xg.astype(f32), b16, wT_s[pl.ds(r0, R), :], decT_s[pl.ds(r0, R), :])
        return ssq

    ssq = lax.fori_loop(0, NG, grp, jnp.zeros((T, 1), f32), unroll=True)
    rs = lax.rsqrt(ssq * (1.0 / SI) + EPS)

    def fin(g, carry):
        ys_ref[g] = ((ypre[g] * rs) * snw_ref[g]).astype(ys_ref.dtype)
        return carry

    lax.fori_loop(0, NG, fin, 0)

    def head(h, carry):
        q16 = q_ref[h]
        k16 = k_ref[h]
        v16 = v_ref[h]
        m16 = (_nt(q16, k16) * wret_ref[h]).astype(bf16)
        y = _dot(m16, v16)
        qf = q16.astype(f32)
        lc = jnp.concatenate([(qf * ef_ref[h]).astype(bf16), (qf * eb_ref[h]).astype(bf16)], axis=1)
        hcat = jnp.concatenate([hr[h].astype(bf16), gret_ref[0, h]], axis=1)
        y = y + _nt(lc, hcat)
        mu = jnp.mean(y, axis=-1, keepdims=True)
        d = y - mu
        yn = d * lax.rsqrt(jnp.mean(d * d, axis=-1, keepdims=True) + EPS)
        gg = gr_in_ref[h].astype(f32)
        yr_ref[h] = ((yn * rnw_ref[h]) * _silu(gg)).astype(yr_ref.dtype)
        _ret_state_update(hr, h, v16, k16, wf_ref[h], rdec_ref[h][0:1, :])
        return carry

    lax.fori_loop(0, RH, head, 0, unroll=True)


def _fwd_sweep(L, qk, vg, zx, xc, bc, cc, dt, gret, gssd, tabs, bias, arow, dexp, snw, rnw, h0r, h0s):
    nc = L // T
    wret, ef, eb, wf, _, rdec = tabs

    def blk(lead, n, width):
        return pl.BlockSpec((n, T, width), lambda s: (lead, s, 0))

    def whole(a):
        nd = a.ndim
        return pl.BlockSpec(a.shape, lambda s: (0,) * nd)

    dsel, esel = _selector_constants()
    small = [wret, ef, eb, wf, rdec, bias, arow, dexp, snw, rnw, h0r, h0s, dsel, esel]
    in_specs = [blk(0, RH, HD), blk(1, RH, HD), blk(0, RH, HD), blk(1, RH, HD),
                blk(0, NG, GW), blk(0, NG, GW), blk(0, NG, NS), blk(0, NG, NS),
                pl.BlockSpec((T, 2 * SH), lambda s: (s, 0)),
                pl.BlockSpec((1, RH, HD, HD), lambda s: (s, 0, 0, 0)),
                pl.BlockSpec((1, NG, GW, NS), lambda s: (s, 0, 0, 0))] + [whole(a) for a in small]
    return pl.pallas_call(
        _fwd_kernel,
        grid=(nc,),
        in_specs=in_specs,
        out_specs=[blk(0, RH, HD), blk(0, NG, GW)],
        out_shape=[jax.ShapeDtypeStruct((RH, L, HD), bf16), jax.ShapeDtypeStruct((NG, L, GW), bf16)],
        scratch_shapes=[pltpu.VMEM((RH, HD, HD), f32), pltpu.VMEM((NG, GW, NS), f32), pltpu.VMEM((NG, T, GW), f32),
                        pltpu.VMEM((3, T, 2 * SH), f32), pltpu.VMEM((3, T, 2 * SH), f32),
                        pltpu.VMEM((3, 2 * SH, T), f32)]
        + [pltpu.VMEM((2 * SH, T), f32)] * 2,
        compiler_params=_cp(("arbitrary",)),
        name="fwd_sweep",
    )(qk, qk, vg, vg, zx, xc, bc, cc, dt, gret, gssd, *small)


def _branch_out_kernel(yr_ref, ys_ref, wr_ref, ws_ref, gr_ref, gs_ref, o_ref):
    lhs_r = jnp.concatenate([yr_ref[h] for h in range(RH)], axis=1)
    lhs_s = jnp.concatenate([ys_ref[g] for g in range(NG)], axis=1)
    for nb in range(o_ref.shape[1] // MXU_N):
        cols = slice(nb * MXU_N, (nb + 1) * MXU_N)
        acc_r = _dot(lhs_r, wr_ref[:, cols])
        acc_s = _dot(lhs_s, ws_ref[:, cols])
        m = _sigmoid(gr_ref[:, cols].astype(f32)) * acc_r + _sigmoid(gs_ref[:, cols].astype(f32)) * acc_s
        o_ref[:, cols] = m.astype(o_ref.dtype)


def _branch_out(yr, ys, wr, ws, gates):
    L = yr.shape[1]
    tm, tn = min(L, 512), 512
    nj = D // tn
    return pl.pallas_call(
        _branch_out_kernel,
        grid=(L // tm, nj),
        in_specs=[pl.BlockSpec((RH, tm, HD), lambda i, j: (0, i, 0)),
                  pl.BlockSpec((NG, tm, GW), lambda i, j: (0, i, 0)),
                  pl.BlockSpec((D, tn), lambda i, j: (0, j)),
                  pl.BlockSpec((SI, tn), lambda i, j: (0, j)),
                  pl.BlockSpec((tm, tn), lambda i, j: (i, j)),
                  pl.BlockSpec((tm, tn), lambda i, j: (i, nj + j))],
        out_specs=pl.BlockSpec((tm, tn), lambda i, j: (i, j)),
        out_shape=jax.ShapeDtypeStruct((L, D), bf16),
        compiler_params=_cp(("parallel", "parallel")),
        name="branch_out",
    )(yr, ys, wr, ws, gates, gates)


def _resid_kernel(m_ref, w_ref, x_ref, g_ref, nw_ref, sc_ref, sh_ref, h_ref, f_ref):
    h = x_ref[...] + g_ref[...] * _dot(m_ref[...], w_ref[...])
    h_ref[...] = h
    f_ref[...] = _rms_mod(h, nw_ref[...], sc_ref[...], sh_ref[...]).astype(f_ref.dtype)


def _mix_residual(m, w, x, gate, nw, sc, sh):
    L = m.shape[0]
    tm = min(L, 512)
    vec = pl.BlockSpec((1, D), lambda i: (0, 0))
    row = pl.BlockSpec((tm, D), lambda i: (i, 0))
    return pl.pallas_call(
        _resid_kernel,
        grid=(L // tm,),
        in_specs=[row, pl.BlockSpec((D, D), lambda i: (0, 0)), row, vec, vec, vec, vec],
        out_specs=[row, row],
        out_shape=[jax.ShapeDtypeStruct((L, D), f32), jax.ShapeDtypeStruct((L, D), bf16)],
        compiler_params=_cp(("parallel",)),
        name="mix_residual",
    )(m, w, x, gate, nw, sc, sh)


def _mlp1_kernel(f_ref, w_ref, o_ref, *, tn):
    for nb in range(tn // MXU_N):
        a = jnp.maximum(_dot(f_ref[...], w_ref[:, nb * MXU_N:(nb + 1) * MXU_N]), 0.0)
        o_ref[:, nb * MXU_N:(nb + 1) * MXU_N] = (a * a).astype(o_ref.dtype)


def _mlp1(f, w):
    L = f.shape[0]
    tm, tn = min(L, 1024), 1024
    return pl.pallas_call(
        functools.partial(_mlp1_kernel, tn=tn),
        grid=(L // tm, DFF // tn),
        in_specs=[pl.BlockSpec((tm, D), lambda i, j: (i, 0)), pl.BlockSpec((D, tn), lambda i, j: (0, j))],
        out_specs=pl.BlockSpec((tm, tn), lambda i, j: (i, j)),
        out_shape=jax.ShapeDtypeStruct((L, DFF), bf16),
        compiler_params=_cp(("parallel", "parallel")),
        name="mlp1",
    )(f, w)


def _mlp2_kernel(a_ref, w_ref, h_ref, g_ref, fw_ref, o_ref, acc, *, nk):
    kk = pl.program_id(1)

    @pl.when(kk == 0)
    def _():
        acc[...] = jnp.zeros_like(acc)

    acc[...] += _dot(a_ref[...], w_ref[...])

    @pl.when(kk == nk - 1)
    def _():
        h2 = h_ref[...] + g_ref[...] * acc[...]
        y = h2 * lax.rsqrt(jnp.mean(h2 * h2, axis=-1, keepdims=True) + EPS)
        o_ref[...] = y * fw_ref[...]


def _mlp2_final(a, w, h, gate, fw):
    L = a.shape[0]
    tm, tk = min(L, 512), 2048
    nk = DFF // tk
    vec = pl.BlockSpec((1, D), lambda i, k: (0, 0))
    return pl.pallas_call(
        functools.partial(_mlp2_kernel, nk=nk),
        grid=(L // tm, nk),
        in_specs=[pl.BlockSpec((tm, tk), lambda i, k: (i, k)),
                  pl.BlockSpec((tk, D), lambda i, k: (k, 0)),
                  pl.BlockSpec((tm, D), lambda i, k: (i, 0)), vec, vec],
        out_specs=pl.BlockSpec((tm, D), lambda i, k: (i, 0)),
        out_shape=jax.ShapeDtypeStruct((L, D), f32),
        scratch_shapes=[pltpu.VMEM((tm, D), f32)],
        compiler_params=_cp(("parallel", "arbitrary")),
        name="mlp2_final",
    )(a, w, h, gate, fw)


def _split_conv(conv_w, conv_b):
    def grp(w, b, width):
        return (jnp.transpose(w.reshape(KC, NG, width), (1, 0, 2)), b.reshape(NG, 1, width))
    cwx, cbx = grp(conv_w[:, :SI], conv_b[:SI], GW)
    cwb, cbb = grp(conv_w[:, SI:SI + NG * NS], conv_b[SI:SI + NG * NS], NS)
    cwc, cbc = grp(conv_w[:, SI + NG * NS:], conv_b[SI + NG * NS:], NS)
    return cwx, cbx, cwb, cbb, cwc, cbc


def kernel(x, c, ctx, c_ctx, w_mod, b_mod, norm1_w, w_in, conv_w, conv_b, ret_decay_logit, ret_norm_w,
           ssd_a_log, ssd_dt_bias, ssd_d, ssd_norm_w, w_ret_out, w_ssd_out, w_o, norm2_w, w_mlp1, w_mlp2,
           final_norm_w):
    depth = w_mod.shape[0]
    assert depth == 1 and x.shape[0] == 1 and x.shape[2] == D
    L = x.shape[1]
    Lc = ctx.shape[1]
    assert L % T == 0 and Lc % T == 0 and L % GRID_W == 0
    xl = x[0]
    xcx = ctx[0]
    ly = 0

    a8 = jnp.zeros((8, D), f32).at[0].set(c[0]).at[1].set(c_ctx)
    mod = _modulation(a8, w_mod[ly], b_mod[ly][None, :])
    sh_a, sc_a, g_a, sh_f, sc_f, g_f = [mod[0:1, i * D:(i + 1) * D] for i in range(6)]
    csh_a, csc_a = mod[1:2, 0:D], mod[1:2, D:2 * D]

    wi = w_in[ly]
    o_q, o_k, o_v, o_g, o_z, o_x = 0, D, 2 * D, 3 * D, 4 * D, 4 * D + SI
    o_b = o_x + SI
    o_c = o_b + NG * NS
    o_dt = o_c + NG * NS
    o_gate = o_dt + 2 * SH
    w_qk = wi[:, o_q:o_v].reshape(D, 2 * RH, HD)[:, :, _rope_perm()].reshape(D, 2 * D).astype(bf16)
    w_k = w_qk[:, D:]
    w_vgbc = jnp.concatenate([wi[:, o_v:o_z], wi[:, o_b:o_dt]], axis=1).astype(bf16)
    w_vb = jnp.concatenate([wi[:, o_v:o_g], wi[:, o_b:o_c]], axis=1).astype(bf16)
    w_zx = wi[:, o_z:o_b].astype(bf16)
    w_x = wi[:, o_x:o_b].astype(bf16)
    w_dt = wi[:, o_dt:o_gate].astype(bf16)
    w_gate = wi[:, o_gate:].astype(bf16)

    convw = _split_conv(conv_w[ly], conv_b[ly])
    bias = ssd_dt_bias[ly].reshape(1, 2 * SH)
    arow = (-jnp.exp(ssd_a_log[ly])).reshape(1, 2 * SH)
    dexp = jnp.repeat(ssd_d[ly], P).reshape(NG, 1, GW)
    snw = ssd_norm_w[ly].reshape(NG, 1, GW)
    rnw = ret_norm_w[ly].reshape(RH, 1, HD)
    nw1 = norm1_w[ly][None, :]

    tabs = _ret_tables(ret_decay_logit[ly])
    _, _, _, wf_t, wb_t, rdec = tabs
    zr = jnp.zeros((RH, HD, HD), f32)
    zs = jnp.zeros((NG, GW, NS), f32)

    uc = _norm_mod(xcx, nw1, csc_a, csh_a)
    ident = (jnp.ones((Lc, LANES), f32), jnp.zeros((Lc, LANES), f32))
    kc = _project(uc, w_k, bw=HD, tn=512, rope=ident, scale_from=0, name="ctx_k")
    vbc = _project(uc, w_vb, bw=HD, tn=512, name="ctx_vb")
    xsc = _project(uc, w_x, bw=GW, tn=GW, name="ctx_x")
    dtc = _project(uc, w_dt, bw=0, tn=2 * SH, out_dtype=f32, name="ctx_dt")
    common = dict(k=kc, k_blk=0, v=vbc, v_blk=0, xs=xsc, xs_blk=0, bsrc=vbc, b_blk=RH // NG, c_blk=0, dt=dtc,
                  convw=convw, bias=bias, arow=arow, rdec=rdec, h0r=zr, h0s=zs)
    cfr, cfs = _state_sweep(Lc, fwd=True, emit=False, write_conv=False, rw=wf_t, **common)
    cbr, cbs = _state_sweep(Lc, fwd=False, emit=False, write_conv=False, rw=wb_t, **common)

    u = _norm_mod(xl, nw1, sc_a, sh_a)
    rope = _rope_tables(L)
    qk = _project(u, w_qk, bw=HD, tn=1024, rope=rope, scale_from=D // 1024, name="lat_qk")
    vg = _project(u, w_vgbc, bw=HD, tn=1024, name="lat_vgbc")
    zx = _project(u, w_zx, bw=GW, tn=1024, name="lat_zx")
    gates = _project(u, w_gate, bw=0, tn=1024, name="lat_gates")
    dtl = _project(u, w_dt, bw=0, tn=2 * SH, out_dtype=f32, name="lat_dt")

    gret, gssd, xcv, bcv, ccv, _, _ = _state_sweep(
        L, fwd=False, emit=True, write_conv=True, k=qk, k_blk=1, v=vg, v_blk=0, xs=zx, xs_blk=1, bsrc=vg,
        b_blk=2 * RH // NG, c_blk=2 * RH // NG + 1, dt=dtl, convw=convw, bias=bias, arow=arow, rw=wb_t, rdec=rdec,
        h0r=cbr, h0s=cbs)
    yr, ys = _fwd_sweep(L, qk, vg, zx, xcv, bcv, ccv, dtl, gret, gssd, tabs, bias, arow, dexp, snw, rnw, cfr, cfs)

    m = _branch_out(yr, ys, w_ret_out[ly].astype(bf16), w_ssd_out[ly].astype(bf16), gates)
    h1, f = _mix_residual(m, w_o[ly].astype(bf16), xl, g_a, norm2_w[ly][None, :], sc_f, sh_f)
    a = _mlp1(f, w_mlp1[ly].astype(bf16))
    out = _mlp2_final(a, w_mlp2[ly].astype(bf16), h1, g_f, final_norm_w[None, :])
    return out[None]
```

```python
import functools
import math

import numpy as np
import jax
import jax.numpy as jnp
from jax import lax
from jax.experimental import pallas as pl
from jax.experimental.pallas import tpu as pltpu

f32 = jnp.float32
bf16 = jnp.bfloat16

D = 2048
T = 128
GRID_W = 64
HD = 128
RH = D // HD
SI = 2 * D
P = 64
SH = SI // P
NS = 128
NG = 8
R = SH // NG
GW = R * P
KC = 5
DFF = 4 * D
ROPE_BASE = 10000.0
EPS = 1e-6
LANES = 128
MXU_N = 256
HALO = 16
LOG2E = 1.0 / math.log(2.0)
LOG2_FLOOR = -300.0
ONES_LANE = 96
VMEM_LIMIT = 48 * 1024 * 1024


def _cp(sem):
    return pltpu.CompilerParams(dimension_semantics=sem, vmem_limit_bytes=VMEM_LIMIT)


def _nt(a, b):
    return lax.dot_general(a, b, (((1,), (1,)), ((), ())), preferred_element_type=f32)


def _dot(a, b):
    return jnp.dot(a, b, preferred_element_type=f32)


def _sigmoid(x):
    return 0.5 + 0.5 * jnp.tanh(0.5 * x)


def _silu(x):
    h = 0.5 * x
    return h + h * jnp.tanh(h)


def _softplus(x):
    return jnp.maximum(x, 0.0) + jnp.log1p(jnp.exp(-jnp.abs(x)))


def _log_sigmoid(x):
    return jnp.minimum(x, 0.0) - jnp.log1p(jnp.exp(-jnp.abs(x)))


def _split3(x):
    p1 = x.astype(bf16).astype(f32)
    r1 = x - p1
    p2 = r1.astype(bf16).astype(f32)
    p3 = (r1 - p2).astype(bf16).astype(f32)
    return p1, p2, p3


def _cumsum_rows(la, tri):
    a1, a2, a3 = _split3(la)
    return _dot(tri, a1.astype(bf16)) + _dot(tri, a2.astype(bf16)) + _dot(tri, a3.astype(bf16))


def _selector_constants():
    dsel = np.zeros((R, 2 * SH, 2 * T), np.float32)
    esel = np.zeros((2 * SH, 2 * GW), np.float32)
    for r in range(R):
        for k in range(3):
            dsel[r, k * R + r, :T] = 1.0
            dsel[r, SH + k * R + r, T:] = -1.0
            esel[k * R + r, r * P:(r + 1) * P] = 1.0
            esel[SH + k * R + r, GW + r * P:GW + (r + 1) * P] = 1.0
    return jnp.asarray(dsel, bf16), jnp.asarray(esel, bf16)


def _tri_incl():
    ii = lax.broadcasted_iota(jnp.int32, (T, T), 0)
    jj = lax.broadcasted_iota(jnp.int32, (T, T), 1)
    return jnp.where(jj <= ii, 1.0, 0.0).astype(bf16)


def _rope_tab_kernel(cos_ref, sin_ref, *, tm):
    i = pl.program_id(0)
    pos = i * tm + lax.broadcasted_iota(jnp.int32, (tm, LANES), 0)
    lane = lax.broadcasted_iota(jnp.int32, (tm, LANES), 1)
    row = (pos // GRID_W).astype(f32)
    col = (pos % GRID_W).astype(f32)
    fidx = (lane % (HD // 4)).astype(f32)
    freqs = jnp.exp(fidx * (-math.log(ROPE_BASE) / (HD // 4)))
    ang = jnp.where((lane % (HD // 2)) < HD // 4, row, col) * freqs
    s = jnp.sin(ang)
    cos_ref[...] = jnp.cos(ang)
    sin_ref[...] = jnp.where(lane < HD // 2, -s, s)


def _rope_tables(L):
    tm = min(L, 512)
    return pl.pallas_call(
        functools.partial(_rope_tab_kernel, tm=tm),
        grid=(L // tm,),
        out_specs=[pl.BlockSpec((tm, LANES), lambda i: (i, 0))] * 2,
        out_shape=[jax.ShapeDtypeStruct((L, LANES), f32)] * 2,
        compiler_params=_cp(("parallel",)),
        name="rope_tables",
    )()


def _ret_tab_kernel(lg_ref, w_ref, ef_ref, eb_ref, wf_ref, wb_ref, dec_ref):
    lf = _log_sigmoid(lg_ref[0])
    lb = _log_sigmoid(lg_ref[1])
    ii = lax.broadcasted_iota(jnp.int32, (T, T), 0)
    jj = lax.broadcasted_iota(jnp.int32, (T, T), 1)
    dl = (ii - jj).astype(f32)
    w_ref[...] = jnp.exp(jnp.where(jj <= ii, dl * lf, -dl * lb))
    idx = lax.broadcasted_iota(jnp.int32, (T, LANES), 0).astype(f32)
    ef_ref[...] = jnp.exp((idx + 1.0) * lf)
    eb_ref[...] = jnp.exp((T - idx) * lb)
    wf_ref[...] = jnp.exp((T - 1.0 - idx) * lf)
    wb_ref[...] = jnp.exp(idx * lb)
    srow = lax.broadcasted_iota(jnp.int32, (8, LANES), 0)
    dec_ref[...] = jnp.where(srow == 0, jnp.exp(T * lf), jnp.exp(T * lb))


def _ret_tables(logit):
    lg = jnp.broadcast_to(logit[:, :, None, None], (2, RH, 1, LANES))
    tab = pl.BlockSpec((None, T, LANES), lambda h: (h, 0, 0))
    return pl.pallas_call(
        _ret_tab_kernel,
        grid=(RH,),
        in_specs=[pl.BlockSpec((2, None, 1, LANES), lambda h: (0, h, 0, 0))],
        out_specs=[tab, tab, tab, tab, tab, pl.BlockSpec((None, 8, LANES), lambda h: (h, 0, 0))],
        out_shape=[jax.ShapeDtypeStruct((RH, T, T), f32)] + [jax.ShapeDtypeStruct((RH, T, LANES), f32)] * 4
        + [jax.ShapeDtypeStruct((RH, 8, LANES), f32)],
        compiler_params=_cp(("parallel",)),
        name="ret_tables",
    )(lg)


def _mod_kernel(a_ref, w_ref, b_ref, o_ref):
    a = _silu(a_ref[...])
    o_ref[...] = _dot(a, w_ref[...]) + b_ref[...]


def _modulation(a8, w, b):
    tn = 1024
    n = w.shape[1]
    return pl.pallas_call(
        _mod_kernel,
        grid=(n // tn,),
        in_specs=[pl.BlockSpec((8, D), lambda j: (0, 0)),
                  pl.BlockSpec((D, tn), lambda j: (0, j)),
                  pl.BlockSpec((1, tn), lambda j: (0, j))],
        out_specs=pl.BlockSpec((8, tn), lambda j: (0, j)),
        out_shape=jax.ShapeDtypeStruct((8, n), f32),
        compiler_params=_cp(("parallel",)),
        name="modulation",
    )(a8, w, b)


def _rms_mod(x, nw, sc, sh):
    y = x * lax.rsqrt(jnp.mean(x * x, axis=-1, keepdims=True) + EPS)
    return (y * nw) * (1.0 + sc) + sh


def _norm_kernel(x_ref, nw_ref, sc_ref, sh_ref, o_ref):
    o_ref[...] = _rms_mod(x_ref[...], nw_ref[...], sc_ref[...], sh_ref[...]).astype(o_ref.dtype)


def _norm_mod(x, nw, sc, sh):
    L = x.shape[0]
    tm = min(L, 512)
    vec = pl.BlockSpec((1, D), lambda i: (0, 0))
    return pl.pallas_call(
        _norm_kernel,
        grid=(L // tm,),
        in_specs=[pl.BlockSpec((tm, D), lambda i: (i, 0)), vec, vec, vec],
        out_specs=pl.BlockSpec((tm, D), lambda i: (i, 0)),
        out_shape=jax.ShapeDtypeStruct((L, D), bf16),
        compiler_params=_cp(("parallel",)),
        name="norm_mod",
    )(x, nw, sc, sh)


def _proj_kernel(u_ref, w_ref, *rest, bw, tn, rope, scale_from):
    if rope:
        cos_ref, sin_ref, o_ref = rest
        scale = jnp.where(pl.program_id(1) >= scale_from, HD ** -0.5, 1.0).astype(f32)
        cos = cos_ref[...] * scale
        sin = sin_ref[...] * scale
    else:
        (o_ref,) = rest
    sw = min(tn, MXU_N)
    for nb in range(tn // sw):
        acc = _dot(u_ref[...], w_ref[:, nb * sw:(nb + 1) * sw])
        if rope:
            for hh in range(sw // HD):
                t = acc[:, hh * HD:(hh + 1) * HD]
                o_ref[nb * (sw // HD) + hh] = (t * cos + pltpu.roll(t, HD // 2, 1) * sin).astype(o_ref.dtype)
        elif bw and bw <= sw:
            for hh in range(sw // bw):
                o_ref[nb * (sw // bw) + hh] = acc[:, hh * bw:(hh + 1) * bw].astype(o_ref.dtype)
        elif bw:
            per = bw // sw
            o_ref[nb // per, :, (nb % per) * sw:(nb % per + 1) * sw] = acc.astype(o_ref.dtype)
        else:
            o_ref[:, nb * sw:(nb + 1) * sw] = acc.astype(o_ref.dtype)


def _project(u, w, *, col0, ncols, bw, tn, out_dtype=bf16, rope=None, scale_from=0, name):
    M = u.shape[0]
    N = ncols
    assert col0 % tn == 0 and ncols % tn == 0
    jb = col0 // tn
    tm = min(M, 1024)
    in_specs = [pl.BlockSpec((tm, D), lambda i, j: (i, 0)), pl.BlockSpec((D, tn), lambda i, j: (0, jb + j))]
    args = [u, w]
    if rope is not None:
        in_specs += [pl.BlockSpec((tm, LANES), lambda i, j: (i, 0))] * 2
        args += list(rope)
    if bw:
        out_spec = pl.BlockSpec((tn // bw, tm, bw), lambda i, j: (j, i, 0))
        out_shape = jax.ShapeDtypeStruct((N // bw, M, bw), out_dtype)
    else:
        out_spec = pl.BlockSpec((tm, tn), lambda i, j: (i, j))
        out_shape = jax.ShapeDtypeStruct((M, N), out_dtype)
    return pl.pallas_call(
        functools.partial(_proj_kernel, bw=bw, tn=tn, rope=rope is not None, scale_from=scale_from),
        grid=(M // tm, N // tn),
        in_specs=in_specs,
        out_specs=out_spec,
        out_shape=out_shape,
        compiler_params=_cp(("parallel", "parallel")),
        name=name,
    )(*args)


def _shift_matrix():
    rows = lax.broadcasted_iota(jnp.int32, (4 * T, T + 2 * HALO), 0)
    cols = lax.broadcasted_iota(jnp.int32, (4 * T, T + 2 * HALO), 1)
    blk = rows // T
    tap = jnp.where(blk < KC // 2, blk, blk + 1)
    return jnp.where(cols == (rows - blk * T) + HALO + tap - KC // 2, 1.0, 0.0).astype(bf16)


def _conv_silu(shift, main, prev, nxt, cw, cb, has_prev, has_next):
    zero = jnp.zeros_like(prev)
    ext = jnp.concatenate([jnp.where(has_prev, prev, zero), main, jnp.where(has_next, nxt, zero)], axis=0)
    sh = _dot(shift, ext)
    taps = [sh[0:T], sh[T:2 * T], main.astype(f32), sh[2 * T:3 * T], sh[3 * T:4 * T]]
    acc = cb
    for j in range(KC):
        acc = acc + taps[j] * cw[j:j + 1, :]
    return _silu(acc)


def _decay_prologue(dt_ref, bias_ref, arow_ref):
    dt = _softplus(dt_ref[...] + bias_ref[...])
    la = dt * arow_ref[...]
    acs = _cumsum_rows(la, _tri_incl())
    tot = acs[T - 1:T, :]
    return dt, la, acs, tot


def _transpose_blocks(x):
    n = x.shape[1] // LANES
    return jnp.concatenate([x[:, b * LANES:(b + 1) * LANES].T for b in range(n)], axis=0)


def _ssd_state_update(hs_ref, g, xc, bc16, w_rows, dec_rows):
    xT = _transpose_blocks(xc)
    lhs = jnp.concatenate(
        [(xT[r * P:(r + 1) * P, :] * w_rows[r:r + 1, :]).astype(bf16) for r in range(R)], axis=0)
    upd = _dot(lhs, bc16)
    old = hs_ref[g]
    hs_ref[g] = jnp.concatenate(
        [old[r * P:(r + 1) * P, :] * dec_rows[r:r + 1, :] + upd[r * P:(r + 1) * P, :] for r in range(R)], axis=0)


def _ret_state_update(hr_ref, h, v16, k16, wcol, dec_row):
    vw = (v16.astype(f32) * wcol).T.astype(bf16)
    hr_ref[h] = hr_ref[h] * dec_row + _dot(vw, k16)


def _state_kernel(*refs, nc, fwd, emit, write_conv):
    it = iter(refs)
    k_ref, v_ref = next(it), next(it)
    xs_ref, xs_p, xs_n = next(it), next(it), next(it)
    b_ref, b_p, b_n = next(it), next(it), next(it)
    if write_conv:
        c_ref, c_p, c_n = next(it), next(it), next(it)
    dt_ref = next(it)
    cwx, cbx, cwb, cbb = next(it), next(it), next(it), next(it)
    if write_conv:
        cwc, cbc = next(it), next(it)
    bias_ref, arow_ref = next(it), next(it)
    rw_ref, rdec_ref = next(it), next(it)
    h0r_ref, h0s_ref = next(it), next(it)
    if emit:
        gr_ref, gs_ref = next(it), next(it)
    if write_conv:
        xo_ref, bco_ref = next(it), next(it)
    hfr_ref, hfs_ref = next(it), next(it)
    hr, hs, wT_s, decT_s = next(it), next(it), next(it), next(it)

    s = pl.program_id(0)
    c = s if fwd else nc - 1 - s
    has_prev = c > 0
    has_next = c < nc - 1

    @pl.when(s == 0)
    def _():
        hr[...] = h0r_ref[...]
        hs[...] = h0s_ref[...]

    if emit:
        gr_ref[0] = hr[...].astype(bf16)
        gs_ref[0] = hs[...].astype(bf16)

    dt, la, acs, tot = _decay_prologue(dt_ref, bias_ref, arow_ref)
    wexp = jnp.exp(tot - acs) if fwd else jnp.exp(acs - la)
    wT_s[...] = (wexp * dt).T
    decT_s[...] = jnp.broadcast_to(jnp.exp(tot), (T, 2 * SH)).T
    off = 0 if fwd else SH

    shift = _shift_matrix()

    for g in range(NG):
        xc16 = _conv_silu(shift, xs_ref[g], xs_p[g], xs_n[g], cwx[g], cbx[g], has_prev, has_next).astype(bf16)
        if write_conv:
            def cat(a, b):
                return jnp.concatenate([a, b], axis=1)
            bcc = _conv_silu(shift, cat(b_ref[g], c_ref[g]), cat(b_p[g], c_p[g]), cat(b_n[g], c_n[g]),
                             cat(cwb[g], cwc[g]), cat(cbb[g], cbc[g]), has_prev, has_next).astype(bf16)
            bc16 = bcc[:, :NS]
            xo_ref[g] = xc16
            bco_ref[g] = bcc
        else:
            bc16 = _conv_silu(shift, b_ref[g], b_p[g], b_n[g], cwb[g], cbb[g], has_prev, has_next).astype(bf16)
        r0 = off + g * R
        _ssd_state_update(hs, g, xc16.astype(f32), bc16, wT_s[r0:r0 + R, :], decT_s[r0:r0 + R, :])

    dsel = 0 if fwd else 1
    for h in range(RH):
        _ret_state_update(hr, h, v_ref[h], k_ref[h], rw_ref[h], rdec_ref[h][dsel:dsel + 1, :])

    @pl.when(s == nc - 1)
    def _():
        hfr_ref[...] = hr[...]
        hfs_ref[...] = hs[...]


def _state_sweep(L, *, fwd, emit, write_conv, k, k_blk, v, v_blk, xs, xs_blk, bsrc, b_blk, c_blk, dt,
                 convw, bias, arow, rw, rdec, h0r, h0s):
    nc = L // T
    tb = T // HALO
    nrb = L // HALO

    def cidx(s):
        return s if fwd else nc - 1 - s

    def main(lead, n, width):
        return pl.BlockSpec((n, T, width), lambda s: (lead, cidx(s), 0))

    def prev(lead, n, width):
        return pl.BlockSpec((n, HALO, width), lambda s: (lead, jnp.maximum(cidx(s) * tb - 1, 0), 0))

    def nxt(lead, n, width):
        return pl.BlockSpec((n, HALO, width), lambda s: (lead, jnp.minimum((cidx(s) + 1) * tb, nrb - 1), 0))

    def whole(a):
        nd = a.ndim
        return pl.BlockSpec(a.shape, lambda s: (0,) * nd)

    cwx, cbx, cwb, cbb, cwc, cbc = convw
    in_specs = [main(k_blk, RH, HD), main(v_blk, RH, HD),
                main(xs_blk, NG, GW), prev(xs_blk, NG, GW), nxt(xs_blk, NG, GW),
                main(b_blk, NG, NS), prev(b_blk, NG, NS), nxt(b_blk, NG, NS)]
    args = [k, v, xs, xs, xs, bsrc, bsrc, bsrc]
    if write_conv:
        in_specs += [main(c_blk, NG, NS), prev(c_blk, NG, NS), nxt(c_blk, NG, NS)]
        args += [bsrc, bsrc, bsrc]
    in_specs += [pl.BlockSpec((T, 2 * SH), lambda s: (cidx(s), 0))]
    args += [dt]
    small = [cwx, cbx, cwb, cbb] + ([cwc, cbc] if write_conv else []) + [bias, arow, rw, rdec, h0r, h0s]
    in_specs += [whole(a) for a in small]
    args += small

    out_specs, out_shape = [], []
    if emit:
        out_specs += [pl.BlockSpec((1, RH, HD, HD), lambda s: (cidx(s), 0, 0, 0)),
                      pl.BlockSpec((1, NG, GW, NS), lambda s: (cidx(s), 0, 0, 0))]
        out_shape += [jax.ShapeDtypeStruct((nc, RH, HD, HD), bf16), jax.ShapeDtypeStruct((nc, NG, GW, NS), bf16)]
    if write_conv:
        out_specs += [pl.BlockSpec((NG, T, GW), lambda s: (0, cidx(s), 0)),
                      pl.BlockSpec((NG, T, 2 * NS), lambda s: (0, cidx(s), 0))]
        out_shape += [jax.ShapeDtypeStruct((NG, L, GW), bf16), jax.ShapeDtypeStruct((NG, L, 2 * NS), bf16)]
    out_specs += [pl.BlockSpec((RH, HD, HD), lambda s: (0, 0, 0)), pl.BlockSpec((NG, GW, NS), lambda s: (0, 0, 0))]
    out_shape += [jax.ShapeDtypeStruct((RH, HD, HD), f32), jax.ShapeDtypeStruct((NG, GW, NS), f32)]

    return pl.pallas_call(
        functools.partial(_state_kernel, nc=nc, fwd=fwd, emit=emit, write_conv=write_conv),
        grid=(nc,),
        in_specs=in_specs,
        out_specs=out_specs,
        out_shape=out_shape,
        scratch_shapes=[pltpu.VMEM((RH, HD, HD), f32), pltpu.VMEM((NG, GW, NS), f32),
                        pltpu.VMEM((2 * SH, T), f32), pltpu.VMEM((2 * SH, T), f32)],
        compiler_params=_cp(("arbitrary",)),
        name="state_sweep_" + ("f" if fwd else "b") + ("_emit" if emit else ""),
    )(*args)


def _fwd_kernel(qk_ref, vg_ref, z_ref, xs_ref, bc_ref, dt_ref, gret_ref, gssd_ref,
                wret_ref, ef_ref, eb_ref, wf_ref, rdec_ref, bias_ref, arow_ref, dexp_ref, snw_ref, rnw_ref,
                h0r_ref, h0s_ref, dsel_ref, esel_ref, yr_ref, ys_ref,
                hr, hs, ypre, ap_s, ep_s, apT_s, wT_s, decT_s):
    s = pl.program_id(0)

    @pl.when(s == 0)
    def _():
        hr[...] = h0r_ref[...]
        hs[...] = h0s_ref[...]

    dt, la, acs, tot = _decay_prologue(dt_ref, bias_ref, arow_ref)
    lane = lax.broadcasted_iota(jnp.int32, (T, 2 * SH), 1)
    is_f = lane < SH
    a1 = jnp.where(is_f, acs, acs - la)
    e1 = jnp.exp(jnp.where(is_f, acs, tot - (acs - la)))
    a2 = a1 * LOG2E
    ldt = jnp.maximum(jnp.log2(dt), LOG2_FLOOR)
    for k, part in enumerate(_split3(a2)):
        ap_s[k] = part
    for k, part in enumerate(_split3(jnp.where(is_f, ldt - a2, ldt + a2))):
        apT_s[k] = part.T
    for k, part in enumerate(_split3(e1)):
        ep_s[k] = part
    wT_s[...] = (jnp.exp(tot - acs) * dt).T
    decT_s[...] = jnp.broadcast_to(jnp.exp(tot), (T, 2 * SH)).T

    ii = lax.broadcasted_iota(jnp.int32, (T, T), 0)
    jj = lax.broadcasted_iota(jnp.int32, (T, T), 1)
    lower = jj <= ii
    part_masks = [((lane >= k * R) & (lane < (k + 1) * R)) | ((lane >= SH + k * R) & (lane < SH + (k + 1) * R))
                  for k in range(3)]
    ones_lanes = (lane >= ONES_LANE) & (lane < ONES_LANE + 3)
    row16 = lax.broadcasted_iota(jnp.int32, (16, T), 0)
    lane_p = lax.broadcasted_iota(jnp.int32, (T, 2 * P), 1)

    def pack(parts_ref, g, with_ones):
        acc = jnp.where(ones_lanes, 1.0, 0.0) if with_ones else jnp.zeros((T, 2 * SH), f32)
        for k in range(3):
            shift = (2 * SH - g * R + k * R) % (2 * SH)
            acc = acc + jnp.where(part_masks[k], pltpu.roll(parts_ref[k], shift, 1), 0.0)
        return acc.astype(bf16)

    def grp(g, ssq):
        b16 = bc_ref[g, :, 0:NS]
        c16 = bc_ref[g, :, NS:2 * NS]
        sc = _nt(c16, b16)
        r0 = g * R
        lhs_a = pack(ap_s, g, True)
        lhs_e = pack(ep_s, g, False)
        cf_rows = [apT_s[k, r0:r0 + R, :] for k in range(3)]
        cb_rows = [apT_s[k, SH + r0:SH + r0 + R, :] for k in range(3)]
        xg = xs_ref[g]
        zg = z_ref[g].astype(f32)
        dg = dexp_ref[g]
        hcat = jnp.concatenate([hs[g].astype(bf16), gssd_ref[0, g]], axis=0)
        ycross = _nt(c16, hcat)
        efb = _dot(lhs_e, esel_ref[...])
        ms = []
        for r in range(R):
            lo = jnp.zeros((16, T), f32)
            up = jnp.zeros((16, T), f32)
            for k in range(3):
                lo = jnp.where(row16 == k, cf_rows[k][r:r + 1, :], lo)
                up = jnp.where(row16 == k, cb_rows[k][r:r + 1, :], up)
            dyn = jnp.concatenate([lo, up], axis=1).astype(bf16)
            rhs = jnp.concatenate([dsel_ref[r, 0:ONES_LANE, :], dyn, dsel_ref[r, ONES_LANE + 16:, :]], axis=0)
            dmat = _dot(lhs_a, rhs)
            wm = jnp.exp2(jnp.where(lower, dmat[:, :T], dmat[:, T:]))
            ms.append((sc * wm).astype(bf16))
        ys_parts = []
        for t in range(R // 2):
            xp = xg[:, t * 2 * P:(t + 1) * 2 * P]
            zero = jnp.zeros_like(xp)
            rhs = jnp.concatenate([jnp.where(lane_p < P, xp, zero), jnp.where(lane_p >= P, xp, zero)], axis=0)
            ys_parts.append(_dot(jnp.concatenate([ms[2 * t], ms[2 * t + 1]], axis=1), rhs))
        yg = jnp.concatenate(ys_parts, axis=1)
        yg = yg + ycross[:, :GW] * efb[:, :GW] + ycross[:, GW:] * efb[:, GW:]
        yg = (yg + xg.astype(f32) * dg) * _silu(zg)
        ypre[g] = yg
        ssq = ssq + jnp.sum(yg * yg, axis=1, keepdims=True)
        _ssd_state_update(hs, g, xg.astype(f32), b16, wT_s[r0:r0 + R, :], decT_s[r0:r0 + R, :])
        return ssq

    ssq = jnp.zeros((T, 1), f32)
    for g in range(NG):
        ssq = grp(g, ssq)
    rs = lax.rsqrt(ssq * (1.0 / SI) + EPS)
    for g in range(NG):
        ys_ref[:, g * GW:(g + 1) * GW] = ((ypre[g] * rs) * snw_ref[g]).astype(ys_ref.dtype)

    zero_h = jnp.zeros((T, HD), bf16)

    def blockdiag(a, b):
        return jnp.concatenate([jnp.concatenate([a, zero_h], axis=1), jnp.concatenate([zero_h, b], axis=1)], axis=0)

    for t in range(RH // 2):
        pair = (2 * t, 2 * t + 1)
        q16 = [qk_ref[h] for h in pair]
        k16 = [qk_ref[RH + h] for h in pair]
        v16 = [vg_ref[h] for h in pair]
        s2 = _nt(jnp.concatenate(q16, axis=1), blockdiag(*k16))
        m2 = (s2 * jnp.concatenate([wret_ref[h] for h in pair], axis=1)).astype(bf16)
        y2 = _dot(m2, blockdiag(*v16))
        for n, h in enumerate(pair):
            qf = q16[n].astype(f32)
            lc = jnp.concatenate([(qf * ef_ref[h]).astype(bf16), (qf * eb_ref[h]).astype(bf16)], axis=1)
            hcat = jnp.concatenate([hr[h].astype(bf16), gret_ref[0, h]], axis=1)
            y = y2[:, n * HD:(n + 1) * HD] + _nt(lc, hcat)
            mu = jnp.mean(y, axis=-1, keepdims=True)
            d = y - mu
            yn = d * lax.rsqrt(jnp.mean(d * d, axis=-1, keepdims=True) + EPS)
            gg = vg_ref[RH + h].astype(f32)
            yr_ref[:, h * HD:(h + 1) * HD] = ((yn * rnw_ref[h]) * _silu(gg)).astype(yr_ref.dtype)
            _ret_state_update(hr, h, v16[n], k16[n], wf_ref[h], rdec_ref[h][0:1, :])


def _fwd_sweep(L, qk, vg, zx, xc, bcc, dt, gret, gssd, tabs, bias, arow, dexp, snw, rnw, h0r, h0s):
    nc = L // T
    wret, ef, eb, wf, _, rdec = tabs

    def blk(lead, n, width):
        return pl.BlockSpec((n, T, width), lambda s: (lead, s, 0))

    def whole(a):
        nd = a.ndim
        return pl.BlockSpec(a.shape, lambda s: (0,) * nd)

    dsel, esel = _selector_constants()
    small = [wret, ef, eb, wf, rdec, bias, arow, dexp, snw, rnw, h0r, h0s, dsel, esel]
    in_specs = [blk(0, 2 * RH, HD), blk(0, 2 * RH, HD), blk(0, NG, GW), blk(0, NG, GW), blk(0, NG, 2 * NS),
                pl.BlockSpec((T, 2 * SH), lambda s: (s, 0)),
                pl.BlockSpec((1, RH, HD, HD), lambda s: (s, 0, 0, 0)),
                pl.BlockSpec((1, NG, GW, NS), lambda s: (s, 0, 0, 0))] + [whole(a) for a in small]
    return pl.pallas_call(
        _fwd_kernel,
        grid=(nc,),
        in_specs=in_specs,
        out_specs=[pl.BlockSpec((T, D), lambda s: (s, 0)), pl.BlockSpec((T, SI), lambda s: (s, 0))],
        out_shape=[jax.ShapeDtypeStruct((L, D), bf16), jax.ShapeDtypeStruct((L, SI), bf16)],
        scratch_shapes=[pltpu.VMEM((RH, HD, HD), f32), pltpu.VMEM((NG, GW, NS), f32), pltpu.VMEM((NG, T, GW), f32),
                        pltpu.VMEM((3, T, 2 * SH), f32), pltpu.VMEM((3, T, 2 * SH), f32),
                        pltpu.VMEM((3, 2 * SH, T), f32)]
        + [pltpu.VMEM((2 * SH, T), f32)] * 2,
        compiler_params=_cp(("arbitrary",)),
        name="fwd_sweep",
    )(qk, vg, zx, xc, bcc, dt, gret, gssd, *small)


def _branch_out_kernel(yr_ref, ys_ref, wr_ref, ws_ref, gr_ref, gs_ref, o_ref):
    for nb in range(o_ref.shape[1] // MXU_N):
        cols = slice(nb * MXU_N, (nb + 1) * MXU_N)
        acc_r = _dot(yr_ref[...], wr_ref[:, cols])
        acc_s = _dot(ys_ref[...], ws_ref[:, cols])
        m = _sigmoid(gr_ref[:, cols].astype(f32)) * acc_r + _sigmoid(gs_ref[:, cols].astype(f32)) * acc_s
        o_ref[:, cols] = m.astype(o_ref.dtype)


def _branch_out(yr, ys, wr, ws, gates):
    L = yr.shape[0]
    tm, tn = min(L, 512), 512
    nj = D // tn
    return pl.pallas_call(
        _branch_out_kernel,
        grid=(L // tm, nj),
        in_specs=[pl.BlockSpec((tm, D), lambda i, j: (i, 0)),
                  pl.BlockSpec((tm, SI), lambda i, j: (i, 0)),
                  pl.BlockSpec((D, tn), lambda i, j: (0, j)),
                  pl.BlockSpec((SI, tn), lambda i, j: (0, j)),
                  pl.BlockSpec((tm, tn), lambda i, j: (i, j)),
                  pl.BlockSpec((tm, tn), lambda i, j: (i, nj + j))],
        out_specs=pl.BlockSpec((tm, tn), lambda i, j: (i, j)),
        out_shape=jax.ShapeDtypeStruct((L, D), bf16),
        compiler_params=_cp(("parallel", "parallel")),
        name="branch_out",
    )(yr, ys, wr, ws, gates, gates)


def _resid_kernel(m_ref, w_ref, x_ref, g_ref, nw_ref, sc_ref, sh_ref, h_ref, f_ref):
    h = x_ref[...] + g_ref[...] * _dot(m_ref[...], w_ref[...])
    h_ref[...] = h
    f_ref[...] = _rms_mod(h, nw_ref[...], sc_ref[...], sh_ref[...]).astype(f_ref.dtype)


def _mix_residual(m, w, x, gate, nw, sc, sh):
    L = m.shape[0]
    tm = min(L, 512)
    vec = pl.BlockSpec((1, D), lambda i: (0, 0))
    row = pl.BlockSpec((tm, D), lambda i: (i, 0))
    return pl.pallas_call(
        _resid_kernel,
        grid=(L // tm,),
        in_specs=[row, pl.BlockSpec((D, D), lambda i: (0, 0)), row, vec, vec, vec, vec],
        out_specs=[row, row],
        out_shape=[jax.ShapeDtypeStruct((L, D), f32), jax.ShapeDtypeStruct((L, D), bf16)],
        compiler_params=_cp(("parallel",)),
        name="mix_residual",
    )(m, w, x, gate, nw, sc, sh)


def _mlp1_kernel(f_ref, w_ref, o_ref, *, tn):
    for nb in range(tn // MXU_N):
        a = jnp.maximum(_dot(f_ref[...], w_ref[:, nb * MXU_N:(nb + 1) * MXU_N]), 0.0)
        o_ref[:, nb * MXU_N:(nb + 1) * MXU_N] = (a * a).astype(o_ref.dtype)


def _mlp1(f, w):
    L = f.shape[0]
    tm, tn = min(L, 1024), 1024
    return pl.pallas_call(
        functools.partial(_mlp1_kernel, tn=tn),
        grid=(L // tm, DFF // tn),
        in_specs=[pl.BlockSpec((tm, D), lambda i, j: (i, 0)), pl.BlockSpec((D, tn), lambda i, j: (0, j))],
        out_specs=pl.BlockSpec((tm, tn), lambda i, j: (i, j)),
        out_shape=jax.ShapeDtypeStruct((L, DFF), bf16),
        compiler_params=_cp(("parallel", "parallel")),
        name="mlp1",
    )(f, w)


def _mlp2_kernel(a_ref, w_ref, h_ref, g_ref, fw_ref, o_ref, acc, *, nk):
    kk = pl.program_id(1)

    @pl.when(kk == 0)
    def _():
        acc[...] = jnp.zeros_like(acc)

    acc[...] += _dot(a_ref[...], w_ref[...])

    @pl.when(kk == nk - 1)
    def _():
        h2 = h_ref[...] + g_ref[...] * acc[...]
        y = h2 * lax.rsqrt(jnp.mean(h2 * h2, axis=-1, keepdims=True) + EPS)
        o_ref[...] = y * fw_ref[...]


def _mlp2_final(a, w, h, gate, fw):
    L = a.shape[0]
    tm, tk = min(L, 512), 2048
    nk = DFF // tk
    vec = pl.BlockSpec((1, D), lambda i, k: (0, 0))
    return pl.pallas_call(
        functools.partial(_mlp2_kernel, nk=nk),
        grid=(L // tm, nk),
        in_specs=[pl.BlockSpec((tm, tk), lambda i, k: (i, k)),
                  pl.BlockSpec((tk, D), lambda i, k: (k, 0)),
                  pl.BlockSpec((tm, D), lambda i, k: (i, 0)), vec, vec],
        out_specs=pl.BlockSpec((tm, D), lambda i, k: (i, 0)),
        out_shape=jax.ShapeDtypeStruct((L, D), f32),
        scratch_shapes=[pltpu.VMEM((tm, D), f32)],
        compiler_params=_cp(("parallel", "arbitrary")),
        name="mlp2_final",
    )(a, w, h, gate, fw)


def _split_conv(conv_w, conv_b):
    def grp(w, b, width):
        return (jnp.transpose(w.reshape(KC, NG, width), (1, 0, 2)), b.reshape(NG, 1, width))
    cwx, cbx = grp(conv_w[:, :SI], conv_b[:SI], GW)
    cwb, cbb = grp(conv_w[:, SI:SI + NG * NS], conv_b[SI:SI + NG * NS], NS)
    cwc, cbc = grp(conv_w[:, SI + NG * NS:], conv_b[SI + NG * NS:], NS)
    return cwx, cbx, cwb, cbb, cwc, cbc


def kernel(x, c, ctx, c_ctx, w_mod, b_mod, norm1_w, w_in, conv_w, conv_b, ret_decay_logit, ret_norm_w,
           ssd_a_log, ssd_dt_bias, ssd_d, ssd_norm_w, w_ret_out, w_ssd_out, w_o, norm2_w, w_mlp1, w_mlp2,
           final_norm_w):
    depth = w_mod.shape[0]
    assert depth == 1 and x.shape[0] == 1 and x.shape[2] == D
    L = x.shape[1]
    Lc = ctx.shape[1]
    assert L % T == 0 and Lc % T == 0 and L % GRID_W == 0
    xl = x[0]
    xcx = ctx[0]
    ly = 0

    a8 = jnp.zeros((8, D), f32).at[0].set(c[0]).at[1].set(c_ctx)
    mod = _modulation(a8, w_mod[ly], b_mod[ly][None, :])
    sh_a, sc_a, g_a, sh_f, sc_f, g_f = [mod[0:1, i * D:(i + 1) * D] for i in range(6)]
    csh_a, csc_a = mod[1:2, 0:D], mod[1:2, D:2 * D]

    wi = w_in[ly]
    o_q, o_k, o_v, o_z, o_x = 0, D, 2 * D, 4 * D, 4 * D + SI
    o_b = o_x + SI
    o_dt = o_b + 2 * NG * NS
    o_gate = o_dt + 2 * SH
    qd = HD // 4
    w_qk = wi[:, o_q:o_v].reshape(D, 2 * RH, 2, 2, qd).transpose(0, 1, 3, 2, 4).reshape(D, 2 * D)
    wb = jnp.concatenate([w_qk, wi[:, o_v:o_gate]], axis=1).astype(bf16)
    w_gate = wi[:, o_gate:].astype(bf16)

    def proj(src, col, n, **kw):
        return _project(src, wb, col0=col, ncols=n, **kw)

    convw = _split_conv(conv_w[ly], conv_b[ly])
    bias = ssd_dt_bias[ly].reshape(1, 2 * SH)
    arow = (-jnp.exp(ssd_a_log[ly])).reshape(1, 2 * SH)
    dexp = jnp.repeat(ssd_d[ly], P).reshape(NG, 1, GW)
    snw = ssd_norm_w[ly].reshape(NG, 1, GW)
    rnw = ret_norm_w[ly].reshape(RH, 1, HD)
    nw1 = norm1_w[ly][None, :]

    tabs = _ret_tables(ret_decay_logit[ly])
    _, _, _, wf_t, wb_t, rdec = tabs
    zr = jnp.zeros((RH, HD, HD), f32)
    zs = jnp.zeros((NG, GW, NS), f32)

    uc = _norm_mod(xcx, nw1, csc_a, csh_a)
    ident = (jnp.ones((Lc, LANES), f32), jnp.zeros((Lc, LANES), f32))
    kc = proj(uc, o_k, D, bw=HD, tn=512, rope=ident, scale_from=0, name="ctx_k")
    vc = proj(uc, o_v, D, bw=HD, tn=512, name="ctx_v")
    bcx = proj(uc, o_b, NG * NS, bw=NS, tn=512, name="ctx_b")
    xsc = proj(uc, o_x, SI, bw=GW, tn=GW, name="ctx_x")
    dtc = proj(uc, o_dt, 2 * SH, bw=0, tn=2 * SH, out_dtype=f32, name="ctx_dt")
    common = dict(k=kc, k_blk=0, v=vc, v_blk=0, xs=xsc, xs_blk=0, bsrc=bcx, b_blk=0, c_blk=0, dt=dtc,
                  convw=convw, bias=bias, arow=arow, rdec=rdec, h0r=zr, h0s=zs)
    cfr, cfs = _state_sweep(Lc, fwd=True, emit=False, write_conv=False, rw=wf_t, **common)
    cbr, cbs = _state_sweep(Lc, fwd=False, emit=False, write_conv=False, rw=wb_t, **common)

    u = _norm_mod(xl, nw1, sc_a, sh_a)
    rope = _rope_tables(L)
    qk = proj(u, o_q, 2 * D, bw=HD, tn=1024, rope=rope, scale_from=D // 1024, name="lat_qk")
    vg = proj(u, o_v, 2 * D, bw=HD, tn=1024, name="lat_vg")
    zx = proj(u, o_z, 2 * SI, bw=GW, tn=1024, name="lat_zx")
    bcl = proj(u, o_b, 2 * NG * NS, bw=NS, tn=1024, name="lat_bc")
    dtl = proj(u, o_dt, 2 * SH, bw=0, tn=2 * SH, out_dtype=f32, name="lat_dt")
    gates = _project(u, w_gate, col0=0, ncols=2 * D, bw=0, tn=1024, name="lat_gates")

    gret, gssd, xcv, bccv, _, _ = _state_sweep(
        L, fwd=False, emit=True, write_conv=True, k=qk, k_blk=1, v=vg, v_blk=0, xs=zx, xs_blk=1, bsrc=bcl,
        b_blk=0, c_blk=1, dt=dtl, convw=convw, bias=bias, arow=arow, rw=wb_t, rdec=rdec,
        h0r=cbr, h0s=cbs)
    yr, ys = _fwd_sweep(L, qk, vg, zx, xcv, bccv, dtl, gret, gssd, tabs, bias, arow, dexp, snw, rnw, cfr, cfs)

    m = _branch_out(yr, ys, w_ret_out[ly].astype(bf16), w_ssd_out[ly].astype(bf16), gates)
    h1, f = _mix_residual(m, w_o[ly].astype(bf16), xl, g_a, norm2_w[ly][None, :], sc_f, sh_f)
    a = _mlp1(f, w_mlp1[ly].astype(bf16))
    out = _mlp2_final(a, w_mlp2[ly].astype(bf16), h1, g_f, final_norm_w[None, :])
    return out[None]
```

```python
import functools
import math

import numpy as np
import jax
import jax.numpy as jnp
from jax import lax
from jax.experimental import pallas as pl
from jax.experimental.pallas import tpu as pltpu

f32 = jnp.float32
bf16 = jnp.bfloat16

D = 2048
T = 128
GRID_W = 64
HD = 128
RH = D // HD
SI = 2 * D
P = 64
SH = SI // P
NS = 128
NG = 8
R = SH // NG
GW = R * P
KC = 5
DFF = 4 * D
ROPE_BASE = 10000.0
EPS = 1e-6
LANES = 128
MXU_N = 256
CAST_ROWS = 256
HALO = 16
LOG2E = 1.0 / math.log(2.0)
LOG2_FLOOR = -300.0
ONES_LANE = 96
VMEM_LIMIT = 48 * 1024 * 1024


def _cp(sem):
    return pltpu.CompilerParams(dimension_semantics=sem, vmem_limit_bytes=VMEM_LIMIT)


def _nt(a, b):
    return lax.dot_general(a, b, (((1,), (1,)), ((), ())), preferred_element_type=f32)


def _dot(a, b):
    return jnp.dot(a, b, preferred_element_type=f32)


def _sigmoid(x):
    return 0.5 + 0.5 * jnp.tanh(0.5 * x)


def _silu(x):
    h = 0.5 * x
    return h + h * jnp.tanh(h)


def _softplus(x):
    return jnp.maximum(x, 0.0) + jnp.log1p(jnp.exp(-jnp.abs(x)))


def _log_sigmoid(x):
    return jnp.minimum(x, 0.0) - jnp.log1p(jnp.exp(-jnp.abs(x)))


def _split3(x):
    p1 = x.astype(bf16).astype(f32)
    r1 = x - p1
    p2 = r1.astype(bf16).astype(f32)
    p3 = (r1 - p2).astype(bf16).astype(f32)
    return p1, p2, p3


def _cumsum_rows(la, tri):
    a1, a2, a3 = _split3(la)
    return _dot(tri, a1.astype(bf16)) + _dot(tri, a2.astype(bf16)) + _dot(tri, a3.astype(bf16))


def _selector_constants():
    dsel = np.zeros((R, 2 * SH, 2 * T), np.float32)
    esel = np.zeros((2 * SH, 2 * GW), np.float32)
    for r in range(R):
        for k in range(3):
            dsel[r, k * R + r, :T] = 1.0
            dsel[r, SH + k * R + r, T:] = -1.0
            esel[k * R + r, r * P:(r + 1) * P] = 1.0
            esel[SH + k * R + r, GW + r * P:GW + (r + 1) * P] = 1.0
    return jnp.asarray(dsel, bf16), jnp.asarray(esel, bf16)


def _tri_incl():
    ii = lax.broadcasted_iota(jnp.int32, (T, T), 0)
    jj = lax.broadcasted_iota(jnp.int32, (T, T), 1)
    return jnp.where(jj <= ii, 1.0, 0.0).astype(bf16)


def _rope_tab_kernel(cos_ref, sin_ref, *, tm):
    i = pl.program_id(0)
    pos = i * tm + lax.broadcasted_iota(jnp.int32, (tm, LANES), 0)
    lane = lax.broadcasted_iota(jnp.int32, (tm, LANES), 1)
    row = (pos // GRID_W).astype(f32)
    col = (pos % GRID_W).astype(f32)
    fidx = (lane % (HD // 4)).astype(f32)
    freqs = jnp.exp(fidx * (-math.log(ROPE_BASE) / (HD // 4)))
    ang = jnp.where((lane % (HD // 2)) < HD // 4, row, col) * freqs
    s = jnp.sin(ang)
    cos_ref[...] = jnp.cos(ang)
    sin_ref[...] = jnp.where(lane < HD // 2, -s, s)


def _rope_tables(L):
    tm = min(L, 512)
    return pl.pallas_call(
        functools.partial(_rope_tab_kernel, tm=tm),
        grid=(L // tm,),
        out_specs=[pl.BlockSpec((tm, LANES), lambda i: (i, 0))] * 2,
        out_shape=[jax.ShapeDtypeStruct((L, LANES), f32)] * 2,
        compiler_params=_cp(("parallel",)),
        name="rope_tables",
    )()


def _ret_tab_kernel(lg_ref, w_ref, ef_ref, eb_ref, wf_ref, wb_ref, dec_ref):
    lf = _log_sigmoid(lg_ref[0])
    lb = _log_sigmoid(lg_ref[1])
    ii = lax.broadcasted_iota(jnp.int32, (T, T), 0)
    jj = lax.broadcasted_iota(jnp.int32, (T, T), 1)
    dl = (ii - jj).astype(f32)
    w_ref[...] = jnp.exp(jnp.where(jj <= ii, dl * lf, -dl * lb))
    idx = lax.broadcasted_iota(jnp.int32, (T, LANES), 0).astype(f32)
    ef_ref[...] = jnp.exp((idx + 1.0) * lf)
    eb_ref[...] = jnp.exp((T - idx) * lb)
    wf_ref[...] = jnp.exp((T - 1.0 - idx) * lf)
    wb_ref[...] = jnp.exp(idx * lb)
    srow = lax.broadcasted_iota(jnp.int32, (8, LANES), 0)
    dec_ref[...] = jnp.where(srow == 0, jnp.exp(T * lf), jnp.exp(T * lb))


def _ret_tables(logit):
    lg = jnp.broadcast_to(logit[:, :, None, None], (2, RH, 1, LANES))
    tab = pl.BlockSpec((None, T, LANES), lambda h: (h, 0, 0))
    return pl.pallas_call(
        _ret_tab_kernel,
        grid=(RH,),
        in_specs=[pl.BlockSpec((2, None, 1, LANES), lambda h: (0, h, 0, 0))],
        out_specs=[tab, tab, tab, tab, tab, pl.BlockSpec((None, 8, LANES), lambda h: (h, 0, 0))],
        out_shape=[jax.ShapeDtypeStruct((RH, T, T), f32)] + [jax.ShapeDtypeStruct((RH, T, LANES), f32)] * 4
        + [jax.ShapeDtypeStruct((RH, 8, LANES), f32)],
        compiler_params=_cp(("parallel",)),
        name="ret_tables",
    )(lg)


def _mod_kernel(a_ref, w_ref, b_ref, o_ref):
    a = _silu(a_ref[...])
    o_ref[...] = _dot(a, w_ref[...]) + b_ref[...]


def _modulation(a8, w, b):
    tn = 1024
    n = w.shape[1]
    return pl.pallas_call(
        _mod_kernel,
        grid=(n // tn,),
        in_specs=[pl.BlockSpec((8, D), lambda j: (0, 0)),
                  pl.BlockSpec((D, tn), lambda j: (0, j)),
                  pl.BlockSpec((1, tn), lambda j: (0, j))],
        out_specs=pl.BlockSpec((8, tn), lambda j: (0, j)),
        out_shape=jax.ShapeDtypeStruct((8, n), f32),
        compiler_params=_cp(("parallel",)),
        name="modulation",
    )(a8, w, b)


def _rms_mod(x, nw, sc, sh):
    y = x * lax.rsqrt(jnp.mean(x * x, axis=-1, keepdims=True) + EPS)
    return (y * nw) * (1.0 + sc) + sh


def _norm_kernel(x_ref, nw_ref, sc_ref, sh_ref, o_ref):
    o_ref[...] = _rms_mod(x_ref[...], nw_ref[...], sc_ref[...], sh_ref[...]).astype(o_ref.dtype)


def _norm_mod(x, nw, sc, sh):
    L = x.shape[0]
    tm = min(L, 512)
    vec = pl.BlockSpec((1, D), lambda i: (0, 0))
    return pl.pallas_call(
        _norm_kernel,
        grid=(L // tm,),
        in_specs=[pl.BlockSpec((tm, D), lambda i: (i, 0)), vec, vec, vec],
        out_specs=pl.BlockSpec((tm, D), lambda i: (i, 0)),
        out_shape=jax.ShapeDtypeStruct((L, D), bf16),
        compiler_params=_cp(("parallel",)),
        name="norm_mod",
    )(x, nw, sc, sh)


def _proj_kernel(u_ref, w_ref, *rest, bw, tn, rope, scale_from, w32):
    if w32:
        *rest, wb = rest

        @pl.when(pl.program_id(1) == 0)
        def _():
            lane = lax.broadcasted_iota(jnp.int32, (CAST_ROWS, tn), 1) % HD
            for rb in range(D // CAST_ROWS):
                rows = slice(rb * CAST_ROWS, (rb + 1) * CAST_ROWS)
                w = w_ref[rows, :]
                if rope:
                    qd = HD // 4
                    w = jnp.where((lane >= qd) & (lane < 2 * qd), pltpu.roll(w, tn - qd, 1),
                                  jnp.where((lane >= 2 * qd) & (lane < 3 * qd), pltpu.roll(w, qd, 1), w))
                wb[rows, :] = w.astype(bf16)
    else:
        wb = w_ref
    if rope:
        cos_ref, sin_ref, o_ref = rest
        scale = jnp.where(pl.program_id(0) >= scale_from, HD ** -0.5, 1.0).astype(f32)
        cos = cos_ref[...] * scale
        sin = sin_ref[...] * scale
    else:
        (o_ref,) = rest
    sw = min(tn, MXU_N)
    for nb in range(tn // sw):
        acc = _dot(u_ref[...], wb[:, nb * sw:(nb + 1) * sw])
        if rope:
            for hh in range(sw // HD):
                t = acc[:, hh * HD:(hh + 1) * HD]
                o_ref[nb * (sw // HD) + hh] = (t * cos + pltpu.roll(t, HD // 2, 1) * sin).astype(o_ref.dtype)
        elif bw and bw <= sw:
            for hh in range(sw // bw):
                o_ref[nb * (sw // bw) + hh] = acc[:, hh * bw:(hh + 1) * bw].astype(o_ref.dtype)
        elif bw:
            per = bw // sw
            o_ref[nb // per, :, (nb % per) * sw:(nb % per + 1) * sw] = acc.astype(o_ref.dtype)
        else:
            o_ref[:, nb * sw:(nb + 1) * sw] = acc.astype(o_ref.dtype)


def _project(u, w, *, col0, ncols, bw, tn, out_dtype=bf16, rope=None, scale_from=0, name):
    M = u.shape[0]
    N = ncols
    assert col0 % tn == 0 and ncols % tn == 0
    jb = col0 // tn
    w32 = w.dtype == f32
    tm = min(M, 1024 if w32 else 2048)
    if w32:
        w_spec = pl.BlockSpec((None, D, tn), lambda j, i: (0, 0, jb + j))
    else:
        w_spec = pl.BlockSpec((D, tn), lambda j, i: (0, jb + j))
    in_specs = [pl.BlockSpec((tm, D), lambda j, i: (i, 0)), w_spec]
    args = [u, w]
    if rope is not None:
        in_specs += [pl.BlockSpec((tm, LANES), lambda j, i: (i, 0))] * 2
        args += list(rope)
    if bw:
        out_spec = pl.BlockSpec((tn // bw, tm, bw), lambda j, i: (j, i, 0))
        out_shape = jax.ShapeDtypeStruct((N // bw, M, bw), out_dtype)
    else:
        out_spec = pl.BlockSpec((tm, tn), lambda j, i: (i, j))
        out_shape = jax.ShapeDtypeStruct((M, N), out_dtype)
    return pl.pallas_call(
        functools.partial(_proj_kernel, bw=bw, tn=tn, rope=rope is not None, scale_from=scale_from, w32=w32),
        grid=(N // tn, M // tm),
        in_specs=in_specs,
        out_specs=out_spec,
        out_shape=out_shape,
        scratch_shapes=[pltpu.VMEM((D, tn), bf16)] if w32 else [],
        compiler_params=_cp(("parallel", "arbitrary")),
        name=name,
    )(*args)


def _shift_matrix():
    rows = lax.broadcasted_iota(jnp.int32, (4 * T, T + 2 * HALO), 0)
    cols = lax.broadcasted_iota(jnp.int32, (4 * T, T + 2 * HALO), 1)
    blk = rows // T
    tap = jnp.where(blk < KC // 2, blk, blk + 1)
    return jnp.where(cols == (rows - blk * T) + HALO + tap - KC // 2, 1.0, 0.0).astype(bf16)


def _conv_silu(shift, main, prev, nxt, cw, cb, has_prev, has_next):
    zero = jnp.zeros_like(prev)
    ext = jnp.concatenate([jnp.where(has_prev, prev, zero), main, jnp.where(has_next, nxt, zero)], axis=0)
    sh = _dot(shift, ext)
    taps = [sh[0:T], sh[T:2 * T], main.astype(f32), sh[2 * T:3 * T], sh[3 * T:4 * T]]
    acc = cb
    for j in range(KC):
        acc = acc + taps[j] * cw[j:j + 1, :]
    return _silu(acc)


def _decay_prologue(dt_ref, bias_ref, arow_ref):
    dt = _softplus(dt_ref[...] + bias_ref[...])
    la = dt * arow_ref[...]
    acs = _cumsum_rows(la, _tri_incl())
    tot = acs[T - 1:T, :]
    return dt, la, acs, tot


def _transpose_blocks(x):
    n = x.shape[1] // LANES
    return jnp.concatenate([x[:, b * LANES:(b + 1) * LANES].T for b in range(n)], axis=0)


def _ssd_state_update(hs_ref, g, xc, bc16, w_rows, dec_rows):
    xT = _transpose_blocks(xc)
    lhs = jnp.concatenate(
        [(xT[r * P:(r + 1) * P, :] * w_rows[r:r + 1, :]).astype(bf16) for r in range(R)], axis=0)
    upd = _dot(lhs, bc16)
    old = hs_ref[g]
    hs_ref[g] = jnp.concatenate(
        [old[r * P:(r + 1) * P, :] * dec_rows[r:r + 1, :] + upd[r * P:(r + 1) * P, :] for r in range(R)], axis=0)


def _ret_state_update(hr_ref, h, v16, k16, wcol, dec_row):
    vw = (v16.astype(f32) * wcol).T.astype(bf16)
    hr_ref[h] = hr_ref[h] * dec_row + _dot(vw, k16)


def _state_kernel(*refs, nc, fwd, emit, write_conv):
    it = iter(refs)
    k_ref, v_ref = next(it), next(it)
    xs_ref, xs_p, xs_n = next(it), next(it), next(it)
    b_ref, b_p, b_n = next(it), next(it), next(it)
    if write_conv:
        c_ref, c_p, c_n = next(it), next(it), next(it)
    dt_ref = next(it)
    cwx, cbx, cwb, cbb = next(it), next(it), next(it), next(it)
    if write_conv:
        cwc, cbc = next(it), next(it)
    bias_ref, arow_ref = next(it), next(it)
    rw_ref, rdec_ref = next(it), next(it)
    h0r_ref, h0s_ref = next(it), next(it)
    if emit:
        gr_ref, gs_ref = next(it), next(it)
    if write_conv:
        xo_ref, bco_ref = next(it), next(it)
    hfr_ref, hfs_ref = next(it), next(it)
    hr, hs, wT_s, decT_s = next(it), next(it), next(it), next(it)

    s = pl.program_id(0)
    c = s if fwd else nc - 1 - s
    has_prev = c > 0
    has_next = c < nc - 1

    @pl.when(s == 0)
    def _():
        hr[...] = h0r_ref[...]
        hs[...] = h0s_ref[...]

    if emit:
        gr_ref[0] = hr[...].astype(bf16)
        gs_ref[0] = hs[...].astype(bf16)

    dt, la, acs, tot = _decay_prologue(dt_ref, bias_ref, arow_ref)
    wexp = jnp.exp(tot - acs) if fwd else jnp.exp(acs - la)
    wT_s[...] = (wexp * dt).T
    decT_s[...] = jnp.broadcast_to(jnp.exp(tot), (T, 2 * SH)).T
    off = 0 if fwd else SH

    shift = _shift_matrix()

    for g in range(NG):
        xc16 = _conv_silu(shift, xs_ref[g], xs_p[g], xs_n[g], cwx[g], cbx[g], has_prev, has_next).astype(bf16)
        if write_conv:
            def cat(a, b):
                return jnp.concatenate([a, b], axis=1)
            bcc = _conv_silu(shift, cat(b_ref[g], c_ref[g]), cat(b_p[g], c_p[g]), cat(b_n[g], c_n[g]),
                             cat(cwb[g], cwc[g]), cat(cbb[g], cbc[g]), has_prev, has_next).astype(bf16)
            bc16 = bcc[:, :NS]
            xo_ref[g] = xc16
            bco_ref[g] = bcc
        else:
            bc16 = _conv_silu(shift, b_ref[g], b_p[g], b_n[g], cwb[g], cbb[g], has_prev, has_next).astype(bf16)
        r0 = off + g * R
        _ssd_state_update(hs, g, xc16.astype(f32), bc16, wT_s[r0:r0 + R, :], decT_s[r0:r0 + R, :])

    dsel = 0 if fwd else 1
    for h in range(RH):
        _ret_state_update(hr, h, v_ref[h], k_ref[h], rw_ref[h], rdec_ref[h][dsel:dsel + 1, :])

    @pl.when(s == nc - 1)
    def _():
        hfr_ref[...] = hr[...]
        hfs_ref[...] = hs[...]


def _state_sweep(L, *, fwd, emit, write_conv, k, k_blk, v, v_blk, xs, xs_blk, bsrc, b_blk, c_blk, dt,
                 convw, bias, arow, rw, rdec, h0r, h0s):
    nc = L // T
    tb = T // HALO
    nrb = L // HALO

    def cidx(s):
        return s if fwd else nc - 1 - s

    def main(lead, n, width):
        return pl.BlockSpec((n, T, width), lambda s: (lead, cidx(s), 0))

    def prev(lead, n, width):
        return pl.BlockSpec((n, HALO, width), lambda s: (lead, jnp.maximum(cidx(s) * tb - 1, 0), 0))

    def nxt(lead, n, width):
        return pl.BlockSpec((n, HALO, width), lambda s: (lead, jnp.minimum((cidx(s) + 1) * tb, nrb - 1), 0))

    def whole(a):
        nd = a.ndim
        return pl.BlockSpec(a.shape, lambda s: (0,) * nd)

    cwx, cbx, cwb, cbb, cwc, cbc = convw
    in_specs = [main(k_blk, RH, HD), main(v_blk, RH, HD),
                main(xs_blk, NG, GW), prev(xs_blk, NG, GW), nxt(xs_blk, NG, GW),
                main(b_blk, NG, NS), prev(b_blk, NG, NS), nxt(b_blk, NG, NS)]
    args = [k, v, xs, xs, xs, bsrc, bsrc, bsrc]
    if write_conv:
        in_specs += [main(c_blk, NG, NS), prev(c_blk, NG, NS), nxt(c_blk, NG, NS)]
        args += [bsrc, bsrc, bsrc]
    in_specs += [pl.BlockSpec((T, 2 * SH), lambda s: (cidx(s), 0))]
    args += [dt]
    small = [cwx, cbx, cwb, cbb] + ([cwc, cbc] if write_conv else []) + [bias, arow, rw, rdec, h0r, h0s]
    in_specs += [whole(a) for a in small]
    args += small

    out_specs, out_shape = [], []
    if emit:
        out_specs += [pl.BlockSpec((1, RH, HD, HD), lambda s: (cidx(s), 0, 0, 0)),
                      pl.BlockSpec((1, NG, GW, NS), lambda s: (cidx(s), 0, 0, 0))]
        out_shape += [jax.ShapeDtypeStruct((nc, RH, HD, HD), bf16), jax.ShapeDtypeStruct((nc, NG, GW, NS), bf16)]
    if write_conv:
        out_specs += [pl.BlockSpec((NG, T, GW), lambda s: (0, cidx(s), 0)),
                      pl.BlockSpec((NG, T, 2 * NS), lambda s: (0, cidx(s), 0))]
        out_shape += [jax.ShapeDtypeStruct((NG, L, GW), bf16), jax.ShapeDtypeStruct((NG, L, 2 * NS), bf16)]
    out_specs += [pl.BlockSpec((RH, HD, HD), lambda s: (0, 0, 0)), pl.BlockSpec((NG, GW, NS), lambda s: (0, 0, 0))]
    out_shape += [jax.ShapeDtypeStruct((RH, HD, HD), f32), jax.ShapeDtypeStruct((NG, GW, NS), f32)]

    return pl.pallas_call(
        functools.partial(_state_kernel, nc=nc, fwd=fwd, emit=emit, write_conv=write_conv),
        grid=(nc,),
        in_specs=in_specs,
        out_specs=out_specs,
        out_shape=out_shape,
        scratch_shapes=[pltpu.VMEM((RH, HD, HD), f32), pltpu.VMEM((NG, GW, NS), f32),
                        pltpu.VMEM((2 * SH, T), f32), pltpu.VMEM((2 * SH, T), f32)],
        compiler_params=_cp(("arbitrary",)),
        name="state_sweep_" + ("f" if fwd else "b") + ("_emit" if emit else ""),
    )(*args)


def _fwd_kernel(qk_ref, vg_ref, z_ref, xs_ref, bc_ref, dt_ref, gret_ref, gssd_ref,
                wret_ref, ef_ref, eb_ref, wf_ref, rdec_ref, bias_ref, arow_ref, dexp_ref, snw_ref, rnw_ref,
                h0r_ref, h0s_ref, dsel_ref, esel_ref, yr_ref, ys_ref,
                hr, hs, ypre, ap_s, ep_s, apT_s, wT_s, decT_s):
    s = pl.program_id(0)

    @pl.when(s == 0)
    def _():
        hr[...] = h0r_ref[...]
        hs[...] = h0s_ref[...]

    dt, la, acs, tot = _decay_prologue(dt_ref, bias_ref, arow_ref)
    lane = lax.broadcasted_iota(jnp.int32, (T, 2 * SH), 1)
    is_f = lane < SH
    a1 = jnp.where(is_f, acs, acs - la)
    e1 = jnp.exp(jnp.where(is_f, acs, tot - (acs - la)))
    a2 = a1 * LOG2E
    ldt = jnp.maximum(jnp.log2(dt), LOG2_FLOOR)
    for k, part in enumerate(_split3(a2)):
        ap_s[k] = part
    for k, part in enumerate(_split3(jnp.where(is_f, ldt - a2, ldt + a2))):
        apT_s[k] = part.T
    for k, part in enumerate(_split3(e1)):
        ep_s[k] = part
    wT_s[...] = (jnp.exp(tot - acs) * dt).T
    decT_s[...] = jnp.broadcast_to(jnp.exp(tot), (T, 2 * SH)).T

    ii = lax.broadcasted_iota(jnp.int32, (T, T), 0)
    jj = lax.broadcasted_iota(jnp.int32, (T, T), 1)
    lower = jj <= ii
    part_masks = [((lane >= k * R) & (lane < (k + 1) * R)) | ((lane >= SH + k * R) & (lane < SH + (k + 1) * R))
                  for k in range(3)]
    ones_lanes = (lane >= ONES_LANE) & (lane < ONES_LANE + 3)
    row16 = lax.broadcasted_iota(jnp.int32, (16, T), 0)
    lane_p = lax.broadcasted_iota(jnp.int32, (T, 2 * P), 1)

    def pack(parts_ref, g, with_ones):
        acc = jnp.where(ones_lanes, 1.0, 0.0) if with_ones else jnp.zeros((T, 2 * SH), f32)
        for k in range(3):
            shift = (2 * SH - g * R + k * R) % (2 * SH)
            acc = acc + jnp.where(part_masks[k], pltpu.roll(parts_ref[k], shift, 1), 0.0)
        return acc.astype(bf16)

    def grp(g, ssq):
        b16 = bc_ref[g, :, 0:NS]
        c16 = bc_ref[g, :, NS:2 * NS]
        sc = _nt(c16, b16)
        r0 = g * R
        lhs_a = pack(ap_s, g, True)
        lhs_e = pack(ep_s, g, False)
        cf_rows = [apT_s[k, r0:r0 + R, :] for k in range(3)]
        cb_rows = [apT_s[k, SH + r0:SH + r0 + R, :] for k in range(3)]
        xg = xs_ref[g]
        zg = z_ref[g].astype(f32)
        dg = dexp_ref[g]
        hcat = jnp.concatenate([hs[g].astype(bf16), gssd_ref[0, g]], axis=0)
        ycross = _nt(c16, hcat)
        efb = _dot(lhs_e, esel_ref[...])
        ms = []
        for r in range(R):
            lo = jnp.zeros((16, T), f32)
            up = jnp.zeros((16, T), f32)
            for k in range(3):
                lo = jnp.where(row16 == k, cf_rows[k][r:r + 1, :], lo)
                up = jnp.where(row16 == k, cb_rows[k][r:r + 1, :], up)
            dyn = jnp.concatenate([lo, up], axis=1).astype(bf16)
            rhs = jnp.concatenate([dsel_ref[r, 0:ONES_LANE, :], dyn, dsel_ref[r, ONES_LANE + 16:, :]], axis=0)
            dmat = _dot(lhs_a, rhs)
            wm = jnp.exp2(jnp.where(lower, dmat[:, :T], dmat[:, T:]))
            ms.append((sc * wm).astype(bf16))
        ys_parts = []
        for t in range(R // 2):
            xp = xg[:, t * 2 * P:(t + 1) * 2 * P]
            zero = jnp.zeros_like(xp)
            rhs = jnp.concatenate([jnp.where(lane_p < P, xp, zero), jnp.where(lane_p >= P, xp, zero)], axis=0)
            ys_parts.append(_dot(jnp.concatenate([ms[2 * t], ms[2 * t + 1]], axis=1), rhs))
        yg = jnp.concatenate(ys_parts, axis=1)
        yg = yg + ycross[:, :GW] * efb[:, :GW] + ycross[:, GW:] * efb[:, GW:]
        yg = (yg + xg.astype(f32) * dg) * _silu(zg)
        ypre[g] = yg
        ssq = ssq + jnp.sum(yg * yg, axis=1, keepdims=True)
        _ssd_state_update(hs, g, xg.astype(f32), b16, wT_s[r0:r0 + R, :], decT_s[r0:r0 + R, :])
        return ssq

    ssq = jnp.zeros((T, 1), f32)
    for g in range(NG):
        ssq = grp(g, ssq)
    rs = lax.rsqrt(ssq * (1.0 / SI) + EPS)
    for g in range(NG):
        ys_ref[:, g * GW:(g + 1) * GW] = ((ypre[g] * rs) * snw_ref[g]).astype(ys_ref.dtype)

    zero_h = jnp.zeros((T, HD), bf16)

    def blockdiag(a, b):
        return jnp.concatenate([jnp.concatenate([a, zero_h], axis=1), jnp.concatenate([zero_h, b], axis=1)], axis=0)

    for t in range(RH // 2):
        pair = (2 * t, 2 * t + 1)
        q16 = [qk_ref[h] for h in pair]
        k16 = [qk_ref[RH + h] for h in pair]
        v16 = [vg_ref[h] for h in pair]
        s2 = _nt(jnp.concatenate(q16, axis=1), blockdiag(*k16))
        m2 = (s2 * jnp.concatenate([wret_ref[h] for h in pair], axis=1)).astype(bf16)
        y2 = _dot(m2, blockdiag(*v16))
        for n, h in enumerate(pair):
            qf = q16[n].astype(f32)
            lc = jnp.concatenate([(qf * ef_ref[h]).astype(bf16), (qf * eb_ref[h]).astype(bf16)], axis=1)
            hcat = jnp.concatenate([hr[h].astype(bf16), gret_ref[0, h]], axis=1)
            y = y2[:, n * HD:(n + 1) * HD] + _nt(lc, hcat)
            mu = jnp.mean(y, axis=-1, keepdims=True)
            d = y - mu
            yn = d * lax.rsqrt(jnp.mean(d * d, axis=-1, keepdims=True) + EPS)
            gg = vg_ref[RH + h].astype(f32)
            yr_ref[:, h * HD:(h + 1) * HD] = ((yn * rnw_ref[h]) * _silu(gg)).astype(yr_ref.dtype)
            _ret_state_update(hr, h, v16[n], k16[n], wf_ref[h], rdec_ref[h][0:1, :])


def _fwd_sweep(L, qk, vg, zx, xc, bcc, dt, gret, gssd, tabs, bias, arow, dexp, snw, rnw, h0r, h0s):
    nc = L // T
    wret, ef, eb, wf, _, rdec = tabs

    def blk(lead, n, width):
        return pl.BlockSpec((n, T, width), lambda s: (lead, s, 0))

    def whole(a):
        nd = a.ndim
        return pl.BlockSpec(a.shape, lambda s: (0,) * nd)

    dsel, esel = _selector_constants()
    small = [wret, ef, eb, wf, rdec, bias, arow, dexp, snw, rnw, h0r, h0s, dsel, esel]
    in_specs = [blk(0, 2 * RH, HD), blk(0, 2 * RH, HD), blk(0, NG, GW), blk(0, NG, GW), blk(0, NG, 2 * NS),
                pl.BlockSpec((T, 2 * SH), lambda s: (s, 0)),
                pl.BlockSpec((1, RH, HD, HD), lambda s: (s, 0, 0, 0)),
                pl.BlockSpec((1, NG, GW, NS), lambda s: (s, 0, 0, 0))] + [whole(a) for a in small]
    return pl.pallas_call(
        _fwd_kernel,
        grid=(nc,),
        in_specs=in_specs,
        out_specs=[pl.BlockSpec((T, D), lambda s: (s, 0)), pl.BlockSpec((T, SI), lambda s: (s, 0))],
        out_shape=[jax.ShapeDtypeStruct((L, D), bf16), jax.ShapeDtypeStruct((L, SI), bf16)],
        scratch_shapes=[pltpu.VMEM((RH, HD, HD), f32), pltpu.VMEM((NG, GW, NS), f32), pltpu.VMEM((NG, T, GW), f32),
                        pltpu.VMEM((3, T, 2 * SH), f32), pltpu.VMEM((3, T, 2 * SH), f32),
                        pltpu.VMEM((3, 2 * SH, T), f32)]
        + [pltpu.VMEM((2 * SH, T), f32)] * 2,
        compiler_params=_cp(("arbitrary",)),
        name="fwd_sweep",
    )(qk, vg, zx, xc, bcc, dt, gret, gssd, *small)


def _branch_out_kernel(yr_ref, ys_ref, wr_ref, ws_ref, gr_ref, gs_ref, o_ref):
    for nb in range(o_ref.shape[1] // MXU_N):
        cols = slice(nb * MXU_N, (nb + 1) * MXU_N)
        acc_r = _dot(yr_ref[...], wr_ref[:, cols])
        acc_s = _dot(ys_ref[...], ws_ref[:, cols])
        m = _sigmoid(gr_ref[:, cols].astype(f32)) * acc_r + _sigmoid(gs_ref[:, cols].astype(f32)) * acc_s
        o_ref[:, cols] = m.astype(o_ref.dtype)


def _branch_out(yr, ys, wr, ws, gates):
    L = yr.shape[0]
    tm, tn = min(L, 1024), 256
    nj = D // tn
    return pl.pallas_call(
        _branch_out_kernel,
        grid=(L // tm, nj),
        in_specs=[pl.BlockSpec((tm, D), lambda i, j: (i, 0)),
                  pl.BlockSpec((tm, SI), lambda i, j: (i, 0)),
                  pl.BlockSpec((D, tn), lambda i, j: (0, j)),
                  pl.BlockSpec((SI, tn), lambda i, j: (0, j)),
                  pl.BlockSpec((tm, tn), lambda i, j: (i, j)),
                  pl.BlockSpec((tm, tn), lambda i, j: (i, nj + j))],
        out_specs=pl.BlockSpec((tm, tn), lambda i, j: (i, j)),
        out_shape=jax.ShapeDtypeStruct((L, D), bf16),
        compiler_params=_cp(("parallel", "parallel")),
        name="branch_out",
    )(yr, ys, wr, ws, gates, gates)


def _resid_kernel(m_ref, w_ref, x_ref, g_ref, nw_ref, sc_ref, sh_ref, h_ref, f_ref):
    h = x_ref[...] + g_ref[...] * _dot(m_ref[...], w_ref[...])
    h_ref[...] = h
    f_ref[...] = _rms_mod(h, nw_ref[...], sc_ref[...], sh_ref[...]).astype(f_ref.dtype)


def _mix_residual(m, w, x, gate, nw, sc, sh):
    L = m.shape[0]
    tm = min(L, 512)
    vec = pl.BlockSpec((1, D), lambda i: (0, 0))
    row = pl.BlockSpec((tm, D), lambda i: (i, 0))
    return pl.pallas_call(
        _resid_kernel,
        grid=(L // tm,),
        in_specs=[row, pl.BlockSpec((D, D), lambda i: (0, 0)), row, vec, vec, vec, vec],
        out_specs=[row, row],
        out_shape=[jax.ShapeDtypeStruct((L, D), f32), jax.ShapeDtypeStruct((L, D), bf16)],
        compiler_params=_cp(("parallel",)),
        name="mix_residual",
    )(m, w, x, gate, nw, sc, sh)


def _mlp1_kernel(f_ref, w_ref, o_ref, *, tn):
    for nb in range(tn // MXU_N):
        a = jnp.maximum(_dot(f_ref[...], w_ref[:, nb * MXU_N:(nb + 1) * MXU_N]), 0.0)
        o_ref[:, nb * MXU_N:(nb + 1) * MXU_N] = (a * a).astype(o_ref.dtype)


def _mlp1(f, w):
    L = f.shape[0]
    tm, tn = min(L, 1024), 1024
    return pl.pallas_call(
        functools.partial(_mlp1_kernel, tn=tn),
        grid=(L // tm, DFF // tn),
        in_specs=[pl.BlockSpec((tm, D), lambda i, j: (i, 0)), pl.BlockSpec((D, tn), lambda i, j: (0, j))],
        out_specs=pl.BlockSpec((tm, tn), lambda i, j: (i, j)),
        out_shape=jax.ShapeDtypeStruct((L, DFF), bf16),
        compiler_params=_cp(("parallel", "parallel")),
        name="mlp1",
    )(f, w)


def _mlp2_kernel(a_ref, w_ref, h_ref, g_ref, fw_ref, o_ref, acc, *, nk):
    kk = pl.program_id(1)

    @pl.when(kk == 0)
    def _():
        acc[...] = jnp.zeros_like(acc)

    acc[...] += _dot(a_ref[...], w_ref[...])

    @pl.when(kk == nk - 1)
    def _():
        h2 = h_ref[...] + g_ref[...] * acc[...]
        y = h2 * lax.rsqrt(jnp.mean(h2 * h2, axis=-1, keepdims=True) + EPS)
        o_ref[...] = y * fw_ref[...]


def _mlp2_final(a, w, h, gate, fw):
    L = a.shape[0]
    tm, tk = min(L, 512), 2048
    nk = DFF // tk
    vec = pl.BlockSpec((1, D), lambda i, k: (0, 0))
    return pl.pallas_call(
        functools.partial(_mlp2_kernel, nk=nk),
        grid=(L // tm, nk),
        in_specs=[pl.BlockSpec((tm, tk), lambda i, k: (i, k)),
                  pl.BlockSpec((tk, D), lambda i, k: (k, 0)),
                  pl.BlockSpec((tm, D), lambda i, k: (i, 0)), vec, vec],
        out_specs=pl.BlockSpec((tm, D), lambda i, k: (i, 0)),
        out_shape=jax.ShapeDtypeStruct((L, D), f32),
        scratch_shapes=[pltpu.VMEM((tm, D), f32)],
        compiler_params=_cp(("parallel", "arbitrary")),
        name="mlp2_final",
    )(a, w, h, gate, fw)


def _split_conv(conv_w, conv_b):
    def grp(w, b, width):
        return (jnp.transpose(w.reshape(KC, NG, width), (1, 0, 2)), b.reshape(NG, 1, width))
    cwx, cbx = grp(conv_w[:, :SI], conv_b[:SI], GW)
    cwb, cbb = grp(conv_w[:, SI:SI + NG * NS], conv_b[SI:SI + NG * NS], NS)
    cwc, cbc = grp(conv_w[:, SI + NG * NS:], conv_b[SI + NG * NS:], NS)
    return cwx, cbx, cwb, cbb, cwc, cbc


def kernel(x, c, ctx, c_ctx, w_mod, b_mod, norm1_w, w_in, conv_w, conv_b, ret_decay_logit, ret_norm_w,
           ssd_a_log, ssd_dt_bias, ssd_d, ssd_norm_w, w_ret_out, w_ssd_out, w_o, norm2_w, w_mlp1, w_mlp2,
           final_norm_w):
    depth = w_mod.shape[0]
    assert depth == 1 and x.shape[0] == 1 and x.shape[2] == D
    L = x.shape[1]
    Lc = ctx.shape[1]
    assert L % T == 0 and Lc % T == 0 and L % GRID_W == 0
    xl = x[0]
    xcx = ctx[0]
    ly = 0

    a8 = jnp.zeros((8, D), f32).at[0].set(c[0]).at[1].set(c_ctx)
    mod = _modulation(a8, w_mod[ly], b_mod[ly][None, :])
    sh_a, sc_a, g_a, sh_f, sc_f, g_f = [mod[0:1, i * D:(i + 1) * D] for i in range(6)]
    csh_a, csc_a = mod[1:2, 0:D], mod[1:2, D:2 * D]

    wi = w_in[ly]
    o_q, o_k, o_v, o_z, o_x = 0, D, 2 * D, 4 * D, 4 * D + SI
    o_b = o_x + SI
    o_dt = o_b + 2 * NG * NS
    o_gate = o_dt + 2 * SH
    w_gate = wi[:, o_gate:].astype(bf16)

    def proj(src, col, n, **kw):
        return _project(src, w_in, col0=col, ncols=n, **kw)

    convw = _split_conv(conv_w[ly], conv_b[ly])
    bias = ssd_dt_bias[ly].reshape(1, 2 * SH)
    arow = (-jnp.exp(ssd_a_log[ly])).reshape(1, 2 * SH)
    dexp = jnp.repeat(ssd_d[ly], P).reshape(NG, 1, GW)
    snw = ssd_norm_w[ly].reshape(NG, 1, GW)
    rnw = ret_norm_w[ly].reshape(RH, 1, HD)
    nw1 = norm1_w[ly][None, :]

    tabs = _ret_tables(ret_decay_logit[ly])
    _, _, _, wf_t, wb_t, rdec = tabs
    zr = jnp.zeros((RH, HD, HD), f32)
    zs = jnp.zeros((NG, GW, NS), f32)

    uc = _norm_mod(xcx, nw1, csc_a, csh_a)
    ident = (jnp.ones((Lc, LANES), f32), jnp.zeros((Lc, LANES), f32))
    kc = proj(uc, o_k, D, bw=HD, tn=512, rope=ident, scale_from=0, name="ctx_k")
    vc = proj(uc, o_v, D, bw=HD, tn=512, name="ctx_v")
    bcx = proj(uc, o_b, NG * NS, bw=NS, tn=512, name="ctx_b")
    xsc = proj(uc, o_x, SI, bw=GW, tn=GW, name="ctx_x")
    dtc = proj(uc, o_dt, 2 * SH, bw=0, tn=2 * SH, out_dtype=f32, name="ctx_dt")
    common = dict(k=kc, k_blk=0, v=vc, v_blk=0, xs=xsc, xs_blk=0, bsrc=bcx, b_blk=0, c_blk=0, dt=dtc,
                  convw=convw, bias=bias, arow=arow, rdec=rdec, h0r=zr, h0s=zs)
    cfr, cfs = _state_sweep(Lc, fwd=True, emit=False, write_conv=False, rw=wf_t, **common)
    cbr, cbs = _state_sweep(Lc, fwd=False, emit=False, write_conv=False, rw=wb_t, **common)

    u = _norm_mod(xl, nw1, sc_a, sh_a)
    rope = _rope_tables(L)
    qk = proj(u, o_q, 2 * D, bw=HD, tn=1024, rope=rope, scale_from=D // 1024, name="lat_qk")
    vg = proj(u, o_v, 2 * D, bw=HD, tn=1024, name="lat_vg")
    zx = proj(u, o_z, 2 * SI, bw=GW, tn=1024, name="lat_zx")
    bcl = proj(u, o_b, 2 * NG * NS, bw=NS, tn=1024, name="lat_bc")
    dtl = proj(u, o_dt, 2 * SH, bw=0, tn=2 * SH, out_dtype=f32, name="lat_dt")
    gates = _project(u, w_gate, col0=0, ncols=2 * D, bw=0, tn=1024, name="lat_gates")

    gret, gssd, xcv, bccv, _, _ = _state_sweep(
        L, fwd=False, emit=True, write_conv=True, k=qk, k_blk=1, v=vg, v_blk=0, xs=zx, xs_blk=1, bsrc=bcl,
        b_blk=0, c_blk=1, dt=dtl, convw=convw, bias=bias, arow=arow, rw=wb_t, rdec=rdec,
        h0r=cbr, h0s=cbs)
    yr, ys = _fwd_sweep(L, qk, vg, zx, xcv, bccv, dtl, gret, gssd, tabs, bias, arow, dexp, snw, rnw, cfr, cfs)

    m = _branch_out(yr, ys, w_ret_out[ly].astype(bf16), w_ssd_out[ly].astype(bf16), gates)
    h1, f = _mix_residual(m, w_o[ly].astype(bf16), xl, g_a, norm2_w[ly][None, :], sc_f, sh_f)
    a = _mlp1(f, w_mlp1[ly].astype(bf16))
    out = _mlp2_final(a, w_mlp2[ly].astype(bf16), h1, g_f, final_norm_w[None, :])
    return out[None]
```

```python
import functools
import math

import numpy as np
import jax
import jax.numpy as jnp
from jax import lax
from jax.experimental import pallas as pl
from jax.experimental.pallas import tpu as pltpu

f32 = jnp.float32
bf16 = jnp.bfloat16

D = 2048
T = 128
GRID_W = 64
HD = 128
RH = D // HD
SI = 2 * D
P = 64
SH = SI // P
NS = 128
NG = 8
R = SH // NG
GW = R * P
KC = 5
DFF = 4 * D
ROPE_BASE = 10000.0
EPS = 1e-6
LANES = 128
MXU_N = 256
CAST_ROWS = 256
HALO = 16
LOG2E = 1.0 / math.log(2.0)
LOG2_FLOOR = -300.0
ONES_LANE = 96
VMEM_LIMIT = 48 * 1024 * 1024


def _cp(sem):
    return pltpu.CompilerParams(dimension_semantics=sem, vmem_limit_bytes=VMEM_LIMIT)


def _nt(a, b):
    return lax.dot_general(a, b, (((1,), (1,)), ((), ())), preferred_element_type=f32)


def _dot(a, b):
    return jnp.dot(a, b, preferred_element_type=f32)


def _sigmoid(x):
    return 0.5 + 0.5 * jnp.tanh(0.5 * x)


def _silu(x):
    h = 0.5 * x
    return h + h * jnp.tanh(h)


def _softplus(x):
    return jnp.maximum(x, 0.0) + jnp.log1p(jnp.exp(-jnp.abs(x)))


def _log_sigmoid(x):
    return jnp.minimum(x, 0.0) - jnp.log1p(jnp.exp(-jnp.abs(x)))


def _split3(x):
    p1 = x.astype(bf16).astype(f32)
    r1 = x - p1
    p2 = r1.astype(bf16).astype(f32)
    p3 = (r1 - p2).astype(bf16).astype(f32)
    return p1, p2, p3


def _cumsum_rows(la, tri):
    a1, a2, a3 = _split3(la)
    return _dot(tri, a1.astype(bf16)) + _dot(tri, a2.astype(bf16)) + _dot(tri, a3.astype(bf16))


def _selector_constants():
    dsel = np.zeros((R, 2 * SH, 2 * T), np.float32)
    esel = np.zeros((2 * SH, 2 * GW), np.float32)
    for r in range(R):
        for k in range(3):
            dsel[r, k * R + r, :T] = 1.0
            dsel[r, SH + k * R + r, T:] = -1.0
            esel[k * R + r, r * P:(r + 1) * P] = 1.0
            esel[SH + k * R + r, GW + r * P:GW + (r + 1) * P] = 1.0
    return jnp.asarray(dsel, bf16), jnp.asarray(esel, bf16)


def _tri_incl():
    ii = lax.broadcasted_iota(jnp.int32, (T, T), 0)
    jj = lax.broadcasted_iota(jnp.int32, (T, T), 1)
    return jnp.where(jj <= ii, 1.0, 0.0).astype(bf16)


def _rope_tab_kernel(rc_ref, rs_ref, cc_ref, cs_ref):
    def tab(n):
        idx = lax.broadcasted_iota(jnp.int32, (n, LANES), 0).astype(f32)
        lane = lax.broadcasted_iota(jnp.int32, (n, LANES), 1)
        freqs = jnp.exp((lane % (HD // 4)).astype(f32) * (-math.log(ROPE_BASE) / (HD // 4)))
        ang = idx * freqs
        s = jnp.sin(ang)
        return jnp.cos(ang), jnp.where(lane < HD // 2, -s, s)

    rc_ref[...], rs_ref[...] = tab(rc_ref.shape[0])
    cc_ref[...], cs_ref[...] = tab(GRID_W)


def _rope_tables(L):
    rows = L // GRID_W
    shapes = [jax.ShapeDtypeStruct((rows, LANES), f32)] * 2 + [jax.ShapeDtypeStruct((GRID_W, LANES), f32)] * 2
    return pl.pallas_call(_rope_tab_kernel, out_shape=shapes, name="rope_tables")()


def _rope_rows(row_tab, col_tab, tm):
    lane = lax.broadcasted_iota(jnp.int32, (GRID_W, LANES), 1)
    use_row = (lane % (HD // 2)) < HD // 4
    return jnp.concatenate([jnp.where(use_row, row_tab[r:r + 1, :], col_tab) for r in range(tm // GRID_W)], axis=0)


def _ret_tab_kernel(lg_ref, w_ref, ef_ref, eb_ref, wf_ref, wb_ref, dec_ref):
    lf = _log_sigmoid(lg_ref[0])
    lb = _log_sigmoid(lg_ref[1])
    ii = lax.broadcasted_iota(jnp.int32, (T, T), 0)
    jj = lax.broadcasted_iota(jnp.int32, (T, T), 1)
    dl = (ii - jj).astype(f32)
    w_ref[...] = jnp.exp(jnp.where(jj <= ii, dl * lf, -dl * lb))
    idx = lax.broadcasted_iota(jnp.int32, (T, LANES), 0).astype(f32)
    ef_ref[...] = jnp.exp((idx + 1.0) * lf)
    eb_ref[...] = jnp.exp((T - idx) * lb)
    wf_ref[...] = jnp.exp((T - 1.0 - idx) * lf)
    wb_ref[...] = jnp.exp(idx * lb)
    srow = lax.broadcasted_iota(jnp.int32, (8, LANES), 0)
    dec_ref[...] = jnp.where(srow == 0, jnp.exp(T * lf), jnp.exp(T * lb))


def _ret_tables(logit):
    lg = jnp.broadcast_to(logit[:, :, None, None], (2, RH, 1, LANES))
    tab = pl.BlockSpec((None, T, LANES), lambda h: (h, 0, 0))
    return pl.pallas_call(
        _ret_tab_kernel,
        grid=(RH,),
        in_specs=[pl.BlockSpec((2, None, 1, LANES), lambda h: (0, h, 0, 0))],
        out_specs=[tab, tab, tab, tab, tab, pl.BlockSpec((None, 8, LANES), lambda h: (h, 0, 0))],
        out_shape=[jax.ShapeDtypeStruct((RH, T, T), f32)] + [jax.ShapeDtypeStruct((RH, T, LANES), f32)] * 4
        + [jax.ShapeDtypeStruct((RH, 8, LANES), f32)],
        compiler_params=_cp(("parallel",)),
        name="ret_tables",
    )(lg)


def _mod_kernel(a_ref, w_ref, b_ref, o_ref):
    a = _silu(a_ref[...])
    o_ref[...] = _dot(a, w_ref[...]) + b_ref[...]


def _modulation(a8, w, b):
    tn = 1024
    n = w.shape[1]
    return pl.pallas_call(
        _mod_kernel,
        grid=(n // tn,),
        in_specs=[pl.BlockSpec((8, D), lambda j: (0, 0)),
                  pl.BlockSpec((D, tn), lambda j: (0, j)),
                  pl.BlockSpec((1, tn), lambda j: (0, j))],
        out_specs=pl.BlockSpec((8, tn), lambda j: (0, j)),
        out_shape=jax.ShapeDtypeStruct((8, n), f32),
        compiler_params=_cp(("parallel",)),
        name="modulation",
    )(a8, w, b)


def _rms_mod(x, nw, sc, sh):
    y = x * lax.rsqrt(jnp.mean(x * x, axis=-1, keepdims=True) + EPS)
    return (y * nw) * (1.0 + sc) + sh


def _norm_kernel(x_ref, nw_ref, sc_ref, sh_ref, o_ref):
    o_ref[...] = _rms_mod(x_ref[...], nw_ref[...], sc_ref[...], sh_ref[...]).astype(o_ref.dtype)


def _norm_mod(x, nw, sc, sh):
    L = x.shape[0]
    tm = min(L, 512)
    vec = pl.BlockSpec((1, D), lambda i: (0, 0))
    return pl.pallas_call(
        _norm_kernel,
        grid=(L // tm,),
        in_specs=[pl.BlockSpec((tm, D), lambda i: (i, 0)), vec, vec, vec],
        out_specs=pl.BlockSpec((tm, D), lambda i: (i, 0)),
        out_shape=jax.ShapeDtypeStruct((L, D), bf16),
        compiler_params=_cp(("parallel",)),
        name="norm_mod",
    )(x, nw, sc, sh)


def _proj_kernel(u_ref, w_ref, *rest, bw, tn, rope, scale_from, w32, relu2):
    if w32:
        *rest, wb = rest
        if len(w_ref.shape) == 3:
            w_ref = w_ref.at[0]

        @pl.when(pl.program_id(1) == 0)
        def _():
            lane = lax.broadcasted_iota(jnp.int32, (CAST_ROWS, tn), 1) % HD
            for rb in range(D // CAST_ROWS):
                rows = slice(rb * CAST_ROWS, (rb + 1) * CAST_ROWS)
                w = w_ref[rows, :]
                if rope:
                    qd = HD // 4
                    w = jnp.where((lane >= qd) & (lane < 2 * qd), pltpu.roll(w, tn - qd, 1),
                                  jnp.where((lane >= 2 * qd) & (lane < 3 * qd), pltpu.roll(w, qd, 1), w))
                wb[rows, :] = w.astype(bf16)
    else:
        wb = w_ref
    if rope:
        rc_ref, rs_ref, cc_ref, cs_ref, o_ref = rest
        tm = u_ref.shape[0]
        scale = jnp.where(pl.program_id(0) >= scale_from, HD ** -0.5, 1.0).astype(f32)
        cos = _rope_rows(rc_ref[...], cc_ref[...], tm) * scale
        sin = _rope_rows(rs_ref[...], cs_ref[...], tm) * scale
    else:
        (o_ref,) = rest
    sw = min(tn, MXU_N)
    for nb in range(tn // sw):
        acc = _dot(u_ref[...], wb[:, nb * sw:(nb + 1) * sw])
        if relu2:
            acc = jnp.maximum(acc, 0.0)
            acc = acc * acc
        if rope:
            for hh in range(sw // HD):
                t = acc[:, hh * HD:(hh + 1) * HD]
                o_ref[nb * (sw // HD) + hh] = (t * cos + pltpu.roll(t, HD // 2, 1) * sin).astype(o_ref.dtype)
        elif bw and bw <= sw:
            for hh in range(sw // bw):
                o_ref[nb * (sw // bw) + hh] = acc[:, hh * bw:(hh + 1) * bw].astype(o_ref.dtype)
        elif bw:
            per = bw // sw
            o_ref[nb // per, :, (nb % per) * sw:(nb % per + 1) * sw] = acc.astype(o_ref.dtype)
        else:
            o_ref[:, nb * sw:(nb + 1) * sw] = acc.astype(o_ref.dtype)


def _project(u, w, *, col0, ncols, bw, tn, out_dtype=bf16, rope=None, scale_from=0, relu2=False, name):
    M = u.shape[0]
    N = ncols
    assert col0 % LANES == 0 and ncols % tn == 0
    jb = col0 // tn
    w32 = w.dtype == f32
    tm = min(M, 1024 if w32 else 2048)
    if w32 and col0 % tn:
        w_spec = pl.BlockSpec((pl.Element(1), pl.Element(D), pl.Element(tn)),
                              lambda j, i: (0, 0, pl.multiple_of(col0 + j * tn, LANES)))
    elif w32:
        w_spec = pl.BlockSpec((None, D, tn), lambda j, i: (0, 0, jb + j))
    else:
        assert col0 % tn == 0
        w_spec = pl.BlockSpec((D, tn), lambda j, i: (0, jb + j))
    in_specs = [pl.BlockSpec((tm, D), lambda j, i: (i, 0)), w_spec]
    args = [u, w]
    if rope is not None:
        in_specs += [pl.BlockSpec((tm // GRID_W, LANES), lambda j, i: (i, 0))] * 2
        in_specs += [pl.BlockSpec((GRID_W, LANES), lambda j, i: (0, 0))] * 2
        args += list(rope)
    if bw:
        out_spec = pl.BlockSpec((tn // bw, tm, bw), lambda j, i: (j, i, 0))
        out_shape = jax.ShapeDtypeStruct((N // bw, M, bw), out_dtype)
    else:
        out_spec = pl.BlockSpec((tm, tn), lambda j, i: (i, j))
        out_shape = jax.ShapeDtypeStruct((M, N), out_dtype)
    return pl.pallas_call(
        functools.partial(_proj_kernel, bw=bw, tn=tn, rope=rope is not None, scale_from=scale_from, w32=w32,
                          relu2=relu2),
        grid=(N // tn, M // tm),
        in_specs=in_specs,
        out_specs=out_spec,
        out_shape=out_shape,
        scratch_shapes=[pltpu.VMEM((D, tn), bf16)] if w32 else [],
        compiler_params=_cp(("parallel", "arbitrary")),
        name=name,
    )(*args)


def _shift_matrix():
    rows = lax.broadcasted_iota(jnp.int32, (4 * T, T + 2 * HALO), 0)
    cols = lax.broadcasted_iota(jnp.int32, (4 * T, T + 2 * HALO), 1)
    blk = rows // T
    tap = jnp.where(blk < KC // 2, blk, blk + 1)
    return jnp.where(cols == (rows - blk * T) + HALO + tap - KC // 2, 1.0, 0.0).astype(bf16)


def _conv_silu(shift, main, prev, nxt, cw, cb, has_prev, has_next):
    zero = jnp.zeros_like(prev)
    ext = jnp.concatenate([jnp.where(has_prev, prev, zero), main, jnp.where(has_next, nxt, zero)], axis=0)
    sh = _dot(shift, ext)
    taps = [sh[0:T], sh[T:2 * T], main.astype(f32), sh[2 * T:3 * T], sh[3 * T:4 * T]]
    acc = cb
    for j in range(KC):
        acc = acc + taps[j] * cw[j:j + 1, :]
    return _silu(acc)


def _decay_prologue(dt_ref, bias_ref, arow_ref):
    dt = _softplus(dt_ref[...] + bias_ref[...])
    la = dt * arow_ref[...]
    acs = _cumsum_rows(la, _tri_incl())
    tot = acs[T - 1:T, :]
    return dt, la, acs, tot


def _transpose_blocks(x):
    n = x.shape[1] // LANES
    return jnp.concatenate([x[:, b * LANES:(b + 1) * LANES].T for b in range(n)], axis=0)


def _ssd_state_update(hs_ref, g, xc, bc16, w_rows, dec_rows):
    xT = _transpose_blocks(xc)
    lhs = jnp.concatenate(
        [(xT[r * P:(r + 1) * P, :] * w_rows[r:r + 1, :]).astype(bf16) for r in range(R)], axis=0)
    upd = _dot(lhs, bc16)
    old = hs_ref[g]
    hs_ref[g] = jnp.concatenate(
        [old[r * P:(r + 1) * P, :] * dec_rows[r:r + 1, :] + upd[r * P:(r + 1) * P, :] for r in range(R)], axis=0)


def _ret_state_update(hr_ref, h, v16, k16, wcol, dec_row):
    vw = (v16.astype(f32) * wcol).T.astype(bf16)
    hr_ref[h] = hr_ref[h] * dec_row + _dot(vw, k16)


def _state_kernel(*refs, nc, fwd, emit, write_conv):
    it = iter(refs)
    k_ref, v_ref = next(it), next(it)
    xs_ref, xs_p, xs_n = next(it), next(it), next(it)
    b_ref, b_p, b_n = next(it), next(it), next(it)
    if write_conv:
        c_ref, c_p, c_n = next(it), next(it), next(it)
    dt_ref = next(it)
    cwx, cbx, cwb, cbb = next(it), next(it), next(it), next(it)
    if write_conv:
        cwc, cbc = next(it), next(it)
    bias_ref, arow_ref = next(it), next(it)
    rw_ref, rdec_ref = next(it), next(it)
    h0r_ref, h0s_ref = next(it), next(it)
    if emit:
        gr_ref, gs_ref = next(it), next(it)
    if write_conv:
        xo_ref, bco_ref = next(it), next(it)
    hfr_ref, hfs_ref = next(it), next(it)
    hr, hs, wT_s, decT_s = next(it), next(it), next(it), next(it)

    s = pl.program_id(0)
    c = s if fwd else nc - 1 - s
    has_prev = c > 0
    has_next = c < nc - 1

    @pl.when(s == 0)
    def _():
        hr[...] = h0r_ref[...]
        hs[...] = h0s_ref[...]

    if emit:
        gr_ref[0] = hr[...].astype(bf16)
        gs_ref[0] = hs[...].astype(bf16)

    dt, la, acs, tot = _decay_prologue(dt_ref, bias_ref, arow_ref)
    wexp = jnp.exp(tot - acs) if fwd else jnp.exp(acs - la)
    wT_s[...] = (wexp * dt).T
    decT_s[...] = jnp.broadcast_to(jnp.exp(tot), (T, 2 * SH)).T
    off = 0 if fwd else SH

    shift = _shift_matrix()

    for g in range(NG):
        xc16 = _conv_silu(shift, xs_ref[g], xs_p[g], xs_n[g], cwx[g], cbx[g], has_prev, has_next).astype(bf16)
        if write_conv:
            def cat(a, b):
                return jnp.concatenate([a, b], axis=1)
            bcc = _conv_silu(shift, cat(b_ref[g], c_ref[g]), cat(b_p[g], c_p[g]), cat(b_n[g], c_n[g]),
                             cat(cwb[g], cwc[g]), cat(cbb[g], cbc[g]), has_prev, has_next).astype(bf16)
            bc16 = bcc[:, :NS]
            xo_ref[g] = xc16
            bco_ref[g] = bcc
        else:
            bc16 = _conv_silu(shift, b_ref[g], b_p[g], b_n[g], cwb[g], cbb[g], has_prev, has_next).astype(bf16)
        r0 = off + g * R
        _ssd_state_update(hs, g, xc16.astype(f32), bc16, wT_s[r0:r0 + R, :], decT_s[r0:r0 + R, :])

    dsel = 0 if fwd else 1
    for h in range(RH):
        _ret_state_update(hr, h, v_ref[h], k_ref[h], rw_ref[h], rdec_ref[h][dsel:dsel + 1, :])

    @pl.when(s == nc - 1)
    def _():
        hfr_ref[...] = hr[...]
        hfs_ref[...] = hs[...]


def _state_sweep(L, *, fwd, emit, write_conv, k, k_blk, v, v_blk, xs, xs_blk, bsrc, b_blk, c_blk, dt,
                 convw, bias, arow, rw, rdec, h0r, h0s):
    nc = L // T
    tb = T // HALO
    nrb = L // HALO

    def cidx(s):
        return s if fwd else nc - 1 - s

    def main(lead, n, width):
        return pl.BlockSpec((n, T, width), lambda s: (lead, cidx(s), 0))

    def prev(lead, n, width):
        return pl.BlockSpec((n, HALO, width), lambda s: (lead, jnp.maximum(cidx(s) * tb - 1, 0), 0))

    def nxt(lead, n, width):
        return pl.BlockSpec((n, HALO, width), lambda s: (lead, jnp.minimum((cidx(s) + 1) * tb, nrb - 1), 0))

    def whole(a):
        nd = a.ndim
        return pl.BlockSpec(a.shape, lambda s: (0,) * nd)

    cwx, cbx, cwb, cbb, cwc, cbc = convw
    in_specs = [main(k_blk, RH, HD), main(v_blk, RH, HD),
                main(xs_blk, NG, GW), prev(xs_blk, NG, GW), nxt(xs_blk, NG, GW),
                main(b_blk, NG, NS), prev(b_blk, NG, NS), nxt(b_blk, NG, NS)]
    args = [k, v, xs, xs, xs, bsrc, bsrc, bsrc]
    if write_conv:
        in_specs += [main(c_blk, NG, NS), prev(c_blk, NG, NS), nxt(c_blk, NG, NS)]
        args += [bsrc, bsrc, bsrc]
    in_specs += [pl.BlockSpec((T, 2 * SH), lambda s: (cidx(s), 0))]
    args += [dt]
    small = [cwx, cbx, cwb, cbb] + ([cwc, cbc] if write_conv else []) + [bias, arow, rw, rdec, h0r, h0s]
    in_specs += [whole(a) for a in small]
    args += small

    out_specs, out_shape = [], []
    if emit:
        out_specs += [pl.BlockSpec((1, RH, HD, HD), lambda s: (cidx(s), 0, 0, 0)),
                      pl.BlockSpec((1, NG, GW, NS), lambda s: (cidx(s), 0, 0, 0))]
        out_shape += [jax.ShapeDtypeStruct((nc, RH, HD, HD), bf16), jax.ShapeDtypeStruct((nc, NG, GW, NS), bf16)]
    if write_conv:
        out_specs += [pl.BlockSpec((NG, T, GW), lambda s: (0, cidx(s), 0)),
                      pl.BlockSpec((NG, T, 2 * NS), lambda s: (0, cidx(s), 0))]
        out_shape += [jax.ShapeDtypeStruct((NG, L, GW), bf16), jax.ShapeDtypeStruct((NG, L, 2 * NS), bf16)]
    out_specs += [pl.BlockSpec((RH, HD, HD), lambda s: (0, 0, 0)), pl.BlockSpec((NG, GW, NS), lambda s: (0, 0, 0))]
    out_shape += [jax.ShapeDtypeStruct((RH, HD, HD), f32), jax.ShapeDtypeStruct((NG, GW, NS), f32)]

    return pl.pallas_call(
        functools.partial(_state_kernel, nc=nc, fwd=fwd, emit=emit, write_conv=write_conv),
        grid=(nc,),
        in_specs=in_specs,
        out_specs=out_specs,
        out_shape=out_shape,
        scratch_shapes=[pltpu.VMEM((RH, HD, HD), f32), pltpu.VMEM((NG, GW, NS), f32),
                        pltpu.VMEM((2 * SH, T), f32), pltpu.VMEM((2 * SH, T), f32)],
        compiler_params=_cp(("arbitrary",)),
        name="state_sweep_" + ("f" if fwd else "b") + ("_emit" if emit else ""),
    )(*args)


def _fwd_kernel(qk_ref, vg_ref, z_ref, xs_ref, bc_ref, dt_ref, gret_ref, gssd_ref,
                wret_ref, ef_ref, eb_ref, wf_ref, rdec_ref, bias_ref, arow_ref, dexp_ref, snw_ref, rnw_ref,
                h0r_ref, h0s_ref, dsel_ref, esel_ref, yr_ref, ys_ref,
                hr, hs, ypre, ap_s, ep_s, apT_s, wT_s, decT_s):
    s = pl.program_id(0)

    @pl.when(s == 0)
    def _():
        hr[...] = h0r_ref[...]
        hs[...] = h0s_ref[...]

    dt, la, acs, tot = _decay_prologue(dt_ref, bias_ref, arow_ref)
    lane = lax.broadcasted_iota(jnp.int32, (T, 2 * SH), 1)
    is_f = lane < SH
    a1 = jnp.where(is_f, acs, acs - la)
    e1 = jnp.exp(jnp.where(is_f, acs, tot - (acs - la)))
    a2 = a1 * LOG2E
    ldt = jnp.maximum(jnp.log2(dt), LOG2_FLOOR)
    for k, part in enumerate(_split3(a2)):
        ap_s[k] = part
    for k, part in enumerate(_split3(jnp.where(is_f, ldt - a2, ldt + a2))):
        apT_s[k] = part.T
    for k, part in enumerate(_split3(e1)):
        ep_s[k] = part
    wT_s[...] = (jnp.exp(tot - acs) * dt).T
    decT_s[...] = jnp.broadcast_to(jnp.exp(tot), (T, 2 * SH)).T

    ii = lax.broadcasted_iota(jnp.int32, (T, T), 0)
    jj = lax.broadcasted_iota(jnp.int32, (T, T), 1)
    lower = jj <= ii
    part_masks = [((lane >= k * R) & (lane < (k + 1) * R)) | ((lane >= SH + k * R) & (lane < SH + (k + 1) * R))
                  for k in range(3)]
    ones_lanes = (lane >= ONES_LANE) & (lane < ONES_LANE + 3)
    row16 = lax.broadcasted_iota(jnp.int32, (16, T), 0)
    lane_p = lax.broadcasted_iota(jnp.int32, (T, 2 * P), 1)

    def pack(parts_ref, g, with_ones):
        acc = jnp.where(ones_lanes, 1.0, 0.0) if with_ones else jnp.zeros((T, 2 * SH), f32)
        for k in range(3):
            shift = (2 * SH - g * R + k * R) % (2 * SH)
            acc = acc + jnp.where(part_masks[k], pltpu.roll(parts_ref[k], shift, 1), 0.0)
        return acc.astype(bf16)

    def grp(g, ssq):
        b16 = bc_ref[g, :, 0:NS]
        c16 = bc_ref[g, :, NS:2 * NS]
        sc = _nt(c16, b16)
        r0 = g * R
        lhs_a = pack(ap_s, g, True)
        lhs_e = pack(ep_s, g, False)
        cf_rows = [apT_s[k, r0:r0 + R, :] for k in range(3)]
        cb_rows = [apT_s[k, SH + r0:SH + r0 + R, :] for k in range(3)]
        xg = xs_ref[g]
        zg = z_ref[g].astype(f32)
        dg = dexp_ref[g]
        hcat = jnp.concatenate([hs[g].astype(bf16), gssd_ref[0, g]], axis=0)
        ycross = _nt(c16, hcat)
        efb = _dot(lhs_e, esel_ref[...])
        ms = []
        for r in range(R):
            lo = jnp.zeros((16, T), f32)
            up = jnp.zeros((16, T), f32)
            for k in range(3):
                lo = jnp.where(row16 == k, cf_rows[k][r:r + 1, :], lo)
                up = jnp.where(row16 == k, cb_rows[k][r:r + 1, :], up)
            dyn = jnp.concatenate([lo, up], axis=1).astype(bf16)
            rhs = jnp.concatenate([dsel_ref[r, 0:ONES_LANE, :], dyn, dsel_ref[r, ONES_LANE + 16:, :]], axis=0)
            dmat = _dot(lhs_a, rhs)
            wm = jnp.exp2(jnp.where(lower, dmat[:, :T], dmat[:, T:]))
            ms.append((sc * wm).astype(bf16))
        ys_parts = []
        for t in range(R // 2):
            xp = xg[:, t * 2 * P:(t + 1) * 2 * P]
            zero = jnp.zeros_like(xp)
            rhs = jnp.concatenate([jnp.where(lane_p < P, xp, zero), jnp.where(lane_p >= P, xp, zero)], axis=0)
            ys_parts.append(_dot(jnp.concatenate([ms[2 * t], ms[2 * t + 1]], axis=1), rhs))
        yg = jnp.concatenate(ys_parts, axis=1)
        yg = yg + ycross[:, :GW] * efb[:, :GW] + ycross[:, GW:] * efb[:, GW:]
        yg = (yg + xg.astype(f32) * dg) * _silu(zg)
        ypre[g] = yg
        ssq = ssq + jnp.sum(yg * yg, axis=1, keepdims=True)
        _ssd_state_update(hs, g, xg.astype(f32), b16, wT_s[r0:r0 + R, :], decT_s[r0:r0 + R, :])
        return ssq

    ssq = jnp.zeros((T, 1), f32)
    for g in range(NG):
        ssq = grp(g, ssq)
    rs = lax.rsqrt(ssq * (1.0 / SI) + EPS)
    for g in range(NG):
        ys_ref[:, g * GW:(g + 1) * GW] = ((ypre[g] * rs) * snw_ref[g]).astype(ys_ref.dtype)

    zero_h = jnp.zeros((T, HD), bf16)

    def blockdiag(a, b):
        return jnp.concatenate([jnp.concatenate([a, zero_h], axis=1), jnp.concatenate([zero_h, b], axis=1)], axis=0)

    for t in range(RH // 2):
        pair = (2 * t, 2 * t + 1)
        q16 = [qk_ref[h] for h in pair]
        k16 = [qk_ref[RH + h] for h in pair]
        v16 = [vg_ref[h] for h in pair]
        s2 = _nt(jnp.concatenate(q16, axis=1), blockdiag(*k16))
        m2 = (s2 * jnp.concatenate([wret_ref[h] for h in pair], axis=1)).astype(bf16)
        y2 = _dot(m2, blockdiag(*v16))
        for n, h in enumerate(pair):
            qf = q16[n].astype(f32)
            lc = jnp.concatenate([(qf * ef_ref[h]).astype(bf16), (qf * eb_ref[h]).astype(bf16)], axis=1)
            hcat = jnp.concatenate([hr[h].astype(bf16), gret_ref[0, h]], axis=1)
            y = y2[:, n * HD:(n + 1) * HD] + _nt(lc, hcat)
            mu = jnp.mean(y, axis=-1, keepdims=True)
            d = y - mu
            yn = d * lax.rsqrt(jnp.mean(d * d, axis=-1, keepdims=True) + EPS)
            gg = vg_ref[RH + h].astype(f32)
            yr_ref[:, h * HD:(h + 1) * HD] = ((yn * rnw_ref[h]) * _silu(gg)).astype(yr_ref.dtype)
            _ret_state_update(hr, h, v16[n], k16[n], wf_ref[h], rdec_ref[h][0:1, :])


def _fwd_sweep(L, qk, vg, zx, xc, bcc, dt, gret, gssd, tabs, bias, arow, dexp, snw, rnw, h0r, h0s):
    nc = L // T
    wret, ef, eb, wf, _, rdec = tabs

    def blk(lead, n, width):
        return pl.BlockSpec((n, T, width), lambda s: (lead, s, 0))

    def whole(a):
        nd = a.ndim
        return pl.BlockSpec(a.shape, lambda s: (0,) * nd)

    dsel, esel = _selector_constants()
    small = [wret, ef, eb, wf, rdec, bias, arow, dexp, snw, rnw, h0r, h0s, dsel, esel]
    in_specs = [blk(0, 2 * RH, HD), blk(0, 2 * RH, HD), blk(0, NG, GW), blk(0, NG, GW), blk(0, NG, 2 * NS),
                pl.BlockSpec((T, 2 * SH), lambda s: (s, 0)),
                pl.BlockSpec((1, RH, HD, HD), lambda s: (s, 0, 0, 0)),
                pl.BlockSpec((1, NG, GW, NS), lambda s: (s, 0, 0, 0))] + [whole(a) for a in small]
    return pl.pallas_call(
        _fwd_kernel,
        grid=(nc,),
        in_specs=in_specs,
        out_specs=[pl.BlockSpec((T, D), lambda s: (s, 0)), pl.BlockSpec((T, SI), lambda s: (s, 0))],
        out_shape=[jax.ShapeDtypeStruct((L, D), bf16), jax.ShapeDtypeStruct((L, SI), bf16)],
        scratch_shapes=[pltpu.VMEM((RH, HD, HD), f32), pltpu.VMEM((NG, GW, NS), f32), pltpu.VMEM((NG, T, GW), f32),
                        pltpu.VMEM((3, T, 2 * SH), f32), pltpu.VMEM((3, T, 2 * SH), f32),
                        pltpu.VMEM((3, 2 * SH, T), f32)]
        + [pltpu.VMEM((2 * SH, T), f32)] * 2,
        compiler_params=_cp(("arbitrary",)),
        name="fwd_sweep",
    )(qk, vg, zx, xc, bcc, dt, gret, gssd, *small)


def _branch_out_kernel(yr_ref, ys_ref, wr_ref, ws_ref, gr_ref, gs_ref, o_ref):
    for nb in range(o_ref.shape[1] // MXU_N):
        cols = slice(nb * MXU_N, (nb + 1) * MXU_N)
        acc_r = _dot(yr_ref[...], wr_ref[:, cols])
        acc_s = _dot(ys_ref[...], ws_ref[:, cols])
        m = _sigmoid(gr_ref[:, cols].astype(f32)) * acc_r + _sigmoid(gs_ref[:, cols].astype(f32)) * acc_s
        o_ref[:, cols] = m.astype(o_ref.dtype)


def _branch_out(yr, ys, wr, ws, gates):
    L = yr.shape[0]
    tm, tn = min(L, 1024), 256
    nj = D // tn
    return pl.pallas_call(
        _branch_out_kernel,
        grid=(L // tm, nj),
        in_specs=[pl.BlockSpec((tm, D), lambda i, j: (i, 0)),
                  pl.BlockSpec((tm, SI), lambda i, j: (i, 0)),
                  pl.BlockSpec((D, tn), lambda i, j: (0, j)),
                  pl.BlockSpec((SI, tn), lambda i, j: (0, j)),
                  pl.BlockSpec((tm, tn), lambda i, j: (i, j)),
                  pl.BlockSpec((tm, tn), lambda i, j: (i, nj + j))],
        out_specs=pl.BlockSpec((tm, tn), lambda i, j: (i, j)),
        out_shape=jax.ShapeDtypeStruct((L, D), bf16),
        compiler_params=_cp(("parallel", "parallel")),
        name="branch_out",
    )(yr, ys, wr, ws, gates, gates)


def _resid_kernel(m_ref, w_ref, x_ref, g_ref, nw_ref, sc_ref, sh_ref, h_ref, f_ref):
    h = x_ref[...] + g_ref[...] * _dot(m_ref[...], w_ref[...])
    h_ref[...] = h
    f_ref[...] = _rms_mod(h, nw_ref[...], sc_ref[...], sh_ref[...]).astype(f_ref.dtype)


def _mix_residual(m, w, x, gate, nw, sc, sh):
    L = m.shape[0]
    tm = min(L, 512)
    vec = pl.BlockSpec((1, D), lambda i: (0, 0))
    row = pl.BlockSpec((tm, D), lambda i: (i, 0))
    return pl.pallas_call(
        _resid_kernel,
        grid=(L // tm,),
        in_specs=[row, pl.BlockSpec((D, D), lambda i: (0, 0)), row, vec, vec, vec, vec],
        out_specs=[row, row],
        out_shape=[jax.ShapeDtypeStruct((L, D), f32), jax.ShapeDtypeStruct((L, D), bf16)],
        compiler_params=_cp(("parallel",)),
        name="mix_residual",
    )(m, w, x, gate, nw, sc, sh)


def _mlp2_kernel(a_ref, w_ref, h_ref, g_ref, fw_ref, o_ref, acc, *, nk):
    kk = pl.program_id(1)

    @pl.when(kk == 0)
    def _():
        acc[...] = jnp.zeros_like(acc)

    acc[...] += _dot(a_ref[...], w_ref[...])

    @pl.when(kk == nk - 1)
    def _():
        h2 = h_ref[...] + g_ref[...] * acc[...]
        y = h2 * lax.rsqrt(jnp.mean(h2 * h2, axis=-1, keepdims=True) + EPS)
        o_ref[...] = y * fw_ref[...]


def _mlp2_final(a, w, h, gate, fw):
    L = a.shape[0]
    tm, tk = min(L, 512), 2048
    nk = DFF // tk
    vec = pl.BlockSpec((1, D), lambda i, k: (0, 0))
    return pl.pallas_call(
        functools.partial(_mlp2_kernel, nk=nk),
        grid=(L // tm, nk),
        in_specs=[pl.BlockSpec((tm, tk), lambda i, k: (i, k)),
                  pl.BlockSpec((tk, D), lambda i, k: (k, 0)),
                  pl.BlockSpec((tm, D), lambda i, k: (i, 0)), vec, vec],
        out_specs=pl.BlockSpec((tm, D), lambda i, k: (i, 0)),
        out_shape=jax.ShapeDtypeStruct((L, D), f32),
        scratch_shapes=[pltpu.VMEM((tm, D), f32)],
        compiler_params=_cp(("parallel", "arbitrary")),
        name="mlp2_final",
    )(a, w, h, gate, fw)


def _split_conv(conv_w, conv_b):
    def grp(w, b, width):
        return (jnp.transpose(w.reshape(KC, NG, width), (1, 0, 2)), b.reshape(NG, 1, width))
    cwx, cbx = grp(conv_w[:, :SI], conv_b[:SI], GW)
    cwb, cbb = grp(conv_w[:, SI:SI + NG * NS], conv_b[SI:SI + NG * NS], NS)
    cwc, cbc = grp(conv_w[:, SI + NG * NS:], conv_b[SI + NG * NS:], NS)
    return cwx, cbx, cwb, cbb, cwc, cbc


def kernel(x, c, ctx, c_ctx, w_mod, b_mod, norm1_w, w_in, conv_w, conv_b, ret_decay_logit, ret_norm_w,
           ssd_a_log, ssd_dt_bias, ssd_d, ssd_norm_w, w_ret_out, w_ssd_out, w_o, norm2_w, w_mlp1, w_mlp2,
           final_norm_w):
    depth = w_mod.shape[0]
    assert depth == 1 and x.shape[0] == 1 and x.shape[2] == D
    L = x.shape[1]
    Lc = ctx.shape[1]
    assert L % T == 0 and Lc % T == 0 and L % GRID_W == 0
    xl = x[0]
    xcx = ctx[0]
    ly = 0

    a8 = jnp.zeros((8, D), f32).at[0].set(c[0]).at[1].set(c_ctx)
    mod = _modulation(a8, w_mod[ly], b_mod[ly][None, :])
    sh_a, sc_a, g_a, sh_f, sc_f, g_f = [mod[0:1, i * D:(i + 1) * D] for i in range(6)]
    csh_a, csc_a = mod[1:2, 0:D], mod[1:2, D:2 * D]

    wi = w_in[ly]
    o_q, o_k, o_v, o_z, o_x = 0, D, 2 * D, 4 * D, 4 * D + SI
    o_b = o_x + SI
    o_dt = o_b + 2 * NG * NS
    o_gate = o_dt + 2 * SH
    def proj(src, col, n, **kw):
        return _project(src, w_in, col0=col, ncols=n, **kw)

    convw = _split_conv(conv_w[ly], conv_b[ly])
    bias = ssd_dt_bias[ly].reshape(1, 2 * SH)
    arow = (-jnp.exp(ssd_a_log[ly])).reshape(1, 2 * SH)
    dexp = jnp.repeat(ssd_d[ly], P).reshape(NG, 1, GW)
    snw = ssd_norm_w[ly].reshape(NG, 1, GW)
    rnw = ret_norm_w[ly].reshape(RH, 1, HD)
    nw1 = norm1_w[ly][None, :]

    tabs = _ret_tables(ret_decay_logit[ly])
    _, _, _, wf_t, wb_t, rdec = tabs
    zr = jnp.zeros((RH, HD, HD), f32)
    zs = jnp.zeros((NG, GW, NS), f32)

    uc = _norm_mod(xcx, nw1, csc_a, csh_a)
    ones_c, zeros_c = jnp.ones((Lc // GRID_W, LANES), f32), jnp.zeros((Lc // GRID_W, LANES), f32)
    ident = (ones_c, zeros_c, jnp.ones((GRID_W, LANES), f32), jnp.zeros((GRID_W, LANES), f32))
    kc = proj(uc, o_k, D, bw=HD, tn=512, rope=ident, scale_from=0, name="ctx_k")
    vc = proj(uc, o_v, D, bw=HD, tn=512, name="ctx_v")
    bcx = proj(uc, o_b, NG * NS, bw=NS, tn=512, name="ctx_b")
    xsc = proj(uc, o_x, SI, bw=GW, tn=GW, name="ctx_x")
    dtc = proj(uc, o_dt, 2 * SH, bw=0, tn=2 * SH, out_dtype=f32, name="ctx_dt")
    common = dict(k=kc, k_blk=0, v=vc, v_blk=0, xs=xsc, xs_blk=0, bsrc=bcx, b_blk=0, c_blk=0, dt=dtc,
                  convw=convw, bias=bias, arow=arow, rdec=rdec, h0r=zr, h0s=zs)
    cfr, cfs = _state_sweep(Lc, fwd=True, emit=False, write_conv=False, rw=wf_t, **common)
    cbr, cbs = _state_sweep(Lc, fwd=False, emit=False, write_conv=False, rw=wb_t, **common)

    u = _norm_mod(xl, nw1, sc_a, sh_a)
    rope = _rope_tables(L)
    qk = proj(u, o_q, 2 * D, bw=HD, tn=1024, rope=rope, scale_from=D // 1024, name="lat_qk")
    vg = proj(u, o_v, 2 * D, bw=HD, tn=1024, name="lat_vg")
    zx = proj(u, o_z, 2 * SI, bw=GW, tn=1024, name="lat_zx")
    bcl = proj(u, o_b, 2 * NG * NS, bw=NS, tn=1024, name="lat_bc")
    dtl = proj(u, o_dt, 2 * SH, bw=0, tn=2 * SH, out_dtype=f32, name="lat_dt")
    gates = proj(u, o_gate, 2 * D, bw=0, tn=1024, name="lat_gates")

    gret, gssd, xcv, bccv, _, _ = _state_sweep(
        L, fwd=False, emit=True, write_conv=True, k=qk, k_blk=1, v=vg, v_blk=0, xs=zx, xs_blk=1, bsrc=bcl,
        b_blk=0, c_blk=1, dt=dtl, convw=convw, bias=bias, arow=arow, rw=wb_t, rdec=rdec,
        h0r=cbr, h0s=cbs)
    yr, ys = _fwd_sweep(L, qk, vg, zx, xcv, bccv, dtl, gret, gssd, tabs, bias, arow, dexp, snw, rnw, cfr, cfs)

    m = _branch_out(yr, ys, w_ret_out[ly].astype(bf16), w_ssd_out[ly].astype(bf16), gates)
    h1, f = _mix_residual(m, w_o[ly].astype(bf16), xl, g_a, norm2_w[ly][None, :], sc_f, sh_f)
    a = _project(f, w_mlp1, col0=0, ncols=DFF, bw=0, tn=1024, relu2=True, name="mlp1")
    out = _mlp2_final(a, w_mlp2[ly].astype(bf16), h1, g_f, final_norm_w[None, :])
    return out[None]
```

```python
import functools
import math

import numpy as np
import jax
import jax.numpy as jnp
from jax import lax
from jax.experimental import pallas as pl
from jax.experimental.pallas import tpu as pltpu

f32 = jnp.float32
bf16 = jnp.bfloat16

D = 2048
T = 128
GRID_W = 64
HD = 128
RH = D // HD
SI = 2 * D
P = 64
SH = SI // P
NS = 128
NG = 8
R = SH // NG
GW = R * P
KC = 5
DFF = 4 * D
ROPE_BASE = 10000.0
EPS = 1e-6
LANES = 128
MXU_N = 256
CAST_ROWS = 256
HALO = 16
LOG2E = 1.0 / math.log(2.0)
LOG2_FLOOR = -300.0
ONES_LANE = 96
VMEM_LIMIT = 48 * 1024 * 1024


def _cp(sem):
    return pltpu.CompilerParams(dimension_semantics=sem, vmem_limit_bytes=VMEM_LIMIT)


def _nt(a, b):
    return lax.dot_general(a, b, (((1,), (1,)), ((), ())), preferred_element_type=f32)


def _dot(a, b):
    return jnp.dot(a, b, preferred_element_type=f32)


def _sigmoid(x):
    return 0.5 + 0.5 * jnp.tanh(0.5 * x)


def _silu(x):
    h = 0.5 * x
    return h + h * jnp.tanh(h)


def _softplus(x):
    return jnp.maximum(x, 0.0) + jnp.log1p(jnp.exp(-jnp.abs(x)))


def _log_sigmoid(x):
    return jnp.minimum(x, 0.0) - jnp.log1p(jnp.exp(-jnp.abs(x)))


def _split3(x):
    p1 = x.astype(bf16).astype(f32)
    r1 = x - p1
    p2 = r1.astype(bf16).astype(f32)
    p3 = (r1 - p2).astype(bf16).astype(f32)
    return p1, p2, p3


def _cumsum_rows(la, tri):
    a1, a2, a3 = _split3(la)
    return _dot(tri, a1.astype(bf16)) + _dot(tri, a2.astype(bf16)) + _dot(tri, a3.astype(bf16))


def _selector_constants():
    dsel = np.zeros((R, 2 * SH, 2 * T), np.float32)
    esel = np.zeros((2 * SH, 2 * GW), np.float32)
    for r in range(R):
        for k in range(3):
            dsel[r, k * R + r, :T] = 1.0
            dsel[r, SH + k * R + r, T:] = -1.0
            esel[k * R + r, r * P:(r + 1) * P] = 1.0
            esel[SH + k * R + r, GW + r * P:GW + (r + 1) * P] = 1.0
    return jnp.asarray(dsel, bf16), jnp.asarray(esel, bf16)


def _tri_incl():
    ii = lax.broadcasted_iota(jnp.int32, (T, T), 0)
    jj = lax.broadcasted_iota(jnp.int32, (T, T), 1)
    return jnp.where(jj <= ii, 1.0, 0.0).astype(bf16)


def _rope_tab_kernel(rc_ref, rs_ref, cc_ref, cs_ref):
    def tab(n):
        idx = lax.broadcasted_iota(jnp.int32, (n, LANES), 0).astype(f32)
        lane = lax.broadcasted_iota(jnp.int32, (n, LANES), 1)
        freqs = jnp.exp((lane % (HD // 4)).astype(f32) * (-math.log(ROPE_BASE) / (HD // 4)))
        ang = idx * freqs
        s = jnp.sin(ang)
        return jnp.cos(ang), jnp.where(lane < HD // 2, -s, s)

    rc_ref[...], rs_ref[...] = tab(rc_ref.shape[0])
    cc_ref[...], cs_ref[...] = tab(GRID_W)


def _rope_tables(L):
    rows = L // GRID_W
    shapes = [jax.ShapeDtypeStruct((rows, LANES), f32)] * 2 + [jax.ShapeDtypeStruct((GRID_W, LANES), f32)] * 2
    return pl.pallas_call(_rope_tab_kernel, out_shape=shapes, name="rope_tables")()


def _rope_rows(row_tab, col_tab, tm):
    lane = lax.broadcasted_iota(jnp.int32, (GRID_W, LANES), 1)
    use_row = (lane % (HD // 2)) < HD // 4
    return jnp.concatenate([jnp.where(use_row, row_tab[r:r + 1, :], col_tab) for r in range(tm // GRID_W)], axis=0)


def _ret_tab_kernel(lg_ref, w_ref, ef_ref, eb_ref, wf_ref, wb_ref, dec_ref):
    lf = _log_sigmoid(lg_ref[0])
    lb = _log_sigmoid(lg_ref[1])
    ii = lax.broadcasted_iota(jnp.int32, (T, T), 0)
    jj = lax.broadcasted_iota(jnp.int32, (T, T), 1)
    dl = (ii - jj).astype(f32)
    w_ref[...] = jnp.exp(jnp.where(jj <= ii, dl * lf, -dl * lb))
    idx = lax.broadcasted_iota(jnp.int32, (T, LANES), 0).astype(f32)
    ef_ref[...] = jnp.exp((idx + 1.0) * lf)
    eb_ref[...] = jnp.exp((T - idx) * lb)
    wf_ref[...] = jnp.exp((T - 1.0 - idx) * lf)
    wb_ref[...] = jnp.exp(idx * lb)
    srow = lax.broadcasted_iota(jnp.int32, (8, LANES), 0)
    dec_ref[...] = jnp.where(srow == 0, jnp.exp(T * lf), jnp.exp(T * lb))


def _ret_tables(logit):
    lg = jnp.broadcast_to(logit[:, :, None, None], (2, RH, 1, LANES))
    tab = pl.BlockSpec((None, T, LANES), lambda h: (h, 0, 0))
    return pl.pallas_call(
        _ret_tab_kernel,
        grid=(RH,),
        in_specs=[pl.BlockSpec((2, None, 1, LANES), lambda h: (0, h, 0, 0))],
        out_specs=[tab, tab, tab, tab, tab, pl.BlockSpec((None, 8, LANES), lambda h: (h, 0, 0))],
        out_shape=[jax.ShapeDtypeStruct((RH, T, T), f32)] + [jax.ShapeDtypeStruct((RH, T, LANES), f32)] * 4
        + [jax.ShapeDtypeStruct((RH, 8, LANES), f32)],
        compiler_params=_cp(("parallel",)),
        name="ret_tables",
    )(lg)


def _mod_kernel(a_ref, w_ref, b_ref, o_ref):
    a = _silu(a_ref[...])
    o_ref[...] = _dot(a, w_ref[...]) + b_ref[...]


def _modulation(a8, w, b):
    tn = 1024
    n = w.shape[1]
    return pl.pallas_call(
        _mod_kernel,
        grid=(n // tn,),
        in_specs=[pl.BlockSpec((8, D), lambda j: (0, 0)),
                  pl.BlockSpec((D, tn), lambda j: (0, j)),
                  pl.BlockSpec((1, tn), lambda j: (0, j))],
        out_specs=pl.BlockSpec((8, tn), lambda j: (0, j)),
        out_shape=jax.ShapeDtypeStruct((8, n), f32),
        compiler_params=_cp(("parallel",)),
        name="modulation",
    )(a8, w, b)


def _rms_mod(x, nw, sc, sh):
    y = x * lax.rsqrt(jnp.mean(x * x, axis=-1, keepdims=True) + EPS)
    return (y * nw) * (1.0 + sc) + sh


def _norm_kernel(x_ref, nw_ref, sc_ref, sh_ref, o_ref):
    o_ref[...] = _rms_mod(x_ref[...], nw_ref[...], sc_ref[...], sh_ref[...]).astype(o_ref.dtype)


def _norm_mod(x, nw, sc, sh):
    L = x.shape[0]
    tm = min(L, 512)
    vec = pl.BlockSpec((1, D), lambda i: (0, 0))
    return pl.pallas_call(
        _norm_kernel,
        grid=(L // tm,),
        in_specs=[pl.BlockSpec((tm, D), lambda i: (i, 0)), vec, vec, vec],
        out_specs=pl.BlockSpec((tm, D), lambda i: (i, 0)),
        out_shape=jax.ShapeDtypeStruct((L, D), bf16),
        compiler_params=_cp(("parallel",)),
        name="norm_mod",
    )(x, nw, sc, sh)


def _proj_kernel(u_ref, w_ref, *rest, bw, tn, rope, scale_from, w32, relu2):
    if w32:
        *rest, wb = rest
        if len(w_ref.shape) == 3:
            w_ref = w_ref.at[0]

        @pl.when(pl.program_id(1) == 0)
        def _():
            lane = lax.broadcasted_iota(jnp.int32, (CAST_ROWS, tn), 1) % HD
            for rb in range(D // CAST_ROWS):
                rows = slice(rb * CAST_ROWS, (rb + 1) * CAST_ROWS)
                w = w_ref[rows, :]
                if rope:
                    qd = HD // 4
                    w = jnp.where((lane >= qd) & (lane < 2 * qd), pltpu.roll(w, tn - qd, 1),
                                  jnp.where((lane >= 2 * qd) & (lane < 3 * qd), pltpu.roll(w, qd, 1), w))
                wb[rows, :] = w.astype(bf16)
    else:
        wb = w_ref
    if rope:
        rc_ref, rs_ref, cc_ref, cs_ref, o_ref = rest
        tm = u_ref.shape[0]
        scale = jnp.where(pl.program_id(0) >= scale_from, HD ** -0.5, 1.0).astype(f32)
        cos = _rope_rows(rc_ref[...], cc_ref[...], tm) * scale
        sin = _rope_rows(rs_ref[...], cs_ref[...], tm) * scale
    else:
        (o_ref,) = rest
    sw = min(tn, MXU_N)
    for nb in range(tn // sw):
        acc = _dot(u_ref[...], wb[:, nb * sw:(nb + 1) * sw])
        if relu2:
            acc = jnp.maximum(acc, 0.0)
            acc = acc * acc
        if rope:
            for hh in range(sw // HD):
                t = acc[:, hh * HD:(hh + 1) * HD]
                o_ref[nb * (sw // HD) + hh] = (t * cos + pltpu.roll(t, HD // 2, 1) * sin).astype(o_ref.dtype)
        elif bw and bw <= sw:
            for hh in range(sw // bw):
                o_ref[nb * (sw // bw) + hh] = acc[:, hh * bw:(hh + 1) * bw].astype(o_ref.dtype)
        elif bw:
            per = bw // sw
            o_ref[nb // per, :, (nb % per) * sw:(nb % per + 1) * sw] = acc.astype(o_ref.dtype)
        else:
            o_ref[:, nb * sw:(nb + 1) * sw] = acc.astype(o_ref.dtype)


def _project(u, w, *, col0, ncols, bw, tn, out_dtype=bf16, rope=None, scale_from=0, relu2=False, name):
    M = u.shape[0]
    N = ncols
    assert col0 % LANES == 0 and ncols % tn == 0
    jb = col0 // tn
    w32 = w.dtype == f32
    tm = min(M, 1024 if w32 else 2048)
    if w32 and col0 % tn:
        w_spec = pl.BlockSpec((pl.Element(1), pl.Element(D), pl.Element(tn)),
                              lambda j, i: (0, 0, pl.multiple_of(col0 + j * tn, LANES)))
    elif w32:
        w_spec = pl.BlockSpec((None, D, tn), lambda j, i: (0, 0, jb + j))
    else:
        assert col0 % tn == 0
        w_spec = pl.BlockSpec((D, tn), lambda j, i: (0, jb + j))
    in_specs = [pl.BlockSpec((tm, D), lambda j, i: (i, 0)), w_spec]
    args = [u, w]
    if rope is not None:
        in_specs += [pl.BlockSpec((tm // GRID_W, LANES), lambda j, i: (i, 0))] * 2
        in_specs += [pl.BlockSpec((GRID_W, LANES), lambda j, i: (0, 0))] * 2
        args += list(rope)
    if bw:
        out_spec = pl.BlockSpec((tn // bw, tm, bw), lambda j, i: (j, i, 0))
        out_shape = jax.ShapeDtypeStruct((N // bw, M, bw), out_dtype)
    else:
        out_spec = pl.BlockSpec((tm, tn), lambda j, i: (i, j))
        out_shape = jax.ShapeDtypeStruct((M, N), out_dtype)
    return pl.pallas_call(
        functools.partial(_proj_kernel, bw=bw, tn=tn, rope=rope is not None, scale_from=scale_from, w32=w32,
                          relu2=relu2),
        grid=(N // tn, M // tm),
        in_specs=in_specs,
        out_specs=out_spec,
        out_shape=out_shape,
        scratch_shapes=[pltpu.VMEM((D, tn), bf16)] if w32 else [],
        compiler_params=_cp(("parallel", "arbitrary")),
        name=name,
    )(*args)


def _shift_matrix():
    rows = lax.broadcasted_iota(jnp.int32, (4 * T, T + 2 * HALO), 0)
    cols = lax.broadcasted_iota(jnp.int32, (4 * T, T + 2 * HALO), 1)
    blk = rows // T
    tap = jnp.where(blk < KC // 2, blk, blk + 1)
    return jnp.where(cols == (rows - blk * T) + HALO + tap - KC // 2, 1.0, 0.0).astype(bf16)


def _conv_shift(shift, main, prev, nxt, has_prev, has_next):
    zero = jnp.zeros_like(prev)
    ext = jnp.concatenate([jnp.where(has_prev, prev, zero), main, jnp.where(has_next, nxt, zero)], axis=0)
    sh = _dot(shift, ext)
    return [sh[0:T], sh[T:2 * T], main.astype(f32), sh[2 * T:3 * T], sh[3 * T:4 * T]]


def _conv_taps(taps, cw, cb):
    acc = cb
    for j in range(KC):
        acc = acc + taps[j] * cw[j:j + 1, :]
    return _silu(acc)


def _decay_prologue(dt_ref, bias_ref, arow_ref):
    dt = _softplus(dt_ref[...] + bias_ref[...])
    la = dt * arow_ref[...]
    acs = _cumsum_rows(la, _tri_incl())
    tot = acs[T - 1:T, :]
    return dt, la, acs, tot


def _transpose_blocks(x):
    n = x.shape[1] // LANES
    return jnp.concatenate([x[:, b * LANES:(b + 1) * LANES].T for b in range(n)], axis=0)


def _ssd_state_update(hs_ref, g, xc, bc16, w_rows, dec_rows):
    xT = _transpose_blocks(xc)
    lhs = jnp.concatenate(
        [(xT[r * P:(r + 1) * P, :] * w_rows[r:r + 1, :]).astype(bf16) for r in range(R)], axis=0)
    upd = _dot(lhs, bc16)
    old = hs_ref[g]
    hs_ref[g] = jnp.concatenate(
        [old[r * P:(r + 1) * P, :] * dec_rows[r:r + 1, :] + upd[r * P:(r + 1) * P, :] for r in range(R)], axis=0)


def _ret_state_update(hr_ref, h, v16, k16, wcol, dec_row):
    vw = (v16.astype(f32) * wcol).T.astype(bf16)
    hr_ref[h] = hr_ref[h] * dec_row + _dot(vw, k16)


def _state_kernel(*refs, nc, fwd, emit, write_conv):
    it = iter(refs)
    k_ref, v_ref = next(it), next(it)
    xs_ref, xs_p, xs_n = next(it), next(it), next(it)
    b_ref, b_p, b_n = next(it), next(it), next(it)
    if write_conv:
        c_ref, c_p, c_n = next(it), next(it), next(it)
    dt_ref = next(it)
    cwx, cbx, cwb, cbb = next(it), next(it), next(it), next(it)
    if write_conv:
        cwc, cbc = next(it), next(it)
    bias_ref, arow_ref = next(it), next(it)
    rw_ref, rdec_ref = next(it), next(it)
    h0r_ref, h0s_ref = next(it), next(it)
    if emit:
        gr_ref, gs_ref = next(it), next(it)
    if write_conv:
        xo_ref, bco_ref = next(it), next(it)
    hfr_ref, hfs_ref = next(it), next(it)
    hr, hs, wT_s, decT_s = next(it), next(it), next(it), next(it)

    s = pl.program_id(0)
    c = s if fwd else nc - 1 - s
    has_prev = c > 0
    has_next = c < nc - 1

    @pl.when(s == 0)
    def _():
        hr[...] = h0r_ref[...]
        hs[...] = h0s_ref[...]

    if emit:
        gr_ref[0] = hr[...].astype(bf16)
        gs_ref[0] = hs[...].astype(bf16)

    dt, la, acs, tot = _decay_prologue(dt_ref, bias_ref, arow_ref)
    wexp = jnp.exp(tot - acs) if fwd else jnp.exp(acs - la)
    wT_s[...] = (wexp * dt).T
    decT_s[...] = jnp.broadcast_to(jnp.exp(tot), (T, 2 * SH)).T
    off = 0 if fwd else SH

    shift = _shift_matrix()

    def cat(a, b):
        return jnp.concatenate([a, b], axis=1)

    def shifted(g):
        tx = _conv_shift(shift, xs_ref[g], xs_p[g], xs_n[g], has_prev, has_next)
        if write_conv:
            tb = _conv_shift(shift, cat(b_ref[g], c_ref[g]), cat(b_p[g], c_p[g]), cat(b_n[g], c_n[g]),
                             has_prev, has_next)
        else:
            tb = _conv_shift(shift, b_ref[g], b_p[g], b_n[g], has_prev, has_next)
        return tx, tb

    nxt_taps = shifted(0)
    for g in range(NG):
        tx, tb = nxt_taps
        if g + 1 < NG:
            nxt_taps = shifted(g + 1)
        xc16 = _conv_taps(tx, cwx[g], cbx[g]).astype(bf16)
        if write_conv:
            bcc = _conv_taps(tb, cat(cwb[g], cwc[g]), cat(cbb[g], cbc[g])).astype(bf16)
            bc16 = bcc[:, :NS]
            xo_ref[g] = xc16
            bco_ref[g] = bcc
        else:
            bc16 = _conv_taps(tb, cwb[g], cbb[g]).astype(bf16)
        r0 = off + g * R
        _ssd_state_update(hs, g, xc16.astype(f32), bc16, wT_s[r0:r0 + R, :], decT_s[r0:r0 + R, :])

    dsel = 0 if fwd else 1
    for h in range(RH):
        _ret_state_update(hr, h, v_ref[h], k_ref[h], rw_ref[h], rdec_ref[h][dsel:dsel + 1, :])

    @pl.when(s == nc - 1)
    def _():
        hfr_ref[...] = hr[...]
        hfs_ref[...] = hs[...]


def _state_sweep(L, *, fwd, emit, write_conv, k, k_blk, v, v_blk, xs, xs_blk, bsrc, b_blk, c_blk, dt,
                 convw, bias, arow, rw, rdec, h0r, h0s):
    nc = L // T
    tb = T // HALO
    nrb = L // HALO

    def cidx(s):
        return s if fwd else nc - 1 - s

    def main(lead, n, width):
        return pl.BlockSpec((n, T, width), lambda s: (lead, cidx(s), 0))

    def prev(lead, n, width):
        return pl.BlockSpec((n, HALO, width), lambda s: (lead, jnp.maximum(cidx(s) * tb - 1, 0), 0))

    def nxt(lead, n, width):
        return pl.BlockSpec((n, HALO, width), lambda s: (lead, jnp.minimum((cidx(s) + 1) * tb, nrb - 1), 0))

    def whole(a):
        nd = a.ndim
        return pl.BlockSpec(a.shape, lambda s: (0,) * nd)

    cwx, cbx, cwb, cbb, cwc, cbc = convw
    in_specs = [main(k_blk, RH, HD), main(v_blk, RH, HD),
                main(xs_blk, NG, GW), prev(xs_blk, NG, GW), nxt(xs_blk, NG, GW),
                main(b_blk, NG, NS), prev(b_blk, NG, NS), nxt(b_blk, NG, NS)]
    args = [k, v, xs, xs, xs, bsrc, bsrc, bsrc]
    if write_conv:
        in_specs += [main(c_blk, NG, NS), prev(c_blk, NG, NS), nxt(c_blk, NG, NS)]
        args += [bsrc, bsrc, bsrc]
    in_specs += [pl.BlockSpec((T, 2 * SH), lambda s: (cidx(s), 0))]
    args += [dt]
    small = [cwx, cbx, cwb, cbb] + ([cwc, cbc] if write_conv else []) + [bias, arow, rw, rdec, h0r, h0s]
    in_specs += [whole(a) for a in small]
    args += small

    out_specs, out_shape = [], []
    if emit:
        out_specs += [pl.BlockSpec((1, RH, HD, HD), lambda s: (cidx(s), 0, 0, 0)),
                      pl.BlockSpec((1, NG, GW, NS), lambda s: (cidx(s), 0, 0, 0))]
        out_shape += [jax.ShapeDtypeStruct((nc, RH, HD, HD), bf16), jax.ShapeDtypeStruct((nc, NG, GW, NS), bf16)]
    if write_conv:
        out_specs += [pl.BlockSpec((NG, T, GW), lambda s: (0, cidx(s), 0)),
                      pl.BlockSpec((NG, T, 2 * NS), lambda s: (0, cidx(s), 0))]
        out_shape += [jax.ShapeDtypeStruct((NG, L, GW), bf16), jax.ShapeDtypeStruct((NG, L, 2 * NS), bf16)]
    out_specs += [pl.BlockSpec((RH, HD, HD), lambda s: (0, 0, 0)), pl.BlockSpec((NG, GW, NS), lambda s: (0, 0, 0))]
    out_shape += [jax.ShapeDtypeStruct((RH, HD, HD), f32), jax.ShapeDtypeStruct((NG, GW, NS), f32)]

    return pl.pallas_call(
        functools.partial(_state_kernel, nc=nc, fwd=fwd, emit=emit, write_conv=write_conv),
        grid=(nc,),
        in_specs=in_specs,
        out_specs=out_specs,
        out_shape=out_shape,
        scratch_shapes=[pltpu.VMEM((RH, HD, HD), f32), pltpu.VMEM((NG, GW, NS), f32),
                        pltpu.VMEM((2 * SH, T), f32), pltpu.VMEM((2 * SH, T), f32)],
        compiler_params=_cp(("arbitrary",)),
        name="state_sweep_" + ("f" if fwd else "b") + ("_emit" if emit else ""),
    )(*args)


def _fwd_kernel(qk_ref, vg_ref, z_ref, xs_ref, bc_ref, dt_ref, gret_ref, gssd_ref,
                wret_ref, ef_ref, eb_ref, wf_ref, rdec_ref, bias_ref, arow_ref, dexp_ref, snw_ref, rnw_ref,
                h0r_ref, h0s_ref, dsel_ref, esel_ref, yr_ref, ys_ref,
                hr, hs, ypre, ap_s, ep_s, apT_s, wT_s, decT_s):
    s = pl.program_id(0)

    @pl.when(s == 0)
    def _():
        hr[...] = h0r_ref[...]
        hs[...] = h0s_ref[...]

    dt, la, acs, tot = _decay_prologue(dt_ref, bias_ref, arow_ref)
    lane = lax.broadcasted_iota(jnp.int32, (T, 2 * SH), 1)
    is_f = lane < SH
    a1 = jnp.where(is_f, acs, acs - la)
    e1 = jnp.exp(jnp.where(is_f, acs, tot - (acs - la)))
    a2 = a1 * LOG2E
    ldt = jnp.maximum(jnp.log2(dt), LOG2_FLOOR)
    for k, part in enumerate(_split3(a2)):
        ap_s[k] = part
    for k, part in enumerate(_split3(jnp.where(is_f, ldt - a2, ldt + a2))):
        apT_s[k] = part.T
    for k, part in enumerate(_split3(e1)):
        ep_s[k] = part
    wT_s[...] = (jnp.exp(tot - acs) * dt).T
    decT_s[...] = jnp.broadcast_to(jnp.exp(tot), (T, 2 * SH)).T

    ii = lax.broadcasted_iota(jnp.int32, (T, T), 0)
    jj = lax.broadcasted_iota(jnp.int32, (T, T), 1)
    lower = jj <= ii
    part_masks = [((lane >= k * R) & (lane < (k + 1) * R)) | ((lane >= SH + k * R) & (lane < SH + (k + 1) * R))
                  for k in range(3)]
    ones_lanes = (lane >= ONES_LANE) & (lane < ONES_LANE + 3)
    row16 = lax.broadcasted_iota(jnp.int32, (16, T), 0)
    lane_p = lax.broadcasted_iota(jnp.int32, (T, 2 * P), 1)

    def pack(parts_ref, g, with_ones):
        acc = jnp.where(ones_lanes, 1.0, 0.0) if with_ones else jnp.zeros((T, 2 * SH), f32)
        for k in range(3):
            shift = (2 * SH - g * R + k * R) % (2 * SH)
            acc = acc + jnp.where(part_masks[k], pltpu.roll(parts_ref[k], shift, 1), 0.0)
        return acc.astype(bf16)

    def front(g):
        b16 = bc_ref[g, :, 0:NS]
        c16 = bc_ref[g, :, NS:2 * NS]
        sc = _nt(c16, b16)
        r0 = g * R
        lhs_a = pack(ap_s, g, True)
        lhs_e = pack(ep_s, g, False)
        cf_rows = [apT_s[k, r0:r0 + R, :] for k in range(3)]
        cb_rows = [apT_s[k, SH + r0:SH + r0 + R, :] for k in range(3)]
        hcat = jnp.concatenate([hs[g].astype(bf16), gssd_ref[0, g]], axis=0)
        ycross = _nt(c16, hcat)
        efb = _dot(lhs_e, esel_ref[...])
        dmats = []
        for r in range(R):
            lo = jnp.zeros((16, T), f32)
            up = jnp.zeros((16, T), f32)
            for k in range(3):
                lo = jnp.where(row16 == k, cf_rows[k][r:r + 1, :], lo)
                up = jnp.where(row16 == k, cb_rows[k][r:r + 1, :], up)
            dyn = jnp.concatenate([lo, up], axis=1).astype(bf16)
            rhs = jnp.concatenate([dsel_ref[r, 0:ONES_LANE, :], dyn, dsel_ref[r, ONES_LANE + 16:, :]], axis=0)
            dmats.append(_dot(lhs_a, rhs))
        return b16, sc, ycross, efb, dmats

    def back(g, staged, ssq):
        b16, sc, ycross, efb, dmats = staged
        r0 = g * R
        xg = xs_ref[g]
        zg = z_ref[g].astype(f32)
        dg = dexp_ref[g]
        ms = [(sc * jnp.exp2(jnp.where(lower, dm[:, :T], dm[:, T:]))).astype(bf16) for dm in dmats]
        ys_parts = []
        for t in range(R // 2):
            xp = xg[:, t * 2 * P:(t + 1) * 2 * P]
            zero = jnp.zeros_like(xp)
            rhs = jnp.concatenate([jnp.where(lane_p < P, xp, zero), jnp.where(lane_p >= P, xp, zero)], axis=0)
            ys_parts.append(_dot(jnp.concatenate([ms[2 * t], ms[2 * t + 1]], axis=1), rhs))
        yg = jnp.concatenate(ys_parts, axis=1)
        yg = yg + ycross[:, :GW] * efb[:, :GW] + ycross[:, GW:] * efb[:, GW:]
        yg = (yg + xg.astype(f32) * dg) * _silu(zg)
        ypre[g] = yg
        ssq = ssq + jnp.sum(yg * yg, axis=1, keepdims=True)
        _ssd_state_update(hs, g, xg.astype(f32), b16, wT_s[r0:r0 + R, :], decT_s[r0:r0 + R, :])
        return ssq

    ssq = jnp.zeros((T, 1), f32)
    staged = front(0)
    for g in range(NG):
        cur = staged
        if g + 1 < NG:
            staged = front(g + 1)
        ssq = back(g, cur, ssq)
    rs = lax.rsqrt(ssq * (1.0 / SI) + EPS)
    for g in range(NG):
        ys_ref[:, g * GW:(g + 1) * GW] = ((ypre[g] * rs) * snw_ref[g]).astype(ys_ref.dtype)

    zero_h = jnp.zeros((T, HD), bf16)

    def blockdiag(a, b):
        return jnp.concatenate([jnp.concatenate([a, zero_h], axis=1), jnp.concatenate([zero_h, b], axis=1)], axis=0)

    def ret_front(t):
        pair = (2 * t, 2 * t + 1)
        q16 = [qk_ref[h] for h in pair]
        k16 = [qk_ref[RH + h] for h in pair]
        s2 = _nt(jnp.concatenate(q16, axis=1), blockdiag(*k16))
        cross = []
        for n, h in enumerate(pair):
            qf = q16[n].astype(f32)
            lc = jnp.concatenate([(qf * ef_ref[h]).astype(bf16), (qf * eb_ref[h]).astype(bf16)], axis=1)
            hcat = jnp.concatenate([hr[h].astype(bf16), gret_ref[0, h]], axis=1)
            cross.append(_nt(lc, hcat))
        return k16, s2, cross

    ret_staged = ret_front(0)
    for t in range(RH // 2):
        pair = (2 * t, 2 * t + 1)
        k16, s2, cross = ret_staged
        if t + 1 < RH // 2:
            ret_staged = ret_front(t + 1)
        v16 = [vg_ref[h] for h in pair]
        m2 = (s2 * jnp.concatenate([wret_ref[h] for h in pair], axis=1)).astype(bf16)
        y2 = _dot(m2, blockdiag(*v16))
        for n, h in enumerate(pair):
            y = y2[:, n * HD:(n + 1) * HD] + cross[n]
            mu = jnp.mean(y, axis=-1, keepdims=True)
            d = y - mu
            yn = d * lax.rsqrt(jnp.mean(d * d, axis=-1, keepdims=True) + EPS)
            gg = vg_ref[RH + h].astype(f32)
            yr_ref[:, h * HD:(h + 1) * HD] = ((yn * rnw_ref[h]) * _silu(gg)).astype(yr_ref.dtype)
            _ret_state_update(hr, h, v16[n], k16[n], wf_ref[h], rdec_ref[h][0:1, :])


def _fwd_sweep(L, qk, vg, zx, xc, bcc, dt, gret, gssd, tabs, bias, arow, dexp, snw, rnw, h0r, h0s):
    nc = L // T
    wret, ef, eb, wf, _, rdec = tabs

    def blk(lead, n, width):
        return pl.BlockSpec((n, T, width), lambda s: (lead, s, 0))

    def whole(a):
        nd = a.ndim
        return pl.BlockSpec(a.shape, lambda s: (0,) * nd)

    dsel, esel = _selector_constants()
    small = [wret, ef, eb, wf, rdec, bias, arow, dexp, snw, rnw, h0r, h0s, dsel, esel]
    in_specs = [blk(0, 2 * RH, HD), blk(0, 2 * RH, HD), blk(0, NG, GW), blk(0, NG, GW), blk(0, NG, 2 * NS),
                pl.BlockSpec((T, 2 * SH), lambda s: (s, 0)),
                pl.BlockSpec((1, RH, HD, HD), lambda s: (s, 0, 0, 0)),
                pl.BlockSpec((1, NG, GW, NS), lambda s: (s, 0, 0, 0))] + [whole(a) for a in small]
    return pl.pallas_call(
        _fwd_kernel,
        grid=(nc,),
        in_specs=in_specs,
        out_specs=[pl.BlockSpec((T, D), lambda s: (s, 0)), pl.BlockSpec((T, SI), lambda s: (s, 0))],
        out_shape=[jax.ShapeDtypeStruct((L, D), bf16), jax.ShapeDtypeStruct((L, SI), bf16)],
        scratch_shapes=[pltpu.VMEM((RH, HD, HD), f32), pltpu.VMEM((NG, GW, NS), f32), pltpu.VMEM((NG, T, GW), f32),
                        pltpu.VMEM((3, T, 2 * SH), f32), pltpu.VMEM((3, T, 2 * SH), f32),
                        pltpu.VMEM((3, 2 * SH, T), f32)]
        + [pltpu.VMEM((2 * SH, T), f32)] * 2,
        compiler_params=_cp(("arbitrary",)),
        name="fwd_sweep",
    )(qk, vg, zx, xc, bcc, dt, gret, gssd, *small)


def _branch_out_kernel(yr_ref, ys_ref, wr_ref, ws_ref, gr_ref, gs_ref, o_ref):
    for nb in range(o_ref.shape[1] // MXU_N):
        cols = slice(nb * MXU_N, (nb + 1) * MXU_N)
        acc_r = _dot(yr_ref[...], wr_ref[:, cols])
        acc_s = _dot(ys_ref[...], ws_ref[:, cols])
        m = _sigmoid(gr_ref[:, cols].astype(f32)) * acc_r + _sigmoid(gs_ref[:, cols].astype(f32)) * acc_s
        o_ref[:, cols] = m.astype(o_ref.dtype)


def _branch_out(yr, ys, wr, ws, gates):
    L = yr.shape[0]
    tm, tn = min(L, 1024), 256
    nj = D // tn
    return pl.pallas_call(
        _branch_out_kernel,
        grid=(L // tm, nj),
        in_specs=[pl.BlockSpec((tm, D), lambda i, j: (i, 0)),
                  pl.BlockSpec((tm, SI), lambda i, j: (i, 0)),
                  pl.BlockSpec((D, tn), lambda i, j: (0, j)),
                  pl.BlockSpec((SI, tn), lambda i, j: (0, j)),
                  pl.BlockSpec((tm, tn), lambda i, j: (i, j)),
                  pl.BlockSpec((tm, tn), lambda i, j: (i, nj + j))],
        out_specs=pl.BlockSpec((tm, tn), lambda i, j: (i, j)),
        out_shape=jax.ShapeDtypeStruct((L, D), bf16),
        compiler_params=_cp(("parallel", "parallel")),
        name="branch_out",
    )(yr, ys, wr, ws, gates, gates)


def _resid_kernel(m_ref, w_ref, x_ref, g_ref, nw_ref, sc_ref, sh_ref, h_ref, f_ref):
    h = x_ref[...] + g_ref[...] * _dot(m_ref[...], w_ref[...])
    h_ref[...] = h
    f_ref[...] = _rms_mod(h, nw_ref[...], sc_ref[...], sh_ref[...]).astype(f_ref.dtype)


def _mix_residual(m, w, x, gate, nw, sc, sh):
    L = m.shape[0]
    tm = min(L, 512)
    vec = pl.BlockSpec((1, D), lambda i: (0, 0))
    row = pl.BlockSpec((tm, D), lambda i: (i, 0))
    return pl.pallas_call(
        _resid_kernel,
        grid=(L // tm,),
        in_specs=[row, pl.BlockSpec((D, D), lambda i: (0, 0)), row, vec, vec, vec, vec],
        out_specs=[row, row],
        out_shape=[jax.ShapeDtypeStruct((L, D), f32), jax.ShapeDtypeStruct((L, D), bf16)],
        compiler_params=_cp(("parallel",)),
        name="mix_residual",
    )(m, w, x, gate, nw, sc, sh)


def _mlp2_kernel(a_ref, w_ref, h_ref, g_ref, fw_ref, o_ref, acc, *, nk):
    kk = pl.program_id(1)

    @pl.when(kk == 0)
    def _():
        acc[...] = jnp.zeros_like(acc)

    acc[...] += _dot(a_ref[...], w_ref[...])

    @pl.when(kk == nk - 1)
    def _():
        h2 = h_ref[...] + g_ref[...] * acc[...]
        y = h2 * lax.rsqrt(jnp.mean(h2 * h2, axis=-1, keepdims=True) + EPS)
        o_ref[...] = y * fw_ref[...]


def _mlp2_final(a, w, h, gate, fw):
    L = a.shape[0]
    tm, tk = min(L, 512), 2048
    nk = DFF // tk
    vec = pl.BlockSpec((1, D), lambda i, k: (0, 0))
    return pl.pallas_call(
        functools.partial(_mlp2_kernel, nk=nk),
        grid=(L // tm, nk),
        in_specs=[pl.BlockSpec((tm, tk), lambda i, k: (i, k)),
                  pl.BlockSpec((tk, D), lambda i, k: (k, 0)),
                  pl.BlockSpec((tm, D), lambda i, k: (i, 0)), vec, vec],
        out_specs=pl.BlockSpec((tm, D), lambda i, k: (i, 0)),
        out_shape=jax.ShapeDtypeStruct((L, D), f32),
        scratch_shapes=[pltpu.VMEM((tm, D), f32)],
        compiler_params=_cp(("parallel", "arbitrary")),
        name="mlp2_final",
    )(a, w, h, gate, fw)


def _split_conv(conv_w, conv_b):
    def grp(w, b, width):
        return (jnp.transpose(w.reshape(KC, NG, width), (1, 0, 2)), b.reshape(NG, 1, width))
    cwx, cbx = grp(conv_w[:, :SI], conv_b[:SI], GW)
    cwb, cbb = grp(conv_w[:, SI:SI + NG * NS], conv_b[SI:SI + NG * NS], NS)
    cwc, cbc = grp(conv_w[:, SI + NG * NS:], conv_b[SI + NG * NS:], NS)
    return cwx, cbx, cwb, cbb, cwc, cbc


def kernel(x, c, ctx, c_ctx, w_mod, b_mod, norm1_w, w_in, conv_w, conv_b, ret_decay_logit, ret_norm_w,
           ssd_a_log, ssd_dt_bias, ssd_d, ssd_norm_w, w_ret_out, w_ssd_out, w_o, norm2_w, w_mlp1, w_mlp2,
           final_norm_w):
    depth = w_mod.shape[0]
    assert depth == 1 and x.shape[0] == 1 and x.shape[2] == D
    L = x.shape[1]
    Lc = ctx.shape[1]
    assert L % T == 0 and Lc % T == 0 and L % GRID_W == 0
    xl = x[0]
    xcx = ctx[0]
    ly = 0

    a8 = jnp.zeros((8, D), f32).at[0].set(c[0]).at[1].set(c_ctx)
    mod = _modulation(a8, w_mod[ly], b_mod[ly][None, :])
    sh_a, sc_a, g_a, sh_f, sc_f, g_f = [mod[0:1, i * D:(i + 1) * D] for i in range(6)]
    csh_a, csc_a = mod[1:2, 0:D], mod[1:2, D:2 * D]

    wi = w_in[ly]
    o_q, o_k, o_v, o_z, o_x = 0, D, 2 * D, 4 * D, 4 * D + SI
    o_b = o_x + SI
    o_dt = o_b + 2 * NG * NS
    o_gate = o_dt + 2 * SH
    def proj(src, col, n, **kw):
        return _project(src, w_in, col0=col, ncols=n, **kw)

    convw = _split_conv(conv_w[ly], conv_b[ly])
    bias = ssd_dt_bias[ly].reshape(1, 2 * SH)
    arow = (-jnp.exp(ssd_a_log[ly])).reshape(1, 2 * SH)
    dexp = jnp.repeat(ssd_d[ly], P).reshape(NG, 1, GW)
    snw = ssd_norm_w[ly].reshape(NG, 1, GW)
    rnw = ret_norm_w[ly].reshape(RH, 1, HD)
    nw1 = norm1_w[ly][None, :]

    tabs = _ret_tables(ret_decay_logit[ly])
    _, _, _, wf_t, wb_t, rdec = tabs
    zr = jnp.zeros((RH, HD, HD), f32)
    zs = jnp.zeros((NG, GW, NS), f32)

    uc = _norm_mod(xcx, nw1, csc_a, csh_a)
    ones_c, zeros_c = jnp.ones((Lc // GRID_W, LANES), f32), jnp.zeros((Lc // GRID_W, LANES), f32)
    ident = (ones_c, zeros_c, jnp.ones((GRID_W, LANES), f32), jnp.zeros((GRID_W, LANES), f32))
    kc = proj(uc, o_k, D, bw=HD, tn=512, rope=ident, scale_from=0, name="ctx_k")
    vc = proj(uc, o_v, D, bw=HD, tn=512, name="ctx_v")
    bcx = proj(uc, o_b, NG * NS, bw=NS, tn=512, name="ctx_b")
    xsc = proj(uc, o_x, SI, bw=GW, tn=GW, name="ctx_x")
    dtc = proj(uc, o_dt, 2 * SH, bw=0, tn=2 * SH, out_dtype=f32, name="ctx_dt")
    common = dict(k=kc, k_blk=0, v=vc, v_blk=0, xs=xsc, xs_blk=0, bsrc=bcx, b_blk=0, c_blk=0, dt=dtc,
                  convw=convw, bias=bias, arow=arow, rdec=rdec, h0r=zr, h0s=zs)
    cfr, cfs = _state_sweep(Lc, fwd=True, emit=False, write_conv=False, rw=wf_t, **common)
    cbr, cbs = _state_sweep(Lc, fwd=False, emit=False, write_conv=False, rw=wb_t, **common)

    u = _norm_mod(xl, nw1, sc_a, sh_a)
    rope = _rope_tables(L)
    qk = proj(u, o_q, 2 * D, bw=HD, tn=1024, rope=rope, scale_from=D // 1024, name="lat_qk")
    vg = proj(u, o_v, 2 * D, bw=HD, tn=1024, name="lat_vg")
    zx = proj(u, o_z, 2 * SI, bw=GW, tn=1024, name="lat_zx")
    bcl = proj(u, o_b, 2 * NG * NS, bw=NS, tn=1024, name="lat_bc")
    dtl = proj(u, o_dt, 2 * SH, bw=0, tn=2 * SH, out_dtype=f32, name="lat_dt")
    gates = proj(u, o_gate, 2 * D, bw=0, tn=1024, name="lat_gates")

    gret, gssd, xcv, bccv, _, _ = _state_sweep(
        L, fwd=False, emit=True, write_conv=True, k=qk, k_blk=1, v=vg, v_blk=0, xs=zx, xs_blk=1, bsrc=bcl,
        b_blk=0, c_blk=1, dt=dtl, convw=convw, bias=bias, arow=arow, rw=wb_t, rdec=rdec,
        h0r=cbr, h0s=cbs)
    yr, ys = _fwd_sweep(L, qk, vg, zx, xcv, bccv, dtl, gret, gssd, tabs, bias, arow, dexp, snw, rnw, cfr, cfs)

    m = _branch_out(yr, ys, w_ret_out[ly].astype(bf16), w_ssd_out[ly].astype(bf16), gates)
    h1, f = _mix_residual(m, w_o[ly].astype(bf16), xl, g_a, norm2_w[ly][None, :], sc_f, sh_f)
    a = _project(f, w_mlp1, col0=0, ncols=DFF, bw=0, tn=1024, relu2=True, name="mlp1")
    out = _mlp2_final(a, w_mlp2[ly].astype(bf16), h1, g_f, final_norm_w[None, :])
    return out[None]
```

```python
import functools
import math

import numpy as np
import jax
import jax.numpy as jnp
from jax import lax
from jax.experimental import pallas as pl
from jax.experimental.pallas import tpu as pltpu

f32 = jnp.float32
bf16 = jnp.bfloat16

D = 2048
T = 128
GRID_W = 64
HD = 128
RH = D // HD
SI = 2 * D
P = 64
SH = SI // P
NS = 128
NG = 8
R = SH // NG
GW = R * P
KC = 5
DFF = 4 * D
ROPE_BASE = 10000.0
EPS = 1e-6
LANES = 128
MXU_N = 256
MLP2_STRIP = 512
CAST_ROWS = 256
HALO = 16
LOG2E = 1.0 / math.log(2.0)
LOG2_FLOOR = -300.0
ONES_LANE = 96
VMEM_LIMIT = 48 * 1024 * 1024


def _cp(sem):
    return pltpu.CompilerParams(dimension_semantics=sem, vmem_limit_bytes=VMEM_LIMIT)


def _nt(a, b):
    return lax.dot_general(a, b, (((1,), (1,)), ((), ())), preferred_element_type=f32)


def _dot(a, b):
    return jnp.dot(a, b, preferred_element_type=f32)


def _sigmoid(x):
    return 0.5 + 0.5 * jnp.tanh(0.5 * x)


def _silu(x):
    h = 0.5 * x
    return h + h * jnp.tanh(h)


def _softplus(x):
    return jnp.maximum(x, 0.0) + jnp.log1p(jnp.exp(-jnp.abs(x)))


def _log_sigmoid(x):
    return jnp.minimum(x, 0.0) - jnp.log1p(jnp.exp(-jnp.abs(x)))


def _split3(x):
    p1 = x.astype(bf16).astype(f32)
    r1 = x - p1
    p2 = r1.astype(bf16).astype(f32)
    p3 = (r1 - p2).astype(bf16).astype(f32)
    return p1, p2, p3


def _cumsum_rows(la, tri):
    a1, a2, a3 = _split3(la)
    return _dot(tri, a1.astype(bf16)) + _dot(tri, a2.astype(bf16)) + _dot(tri, a3.astype(bf16))


def _selector_constants():
    dsel = np.zeros((R, 2 * SH, 2 * T), np.float32)
    esel = np.zeros((2 * SH, 2 * GW), np.float32)
    for r in range(R):
        for k in range(3):
            dsel[r, k * R + r, :T] = 1.0
            dsel[r, SH + k * R + r, T:] = -1.0
            esel[k * R + r, r * P:(r + 1) * P] = 1.0
            esel[SH + k * R + r, GW + r * P:GW + (r + 1) * P] = 1.0
    return jnp.asarray(dsel, bf16), jnp.asarray(esel, bf16)


def _tri_incl():
    ii = lax.broadcasted_iota(jnp.int32, (T, T), 0)
    jj = lax.broadcasted_iota(jnp.int32, (T, T), 1)
    return jnp.where(jj <= ii, 1.0, 0.0).astype(bf16)


def _rope_tab_kernel(rc_ref, rs_ref, cc_ref, cs_ref):
    def tab(n):
        idx = lax.broadcasted_iota(jnp.int32, (n, LANES), 0).astype(f32)
        lane = lax.broadcasted_iota(jnp.int32, (n, LANES), 1)
        freqs = jnp.exp((lane % (HD // 4)).astype(f32) * (-math.log(ROPE_BASE) / (HD // 4)))
        ang = idx * freqs
        s = jnp.sin(ang)
        return jnp.cos(ang), jnp.where(lane < HD // 2, -s, s)

    rc_ref[...], rs_ref[...] = tab(rc_ref.shape[0])
    cc_ref[...], cs_ref[...] = tab(GRID_W)


def _rope_tables(L):
    rows = L // GRID_W
    shapes = [jax.ShapeDtypeStruct((rows, LANES), f32)] * 2 + [jax.ShapeDtypeStruct((GRID_W, LANES), f32)] * 2
    return pl.pallas_call(_rope_tab_kernel, out_shape=shapes, name="rope_tables")()


def _rope_rows(row_tab, col_tab, tm):
    lane = lax.broadcasted_iota(jnp.int32, (GRID_W, LANES), 1)
    use_row = (lane % (HD // 2)) < HD // 4
    return jnp.concatenate([jnp.where(use_row, row_tab[r:r + 1, :], col_tab) for r in range(tm // GRID_W)], axis=0)


def _ret_tab_kernel(lg_ref, w_ref, ef_ref, eb_ref, wf_ref, wb_ref, dec_ref):
    lf = _log_sigmoid(lg_ref[0])
    lb = _log_sigmoid(lg_ref[1])
    ii = lax.broadcasted_iota(jnp.int32, (T, T), 0)
    jj = lax.broadcasted_iota(jnp.int32, (T, T), 1)
    dl = (ii - jj).astype(f32)
    w_ref[...] = jnp.exp(jnp.where(jj <= ii, dl * lf, -dl * lb))
    idx = lax.broadcasted_iota(jnp.int32, (T, LANES), 0).astype(f32)
    ef_ref[...] = jnp.exp((idx + 1.0) * lf)
    eb_ref[...] = jnp.exp((T - idx) * lb)
    wf_ref[...] = jnp.exp((T - 1.0 - idx) * lf)
    wb_ref[...] = jnp.exp(idx * lb)
    srow = lax.broadcasted_iota(jnp.int32, (8, LANES), 0)
    dec_ref[...] = jnp.where(srow == 0, jnp.exp(T * lf), jnp.exp(T * lb))


def _ret_tables(logit):
    lg = jnp.broadcast_to(logit[:, :, None, None], (2, RH, 1, LANES))
    tab = pl.BlockSpec((None, T, LANES), lambda h: (h, 0, 0))
    return pl.pallas_call(
        _ret_tab_kernel,
        grid=(RH,),
        in_specs=[pl.BlockSpec((2, None, 1, LANES), lambda h: (0, h, 0, 0))],
        out_specs=[tab, tab, tab, tab, tab, pl.BlockSpec((None, 8, LANES), lambda h: (h, 0, 0))],
        out_shape=[jax.ShapeDtypeStruct((RH, T, T), f32)] + [jax.ShapeDtypeStruct((RH, T, LANES), f32)] * 4
        + [jax.ShapeDtypeStruct((RH, 8, LANES), f32)],
        compiler_params=_cp(("parallel",)),
        name="ret_tables",
    )(lg)


def _mod_kernel(a_ref, w_ref, b_ref, o_ref):
    a = _silu(a_ref[...])
    o_ref[...] = _dot(a, w_ref[...]) + b_ref[...]


def _modulation(a8, w, b):
    tn = 1024
    n = w.shape[1]
    return pl.pallas_call(
        _mod_kernel,
        grid=(n // tn,),
        in_specs=[pl.BlockSpec((8, D), lambda j: (0, 0)),
                  pl.BlockSpec((D, tn), lambda j: (0, j)),
                  pl.BlockSpec((1, tn), lambda j: (0, j))],
        out_specs=pl.BlockSpec((8, tn), lambda j: (0, j)),
        out_shape=jax.ShapeDtypeStruct((8, n), f32),
        compiler_params=_cp(("parallel",)),
        name="modulation",
    )(a8, w, b)


def _rms_mod(x, nw, sc, sh):
    y = x * lax.rsqrt(jnp.mean(x * x, axis=-1, keepdims=True) + EPS)
    return (y * nw) * (1.0 + sc) + sh


def _norm_kernel(x_ref, nw_ref, sc_ref, sh_ref, o_ref):
    o_ref[...] = _rms_mod(x_ref[...], nw_ref[...], sc_ref[...], sh_ref[...]).astype(o_ref.dtype)


def _norm_mod(x, nw, sc, sh):
    L = x.shape[0]
    tm = min(L, 1024)
    vec = pl.BlockSpec((1, D), lambda i: (0, 0))
    return pl.pallas_call(
        _norm_kernel,
        grid=(L // tm,),
        in_specs=[pl.BlockSpec((tm, D), lambda i: (i, 0)), vec, vec, vec],
        out_specs=pl.BlockSpec((tm, D), lambda i: (i, 0)),
        out_shape=jax.ShapeDtypeStruct((L, D), bf16),
        compiler_params=_cp(("parallel",)),
        name="norm_mod",
    )(x, nw, sc, sh)


def _proj_kernel(u_ref, w_ref, *rest, bw, tn, rope, scale_from, w32, relu2):
    if w32:
        *rest, wb = rest
        if len(w_ref.shape) == 3:
            w_ref = w_ref.at[0]

        @pl.when(pl.program_id(1) == 0)
        def _():
            lane = lax.broadcasted_iota(jnp.int32, (CAST_ROWS, tn), 1) % HD
            for rb in range(D // CAST_ROWS):
                rows = slice(rb * CAST_ROWS, (rb + 1) * CAST_ROWS)
                w = w_ref[rows, :]
                if rope:
                    qd = HD // 4
                    w = jnp.where((lane >= qd) & (lane < 2 * qd), pltpu.roll(w, tn - qd, 1),
                                  jnp.where((lane >= 2 * qd) & (lane < 3 * qd), pltpu.roll(w, qd, 1), w))
                wb[rows, :] = w.astype(bf16)
    else:
        wb = w_ref
    if rope:
        rc_ref, rs_ref, cc_ref, cs_ref, o_ref = rest
        tm = u_ref.shape[0]
        scale = jnp.where(pl.program_id(0) >= scale_from, HD ** -0.5, 1.0).astype(f32)
        cos = _rope_rows(rc_ref[...], cc_ref[...], tm) * scale
        sin = _rope_rows(rs_ref[...], cs_ref[...], tm) * scale
    else:
        (o_ref,) = rest
    sw = min(tn, MXU_N)
    for nb in range(tn // sw):
        acc = _dot(u_ref[...], wb[:, nb * sw:(nb + 1) * sw])
        if relu2:
            acc = jnp.maximum(acc, 0.0)
            acc = acc * acc
        if rope:
            for hh in range(sw // HD):
                t = acc[:, hh * HD:(hh + 1) * HD]
                o_ref[nb * (sw // HD) + hh] = (t * cos + pltpu.roll(t, HD // 2, 1) * sin).astype(o_ref.dtype)
        elif bw and bw <= sw:
            for hh in range(sw // bw):
                o_ref[nb * (sw // bw) + hh] = acc[:, hh * bw:(hh + 1) * bw].astype(o_ref.dtype)
        elif bw:
            per = bw // sw
            o_ref[nb // per, :, (nb % per) * sw:(nb % per + 1) * sw] = acc.astype(o_ref.dtype)
        else:
            o_ref[:, nb * sw:(nb + 1) * sw] = acc.astype(o_ref.dtype)


def _project(u, w, *, col0, ncols, bw, tn, out_dtype=bf16, rope=None, scale_from=0, relu2=False, name):
    M = u.shape[0]
    N = ncols
    assert col0 % LANES == 0 and ncols % tn == 0
    jb = col0 // tn
    w32 = w.dtype == f32
    tm = min(M, 1024 if w32 else 2048)
    if w32 and col0 % tn:
        w_spec = pl.BlockSpec((pl.Element(1), pl.Element(D), pl.Element(tn)),
                              lambda j, i: (0, 0, pl.multiple_of(col0 + j * tn, LANES)))
    elif w32:
        w_spec = pl.BlockSpec((None, D, tn), lambda j, i: (0, 0, jb + j))
    else:
        assert col0 % tn == 0
        w_spec = pl.BlockSpec((D, tn), lambda j, i: (0, jb + j))
    in_specs = [pl.BlockSpec((tm, D), lambda j, i: (i, 0)), w_spec]
    args = [u, w]
    if rope is not None:
        in_specs += [pl.BlockSpec((tm // GRID_W, LANES), lambda j, i: (i, 0))] * 2
        in_specs += [pl.BlockSpec((GRID_W, LANES), lambda j, i: (0, 0))] * 2
        args += list(rope)
    if bw:
        out_spec = pl.BlockSpec((tn // bw, tm, bw), lambda j, i: (j, i, 0))
        out_shape = jax.ShapeDtypeStruct((N // bw, M, bw), out_dtype)
    else:
        out_spec = pl.BlockSpec((tm, tn), lambda j, i: (i, j))
        out_shape = jax.ShapeDtypeStruct((M, N), out_dtype)
    return pl.pallas_call(
        functools.partial(_proj_kernel, bw=bw, tn=tn, rope=rope is not None, scale_from=scale_from, w32=w32,
                          relu2=relu2),
        grid=(N // tn, M // tm),
        in_specs=in_specs,
        out_specs=out_spec,
        out_shape=out_shape,
        scratch_shapes=[pltpu.VMEM((D, tn), bf16)] if w32 else [],
        compiler_params=_cp(("parallel", "arbitrary")),
        name=name,
    )(*args)


def _shift_matrix():
    rows = lax.broadcasted_iota(jnp.int32, (4 * T, T + 2 * HALO), 0)
    cols = lax.broadcasted_iota(jnp.int32, (4 * T, T + 2 * HALO), 1)
    blk = rows // T
    tap = jnp.where(blk < KC // 2, blk, blk + 1)
    return jnp.where(cols == (rows - blk * T) + HALO + tap - KC // 2, 1.0, 0.0).astype(bf16)


def _conv_shift(shift, main, prev, nxt, has_prev, has_next):
    zero = jnp.zeros_like(prev)
    ext = jnp.concatenate([jnp.where(has_prev, prev, zero), main, jnp.where(has_next, nxt, zero)], axis=0)
    sh = _dot(shift, ext)
    return [sh[0:T], sh[T:2 * T], main.astype(f32), sh[2 * T:3 * T], sh[3 * T:4 * T]]


def _conv_taps(taps, cw, cb):
    acc = cb
    for j in range(KC):
        acc = acc + taps[j] * cw[j:j + 1, :]
    return _silu(acc)


def _decay_prologue(dt_ref, bias_ref, arow_ref):
    dt = _softplus(dt_ref[...] + bias_ref[...])
    la = dt * arow_ref[...]
    acs = _cumsum_rows(la, _tri_incl())
    tot = acs[T - 1:T, :]
    return dt, la, acs, tot


def _transpose_blocks(x):
    n = x.shape[1] // LANES
    return jnp.concatenate([x[:, b * LANES:(b + 1) * LANES].T for b in range(n)], axis=0)


def _ssd_state_update(hs_ref, g, xc, bc16, w_rows, dec_rows):
    xT = _transpose_blocks(xc)
    lhs = jnp.concatenate(
        [(xT[r * P:(r + 1) * P, :] * w_rows[r:r + 1, :]).astype(bf16) for r in range(R)], axis=0)
    upd = _dot(lhs, bc16)
    old = hs_ref[g]
    hs_ref[g] = jnp.concatenate(
        [old[r * P:(r + 1) * P, :] * dec_rows[r:r + 1, :] + upd[r * P:(r + 1) * P, :] for r in range(R)], axis=0)


def _ret_state_update(hr_ref, h, v16, k16, wcol, dec_row):
    vw = (v16.astype(f32) * wcol).T.astype(bf16)
    hr_ref[h] = hr_ref[h] * dec_row + _dot(vw, k16)


def _state_kernel(*refs, nc, fwd, emit, write_conv):
    it = iter(refs)
    k_ref, v_ref = next(it), next(it)
    xs_ref, xs_p, xs_n = next(it), next(it), next(it)
    b_ref, b_p, b_n = next(it), next(it), next(it)
    if write_conv:
        c_ref, c_p, c_n = next(it), next(it), next(it)
    dt_ref = next(it)
    cwx, cbx, cwb, cbb = next(it), next(it), next(it), next(it)
    if write_conv:
        cwc, cbc = next(it), next(it)
    bias_ref, arow_ref = next(it), next(it)
    rw_ref, rdec_ref = next(it), next(it)
    h0r_ref, h0s_ref = next(it), next(it)
    if emit:
        gr_ref, gs_ref = next(it), next(it)
    if write_conv:
        xo_ref, bco_ref = next(it), next(it)
    hfr_ref, hfs_ref = next(it), next(it)
    hr, hs, wT_s, decT_s = next(it), next(it), next(it), next(it)

    s = pl.program_id(0)
    c = s if fwd else nc - 1 - s
    has_prev = c > 0
    has_next = c < nc - 1

    @pl.when(s == 0)
    def _():
        hr[...] = h0r_ref[...]
        hs[...] = h0s_ref[...]

    if emit:
        gr_ref[0] = hr[...].astype(bf16)
        gs_ref[0] = hs[...].astype(bf16)

    dt, la, acs, tot = _decay_prologue(dt_ref, bias_ref, arow_ref)
    wexp = jnp.exp(tot - acs) if fwd else jnp.exp(acs - la)
    wT_s[...] = (wexp * dt).T
    decT_s[...] = jnp.broadcast_to(jnp.exp(tot), (T, 2 * SH)).T
    off = 0 if fwd else SH

    shift = _shift_matrix()

    def cat(a, b):
        return jnp.concatenate([a, b], axis=1)

    def shifted(g):
        tx = _conv_shift(shift, xs_ref[g], xs_p[g], xs_n[g], has_prev, has_next)
        if write_conv:
            tb = _conv_shift(shift, cat(b_ref[g], c_ref[g]), cat(b_p[g], c_p[g]), cat(b_n[g], c_n[g]),
                             has_prev, has_next)
        else:
            tb = _conv_shift(shift, b_ref[g], b_p[g], b_n[g], has_prev, has_next)
        return tx, tb

    nxt_taps = shifted(0)
    for g in range(NG):
        tx, tb = nxt_taps
        if g + 1 < NG:
            nxt_taps = shifted(g + 1)
        xc16 = _conv_taps(tx, cwx[g], cbx[g]).astype(bf16)
        if write_conv:
            bcc = _conv_taps(tb, cat(cwb[g], cwc[g]), cat(cbb[g], cbc[g])).astype(bf16)
            bc16 = bcc[:, :NS]
            xo_ref[g] = xc16
            bco_ref[g] = bcc
        else:
            bc16 = _conv_taps(tb, cwb[g], cbb[g]).astype(bf16)
        r0 = off + g * R
        _ssd_state_update(hs, g, xc16.astype(f32), bc16, wT_s[r0:r0 + R, :], decT_s[r0:r0 + R, :])

    dsel = 0 if fwd else 1

    for h in range(RH):
        _ret_state_update(hr, h, v_ref[h], k_ref[h], rw_ref[h], rdec_ref[h][dsel:dsel + 1, :])

    @pl.when(s == nc - 1)
    def _():
        hfr_ref[...] = hr[...]
        hfs_ref[...] = hs[...]


def _state_sweep(L, *, fwd, emit, write_conv, k, k_blk, v, v_blk, xs, xs_blk, bsrc, b_blk, c_blk, dt,
                 convw, bias, arow, rw, rdec, h0r, h0s):
    nc = L // T
    tb = T // HALO
    nrb = L // HALO

    def cidx(s):
        return s if fwd else nc - 1 - s

    def main(lead, n, width):
        return pl.BlockSpec((n, T, width), lambda s: (lead, cidx(s), 0))

    def prev(lead, n, width):
        return pl.BlockSpec((n, HALO, width), lambda s: (lead, jnp.maximum(cidx(s) * tb - 1, 0), 0))

    def nxt(lead, n, width):
        return pl.BlockSpec((n, HALO, width), lambda s: (lead, jnp.minimum((cidx(s) + 1) * tb, nrb - 1), 0))

    def whole(a):
        nd = a.ndim
        return pl.BlockSpec(a.shape, lambda s: (0,) * nd)

    cwx, cbx, cwb, cbb, cwc, cbc = convw
    in_specs = [main(k_blk, RH, HD), main(v_blk, RH, HD),
                main(xs_blk, NG, GW), prev(xs_blk, NG, GW), nxt(xs_blk, NG, GW),
                main(b_blk, NG, NS), prev(b_blk, NG, NS), nxt(b_blk, NG, NS)]
    args = [k, v, xs, xs, xs, bsrc, bsrc, bsrc]
    if write_conv:
        in_specs += [main(c_blk, NG, NS), prev(c_blk, NG, NS), nxt(c_blk, NG, NS)]
        args += [bsrc, bsrc, bsrc]
    in_specs += [pl.BlockSpec((T, 2 * SH), lambda s: (cidx(s), 0))]
    args += [dt]
    small = [cwx, cbx, cwb, cbb] + ([cwc, cbc] if write_conv else []) + [bias, arow, rw, rdec, h0r, h0s]
    in_specs += [whole(a) for a in small]
    args += small

    out_specs, out_shape = [], []
    if emit:
        out_specs += [pl.BlockSpec((1, RH, HD, HD), lambda s: (cidx(s), 0, 0, 0)),
                      pl.BlockSpec((1, NG, GW, NS), lambda s: (cidx(s), 0, 0, 0))]
        out_shape += [jax.ShapeDtypeStruct((nc, RH, HD, HD), bf16), jax.ShapeDtypeStruct((nc, NG, GW, NS), bf16)]
    if write_conv:
        out_specs += [pl.BlockSpec((NG, T, GW), lambda s: (0, cidx(s), 0)),
                      pl.BlockSpec((NG, T, 2 * NS), lambda s: (0, cidx(s), 0))]
        out_shape += [jax.ShapeDtypeStruct((NG, L, GW), bf16), jax.ShapeDtypeStruct((NG, L, 2 * NS), bf16)]
    out_specs += [pl.BlockSpec((RH, HD, HD), lambda s: (0, 0, 0)), pl.BlockSpec((NG, GW, NS), lambda s: (0, 0, 0))]
    out_shape += [jax.ShapeDtypeStruct((RH, HD, HD), f32), jax.ShapeDtypeStruct((NG, GW, NS), f32)]

    return pl.pallas_call(
        functools.partial(_state_kernel, nc=nc, fwd=fwd, emit=emit, write_conv=write_conv),
        grid=(nc,),
        in_specs=in_specs,
        out_specs=out_specs,
        out_shape=out_shape,
        scratch_shapes=[pltpu.VMEM((RH, HD, HD), f32), pltpu.VMEM((NG, GW, NS), f32),
                        pltpu.VMEM((2 * SH, T), f32), pltpu.VMEM((2 * SH, T), f32)],
        compiler_params=_cp(("arbitrary",)),
        name="state_sweep_" + ("f" if fwd else "b") + ("_emit" if emit else ""),
    )(*args)


def _fwd_kernel(qk_ref, vg_ref, z_ref, xs_ref, bc_ref, dt_ref, gret_ref, gssd_ref,
                wret_ref, ef_ref, eb_ref, wf_ref, rdec_ref, bias_ref, arow_ref, dexp_ref, snw_ref, rnw_ref,
                h0r_ref, h0s_ref, dsel_ref, esel_ref, yr_ref, ys_ref,
                hr, hs, ypre, ap_s, ep_s, apT_s, wT_s, decT_s):
    s = pl.program_id(0)

    @pl.when(s == 0)
    def _():
        hr[...] = h0r_ref[...]
        hs[...] = h0s_ref[...]

    dt, la, acs, tot = _decay_prologue(dt_ref, bias_ref, arow_ref)
    lane = lax.broadcasted_iota(jnp.int32, (T, 2 * SH), 1)
    is_f = lane < SH
    a1 = jnp.where(is_f, acs, acs - la)
    e1 = jnp.exp(jnp.where(is_f, acs, tot - (acs - la)))
    a2 = a1 * LOG2E
    ldt = jnp.maximum(jnp.log2(dt), LOG2_FLOOR)
    for k, part in enumerate(_split3(a2)):
        ap_s[k] = part
    for k, part in enumerate(_split3(jnp.where(is_f, ldt - a2, ldt + a2))):
        apT_s[k] = part.T
    for k, part in enumerate(_split3(e1)):
        ep_s[k] = part
    wT_s[...] = (jnp.exp(tot - acs) * dt).T
    decT_s[...] = jnp.broadcast_to(jnp.exp(tot), (T, 2 * SH)).T

    ii = lax.broadcasted_iota(jnp.int32, (T, T), 0)
    jj = lax.broadcasted_iota(jnp.int32, (T, T), 1)
    lower = jj <= ii
    part_masks = [((lane >= k * R) & (lane < (k + 1) * R)) | ((lane >= SH + k * R) & (lane < SH + (k + 1) * R))
                  for k in range(3)]
    ones_lanes = (lane >= ONES_LANE) & (lane < ONES_LANE + 3)
    row16 = lax.broadcasted_iota(jnp.int32, (16, T), 0)
    lane_p = lax.broadcasted_iota(jnp.int32, (T, 2 * P), 1)

    def pack(parts_ref, g, with_ones):
        acc = jnp.where(ones_lanes, 1.0, 0.0) if with_ones else jnp.zeros((T, 2 * SH), f32)
        for k in range(3):
            shift = (2 * SH - g * R + k * R) % (2 * SH)
            acc = acc + jnp.where(part_masks[k], pltpu.roll(parts_ref[k], shift, 1), 0.0)
        return acc.astype(bf16)

    def front(g):
        b16 = bc_ref[g, :, 0:NS]
        c16 = bc_ref[g, :, NS:2 * NS]
        sc = _nt(c16, b16)
        r0 = g * R
        lhs_a = pack(ap_s, g, True)
        lhs_e = pack(ep_s, g, False)
        cf_rows = [apT_s[k, r0:r0 + R, :] for k in range(3)]
        cb_rows = [apT_s[k, SH + r0:SH + r0 + R, :] for k in range(3)]
        hcat = jnp.concatenate([hs[g].astype(bf16), gssd_ref[0, g]], axis=0)
        ycross = _nt(c16, hcat)
        efb = _dot(lhs_e, esel_ref[...])
        dmats = []
        for r in range(R):
            lo = jnp.zeros((16, T), f32)
            up = jnp.zeros((16, T), f32)
            for k in range(3):
                lo = jnp.where(row16 == k, cf_rows[k][r:r + 1, :], lo)
                up = jnp.where(row16 == k, cb_rows[k][r:r + 1, :], up)
            dyn = jnp.concatenate([lo, up], axis=1).astype(bf16)
            rhs = jnp.concatenate([dsel_ref[r, 0:ONES_LANE, :], dyn, dsel_ref[r, ONES_LANE + 16:, :]], axis=0)
            dmats.append(_dot(lhs_a, rhs))
        return b16, sc, ycross, efb, dmats

    def back(g, staged, ssq):
        b16, sc, ycross, efb, dmats = staged
        r0 = g * R
        xg = xs_ref[g]
        zg = z_ref[g].astype(f32)
        dg = dexp_ref[g]
        ms = [(sc * jnp.exp2(jnp.where(lower, dm[:, :T], dm[:, T:]))).astype(bf16) for dm in dmats]
        ys_parts = []
        for t in range(R // 2):
            xp = xg[:, t * 2 * P:(t + 1) * 2 * P]
            zero = jnp.zeros_like(xp)
            rhs = jnp.concatenate([jnp.where(lane_p < P, xp, zero), jnp.where(lane_p >= P, xp, zero)], axis=0)
            ys_parts.append(_dot(jnp.concatenate([ms[2 * t], ms[2 * t + 1]], axis=1), rhs))
        yg = jnp.concatenate(ys_parts, axis=1)
        yg = yg + ycross[:, :GW] * efb[:, :GW] + ycross[:, GW:] * efb[:, GW:]
        yg = (yg + xg.astype(f32) * dg) * _silu(zg)
        ypre[g] = yg
        ssq = ssq + jnp.sum(yg * yg, axis=1, keepdims=True)
        _ssd_state_update(hs, g, xg.astype(f32), b16, wT_s[r0:r0 + R, :], decT_s[r0:r0 + R, :])
        return ssq

    ssq = jnp.zeros((T, 1), f32)
    staged = front(0)
    for g in range(NG):
        cur = staged
        if g + 1 < NG:
            staged = front(g + 1)
        ssq = back(g, cur, ssq)
    rs = lax.rsqrt(ssq * (1.0 / SI) + EPS)
    for g in range(NG):
        ys_ref[:, g * GW:(g + 1) * GW] = ((ypre[g] * rs) * snw_ref[g]).astype(ys_ref.dtype)

    zero_h = jnp.zeros((T, HD), bf16)

    def blockdiag(a, b):
        return jnp.concatenate([jnp.concatenate([a, zero_h], axis=1), jnp.concatenate([zero_h, b], axis=1)], axis=0)

    def ret_front(t):
        pair = (2 * t, 2 * t + 1)
        q16 = [qk_ref[h] for h in pair]
        k16 = [qk_ref[RH + h] for h in pair]
        s2 = _nt(jnp.concatenate(q16, axis=1), blockdiag(*k16))
        cross = []
        for n, h in enumerate(pair):
            qf = q16[n].astype(f32)
            lc = jnp.concatenate([(qf * ef_ref[h]).astype(bf16), (qf * eb_ref[h]).astype(bf16)], axis=1)
            hcat = jnp.concatenate([hr[h].astype(bf16), gret_ref[0, h]], axis=1)
            cross.append(_nt(lc, hcat))
        return k16, s2, cross

    ret_staged = ret_front(0)
    for t in range(RH // 2):
        pair = (2 * t, 2 * t + 1)
        k16, s2, cross = ret_staged
        if t + 1 < RH // 2:
            ret_staged = ret_front(t + 1)
        v16 = [vg_ref[h] for h in pair]
        m2 = (s2 * jnp.concatenate([wret_ref[h] for h in pair], axis=1)).astype(bf16)
        y2 = _dot(m2, blockdiag(*v16))
        for n, h in enumerate(pair):
            y = y2[:, n * HD:(n + 1) * HD] + cross[n]
            mu = jnp.mean(y, axis=-1, keepdims=True)
            d = y - mu
            yn = d * lax.rsqrt(jnp.mean(d * d, axis=-1, keepdims=True) + EPS)
            gg = vg_ref[RH + h].astype(f32)
            yr_ref[:, h * HD:(h + 1) * HD] = ((yn * rnw_ref[h]) * _silu(gg)).astype(yr_ref.dtype)
            _ret_state_update(hr, h, v16[n], k16[n], wf_ref[h], rdec_ref[h][0:1, :])


def _fwd_sweep(L, qk, vg, zx, xc, bcc, dt, gret, gssd, tabs, bias, arow, dexp, snw, rnw, h0r, h0s):
    nc = L // T
    wret, ef, eb, wf, _, rdec = tabs

    def blk(lead, n, width):
        return pl.BlockSpec((n, T, width), lambda s: (lead, s, 0))

    def whole(a):
        nd = a.ndim
        return pl.BlockSpec(a.shape, lambda s: (0,) * nd)

    dsel, esel = _selector_constants()
    small = [wret, ef, eb, wf, rdec, bias, arow, dexp, snw, rnw, h0r, h0s, dsel, esel]
    in_specs = [blk(0, 2 * RH, HD), blk(0, 2 * RH, HD), blk(0, NG, GW), blk(0, NG, GW), blk(0, NG, 2 * NS),
                pl.BlockSpec((T, 2 * SH), lambda s: (s, 0)),
                pl.BlockSpec((1, RH, HD, HD), lambda s: (s, 0, 0, 0)),
                pl.BlockSpec((1, NG, GW, NS), lambda s: (s, 0, 0, 0))] + [whole(a) for a in small]
    return pl.pallas_call(
        _fwd_kernel,
        grid=(nc,),
        in_specs=in_specs,
        out_specs=[pl.BlockSpec((T, D), lambda s: (s, 0)), pl.BlockSpec((T, SI), lambda s: (s, 0))],
        out_shape=[jax.ShapeDtypeStruct((L, D), bf16), jax.ShapeDtypeStruct((L, SI), bf16)],
        scratch_shapes=[pltpu.VMEM((RH, HD, HD), f32), pltpu.VMEM((NG, GW, NS), f32), pltpu.VMEM((NG, T, GW), f32),
                        pltpu.VMEM((3, T, 2 * SH), f32), pltpu.VMEM((3, T, 2 * SH), f32),
                        pltpu.VMEM((3, 2 * SH, T), f32)]
        + [pltpu.VMEM((2 * SH, T), f32)] * 2,
        compiler_params=_cp(("arbitrary",)),
        name="fwd_sweep",
    )(qk, vg, zx, xc, bcc, dt, gret, gssd, *small)


def _branch_out_kernel(yr_ref, ys_ref, wr_ref, ws_ref, gr_ref, gs_ref, o_ref):
    for nb in range(o_ref.shape[1] // MXU_N):
        cols = slice(nb * MXU_N, (nb + 1) * MXU_N)
        acc_r = _dot(yr_ref[...], wr_ref[:, cols])
        acc_s = _dot(ys_ref[...], ws_ref[:, cols])
        m = _sigmoid(gr_ref[:, cols].astype(f32)) * acc_r + _sigmoid(gs_ref[:, cols].astype(f32)) * acc_s
        o_ref[:, cols] = m.astype(o_ref.dtype)


def _branch_out(yr, ys, wr, ws, gates):
    L = yr.shape[0]
    tm, tn = min(L, 1024), 512
    nj = D // tn
    return pl.pallas_call(
        _branch_out_kernel,
        grid=(L // tm, nj),
        in_specs=[pl.BlockSpec((tm, D), lambda i, j: (i, 0)),
                  pl.BlockSpec((tm, SI), lambda i, j: (i, 0)),
                  pl.BlockSpec((D, tn), lambda i, j: (0, j)),
                  pl.BlockSpec((SI, tn), lambda i, j: (0, j)),
                  pl.BlockSpec((tm, tn), lambda i, j: (i, j)),
                  pl.BlockSpec((tm, tn), lambda i, j: (i, nj + j))],
        out_specs=pl.BlockSpec((tm, tn), lambda i, j: (i, j)),
        out_shape=jax.ShapeDtypeStruct((L, D), bf16),
        compiler_params=_cp(("parallel", "parallel")),
        name="branch_out",
    )(yr, ys, wr, ws, gates, gates)


def _resid_kernel(m_ref, w_ref, x_ref, g_ref, nw_ref, sc_ref, sh_ref, h_ref, f_ref):
    h = x_ref[...] + g_ref[...] * _dot(m_ref[...], w_ref[...])
    h_ref[...] = h
    f_ref[...] = _rms_mod(h, nw_ref[...], sc_ref[...], sh_ref[...]).astype(f_ref.dtype)


def _mix_residual(m, w, x, gate, nw, sc, sh):
    L = m.shape[0]
    tm = min(L, 512)
    vec = pl.BlockSpec((1, D), lambda i: (0, 0))
    row = pl.BlockSpec((tm, D), lambda i: (i, 0))
    return pl.pallas_call(
        _resid_kernel,
        grid=(L // tm,),
        in_specs=[row, pl.BlockSpec((D, D), lambda i: (0, 0)), row, vec, vec, vec, vec],
        out_specs=[row, row],
        out_shape=[jax.ShapeDtypeStruct((L, D), f32), jax.ShapeDtypeStruct((L, D), bf16)],
        compiler_params=_cp(("parallel",)),
        name="mix_residual",
    )(m, w, x, gate, nw, sc, sh)


def _mlp2_kernel(a_ref, w_ref, h_ref, g_ref, fw_ref, o_ref, *, nk):
    kk = pl.program_id(1)

    @pl.when(kk == 0)
    def _():
        o_ref[...] = jnp.zeros_like(o_ref)

    for nb in range(D // MLP2_STRIP):
        cols = slice(nb * MLP2_STRIP, (nb + 1) * MLP2_STRIP)
        o_ref[:, cols] += _dot(a_ref[...], w_ref[:, cols])

    @pl.when(kk == nk - 1)
    def _():
        h2 = h_ref[...] + g_ref[...] * o_ref[...]
        y = h2 * lax.rsqrt(jnp.mean(h2 * h2, axis=-1, keepdims=True) + EPS)
        o_ref[...] = y * fw_ref[...]


def _mlp2_final(a, w, h, gate, fw):
    L = a.shape[0]
    tm, tk = min(L, 512), 2048
    nk = DFF // tk
    vec = pl.BlockSpec((1, D), lambda i, k: (0, 0))
    return pl.pallas_call(
        functools.partial(_mlp2_kernel, nk=nk),
        grid=(L // tm, nk),
        in_specs=[pl.BlockSpec((tm, tk), lambda i, k: (i, k)),
                  pl.BlockSpec((tk, D), lambda i, k: (k, 0)),
                  pl.BlockSpec((tm, D), lambda i, k: (i, 0)), vec, vec],
        out_specs=pl.BlockSpec((tm, D), lambda i, k: (i, 0)),
        out_shape=jax.ShapeDtypeStruct((L, D), f32),
        compiler_params=_cp(("parallel", "arbitrary")),
        name="mlp2_final",
    )(a, w, h, gate, fw)


def _split_conv(conv_w, conv_b):
    def grp(w, b, width):
        return (jnp.transpose(w.reshape(KC, NG, width), (1, 0, 2)), b.reshape(NG, 1, width))
    cwx, cbx = grp(conv_w[:, :SI], conv_b[:SI], GW)
    cwb, cbb = grp(conv_w[:, SI:SI + NG * NS], conv_b[SI:SI + NG * NS], NS)
    cwc, cbc = grp(conv_w[:, SI + NG * NS:], conv_b[SI + NG * NS:], NS)
    return cwx, cbx, cwb, cbb, cwc, cbc


def kernel(x, c, ctx, c_ctx, w_mod, b_mod, norm1_w, w_in, conv_w, conv_b, ret_decay_logit, ret_norm_w,
           ssd_a_log, ssd_dt_bias, ssd_d, ssd_norm_w, w_ret_out, w_ssd_out, w_o, norm2_w, w_mlp1, w_mlp2,
           final_norm_w):
    depth = w_mod.shape[0]
    assert depth == 1 and x.shape[0] == 1 and x.shape[2] == D
    L = x.shape[1]
    Lc = ctx.shape[1]
    assert L % T == 0 and Lc % T == 0 and L % GRID_W == 0
    xl = x[0]
    xcx = ctx[0]
    ly = 0

    a8 = jnp.zeros((8, D), f32).at[0].set(c[0]).at[1].set(c_ctx)
    mod = _modulation(a8, w_mod[ly], b_mod[ly][None, :])
    sh_a, sc_a, g_a, sh_f, sc_f, g_f = [mod[0:1, i * D:(i + 1) * D] for i in range(6)]
    csh_a, csc_a = mod[1:2, 0:D], mod[1:2, D:2 * D]

    wi = w_in[ly]
    o_q, o_k, o_v, o_z, o_x = 0, D, 2 * D, 4 * D, 4 * D + SI
    o_b = o_x + SI
    o_dt = o_b + 2 * NG * NS
    o_gate = o_dt + 2 * SH
    def proj(src, col, n, **kw):
        return _project(src, w_in, col0=col, ncols=n, **kw)

    convw = _split_conv(conv_w[ly], conv_b[ly])
    bias = ssd_dt_bias[ly].reshape(1, 2 * SH)
    arow = (-jnp.exp(ssd_a_log[ly])).reshape(1, 2 * SH)
    dexp = jnp.repeat(ssd_d[ly], P).reshape(NG, 1, GW)
    snw = ssd_norm_w[ly].reshape(NG, 1, GW)
    rnw = ret_norm_w[ly].reshape(RH, 1, HD)
    nw1 = norm1_w[ly][None, :]

    tabs = _ret_tables(ret_decay_logit[ly])
    _, _, _, wf_t, wb_t, rdec = tabs
    zr = jnp.zeros((RH, HD, HD), f32)
    zs = jnp.zeros((NG, GW, NS), f32)

    uc = _norm_mod(xcx, nw1, csc_a, csh_a)
    ones_c, zeros_c = jnp.ones((Lc // GRID_W, LANES), f32), jnp.zeros((Lc // GRID_W, LANES), f32)
    ident = (ones_c, zeros_c, jnp.ones((GRID_W, LANES), f32), jnp.zeros((GRID_W, LANES), f32))
    kc = proj(uc, o_k, D, bw=HD, tn=512, rope=ident, scale_from=0, name="ctx_k")
    vc = proj(uc, o_v, D, bw=HD, tn=512, name="ctx_v")
    bcx = proj(uc, o_b, NG * NS, bw=NS, tn=512, name="ctx_b")
    xsc = proj(uc, o_x, SI, bw=GW, tn=GW, name="ctx_x")
    dtc = proj(uc, o_dt, 2 * SH, bw=0, tn=2 * SH, out_dtype=f32, name="ctx_dt")
    common = dict(k=kc, k_blk=0, v=vc, v_blk=0, xs=xsc, xs_blk=0, bsrc=bcx, b_blk=0, c_blk=0, dt=dtc,
                  convw=convw, bias=bias, arow=arow, rdec=rdec, h0r=zr, h0s=zs)
    cfr, cfs = _state_sweep(Lc, fwd=True, emit=False, write_conv=False, rw=wf_t, **common)
    cbr, cbs = _state_sweep(Lc, fwd=False, emit=False, write_conv=False, rw=wb_t, **common)

    u = _norm_mod(xl, nw1, sc_a, sh_a)
    rope = _rope_tables(L)
    qk = proj(u, o_q, 2 * D, bw=HD, tn=1024, rope=rope, scale_from=D // 1024, name="lat_qk")
    vg = proj(u, o_v, 2 * D, bw=HD, tn=1024, name="lat_vg")
    zx = proj(u, o_z, 2 * SI, bw=GW, tn=1024, name="lat_zx")
    bcl = proj(u, o_b, 2 * NG * NS, bw=NS, tn=1024, name="lat_bc")
    dtl = proj(u, o_dt, 2 * SH, bw=0, tn=2 * SH, out_dtype=f32, name="lat_dt")
    gates = proj(u, o_gate, 2 * D, bw=0, tn=1024, name="lat_gates")

    gret, gssd, xcv, bccv, _, _ = _state_sweep(
        L, fwd=False, emit=True, write_conv=True, k=qk, k_blk=1, v=vg, v_blk=0, xs=zx, xs_blk=1, bsrc=bcl,
        b_blk=0, c_blk=1, dt=dtl, convw=convw, bias=bias, arow=arow, rw=wb_t, rdec=rdec,
        h0r=cbr, h0s=cbs)
    yr, ys = _fwd_sweep(L, qk, vg, zx, xcv, bccv, dtl, gret, gssd, tabs, bias, arow, dexp, snw, rnw, cfr, cfs)

    m = _branch_out(yr, ys, w_ret_out[ly].astype(bf16), w_ssd_out[ly].astype(bf16), gates)
    h1, f = _mix_residual(m, w_o[ly].astype(bf16), xl, g_a, norm2_w[ly][None, :], sc_f, sh_f)
    a = _project(f, w_mlp1, col0=0, ncols=DFF, bw=0, tn=1024, relu2=True, name="mlp1")
    out = _mlp2_final(a, w_mlp2[ly].astype(bf16), h1, g_f, final_norm_w[None, :])
    return out[None]
```

```python
import functools
import math

import numpy as np
import jax
import jax.numpy as jnp
from jax import lax
from jax.experimental import pallas as pl
from jax.experimental.pallas import tpu as pltpu

f32 = jnp.float32
bf16 = jnp.bfloat16

D = 2048
T = 128
GRID_W = 64
HD = 128
RH = D // HD
SI = 2 * D
P = 64
SH = SI // P
NS = 128
NG = 8
R = SH // NG
GW = R * P
KC = 5
DFF = 4 * D
ROPE_BASE = 10000.0
EPS = 1e-6
LANES = 128
MXU_N = 256
CAST_ROWS = 256
HALO = 16
LOG2E = 1.0 / math.log(2.0)
LOG2_FLOOR = -300.0
ONES_LANE = 96
VMEM_LIMIT = 48 * 1024 * 1024

PROJ_TM, PROJ_TN = 1024, 1024
CTX_TN = 512
NORM_TM = 1024
MOD_TN = 1024
BRANCH_TM, BRANCH_TN = 1024, 512
MIX_TM = 512
MLP2_TM, MLP2_TK, MLP2_STRIP = 512, 2048, 512


def _cp(sem):
    return pltpu.CompilerParams(dimension_semantics=sem, vmem_limit_bytes=VMEM_LIMIT)


def _nt(a, b):
    return lax.dot_general(a, b, (((1,), (1,)), ((), ())), preferred_element_type=f32)


def _dot(a, b):
    return jnp.dot(a, b, preferred_element_type=f32)


def _sigmoid(x):
    return 0.5 + 0.5 * jnp.tanh(0.5 * x)


def _silu(x):
    h = 0.5 * x
    return h + h * jnp.tanh(h)


def _softplus(x):
    return jnp.maximum(x, 0.0) + jnp.log1p(jnp.exp(-jnp.abs(x)))


def _log_sigmoid(x):
    return jnp.minimum(x, 0.0) - jnp.log1p(jnp.exp(-jnp.abs(x)))


def _split3(x):
    p1 = x.astype(bf16).astype(f32)
    r1 = x - p1
    p2 = r1.astype(bf16).astype(f32)
    p3 = (r1 - p2).astype(bf16).astype(f32)
    return p1, p2, p3


def _cumsum_rows(la, tri):
    a1, a2, a3 = _split3(la)
    return _dot(tri, a1.astype(bf16)) + _dot(tri, a2.astype(bf16)) + _dot(tri, a3.astype(bf16))


def _selector_constants():
    dsel = np.zeros((R, 2 * SH, 2 * T), np.float32)
    esel = np.zeros((2 * SH, 2 * GW), np.float32)
    for r in range(R):
        for k in range(3):
            dsel[r, k * R + r, :T] = 1.0
            dsel[r, SH + k * R + r, T:] = -1.0
            esel[k * R + r, r * P:(r + 1) * P] = 1.0
            esel[SH + k * R + r, GW + r * P:GW + (r + 1) * P] = 1.0
    return jnp.asarray(dsel, bf16), jnp.asarray(esel, bf16)


def _tri_incl():
    ii = lax.broadcasted_iota(jnp.int32, (T, T), 0)
    jj = lax.broadcasted_iota(jnp.int32, (T, T), 1)
    return jnp.where(jj <= ii, 1.0, 0.0).astype(bf16)


def _rope_tab_kernel(rc_ref, rs_ref, cc_ref, cs_ref):
    def tab(n):
        idx = lax.broadcasted_iota(jnp.int32, (n, LANES), 0).astype(f32)
        lane = lax.broadcasted_iota(jnp.int32, (n, LANES), 1)
        freqs = jnp.exp((lane % (HD // 4)).astype(f32) * (-math.log(ROPE_BASE) / (HD // 4)))
        ang = idx * freqs
        s = jnp.sin(ang)
        return jnp.cos(ang), jnp.where(lane < HD // 2, -s, s)

    rc_ref[...], rs_ref[...] = tab(rc_ref.shape[0])
    cc_ref[...], cs_ref[...] = tab(GRID_W)


def _rope_tables(L):
    rows = L // GRID_W
    shapes = [jax.ShapeDtypeStruct((rows, LANES), f32)] * 2 + [jax.ShapeDtypeStruct((GRID_W, LANES), f32)] * 2
    return pl.pallas_call(_rope_tab_kernel, out_shape=shapes, name="rope_tables")()


def _rope_rows(row_tab, col_tab, tm):
    lane = lax.broadcasted_iota(jnp.int32, (GRID_W, LANES), 1)
    use_row = (lane % (HD // 2)) < HD // 4
    return jnp.concatenate([jnp.where(use_row, row_tab[r:r + 1, :], col_tab) for r in range(tm // GRID_W)], axis=0)


def _ret_tab_kernel(lg_ref, w_ref, ef_ref, eb_ref, wf_ref, wb_ref, dec_ref):
    lf = _log_sigmoid(lg_ref[0])
    lb = _log_sigmoid(lg_ref[1])
    ii = lax.broadcasted_iota(jnp.int32, (T, T), 0)
    jj = lax.broadcasted_iota(jnp.int32, (T, T), 1)
    dl = (ii - jj).astype(f32)
    w_ref[...] = jnp.exp(jnp.where(jj <= ii, dl * lf, -dl * lb))
    idx = lax.broadcasted_iota(jnp.int32, (T, LANES), 0).astype(f32)
    ef_ref[...] = jnp.exp((idx + 1.0) * lf)
    eb_ref[...] = jnp.exp((T - idx) * lb)
    wf_ref[...] = jnp.exp((T - 1.0 - idx) * lf)
    wb_ref[...] = jnp.exp(idx * lb)
    srow = lax.broadcasted_iota(jnp.int32, (8, LANES), 0)
    dec_ref[...] = jnp.where(srow == 0, jnp.exp(T * lf), jnp.exp(T * lb))


def _ret_tables(logit):
    lg = jnp.broadcast_to(logit[:, :, None, None], (2, RH, 1, LANES))
    tab = pl.BlockSpec((None, T, LANES), lambda h: (h, 0, 0))
    return pl.pallas_call(
        _ret_tab_kernel,
        grid=(RH,),
        in_specs=[pl.BlockSpec((2, None, 1, LANES), lambda h: (0, h, 0, 0))],
        out_specs=[tab, tab, tab, tab, tab, pl.BlockSpec((None, 8, LANES), lambda h: (h, 0, 0))],
        out_shape=[jax.ShapeDtypeStruct((RH, T, T), f32)] + [jax.ShapeDtypeStruct((RH, T, LANES), f32)] * 4
        + [jax.ShapeDtypeStruct((RH, 8, LANES), f32)],
        compiler_params=_cp(("parallel",)),
        name="ret_tables",
    )(lg)


def _mod_kernel(a_ref, w_ref, b_ref, o_ref):
    a = _silu(a_ref[...])
    o_ref[...] = _dot(a, w_ref[...]) + b_ref[...]


def _modulation(a8, w, b):
    tn = MOD_TN
    n = w.shape[1]
    return pl.pallas_call(
        _mod_kernel,
        grid=(n // tn,),
        in_specs=[pl.BlockSpec((8, D), lambda j: (0, 0)),
                  pl.BlockSpec((D, tn), lambda j: (0, j)),
                  pl.BlockSpec((1, tn), lambda j: (0, j))],
        out_specs=pl.BlockSpec((8, tn), lambda j: (0, j)),
        out_shape=jax.ShapeDtypeStruct((8, n), f32),
        compiler_params=_cp(("parallel",)),
        name="modulation",
    )(a8, w, b)


def _rms_mod(x, nw, sc, sh):
    y = x * lax.rsqrt(jnp.mean(x * x, axis=-1, keepdims=True) + EPS)
    return (y * nw) * (1.0 + sc) + sh


def _norm_kernel(x_ref, nw_ref, sc_ref, sh_ref, o_ref):
    o_ref[...] = _rms_mod(x_ref[...], nw_ref[...], sc_ref[...], sh_ref[...]).astype(o_ref.dtype)


def _norm_mod(x, nw, sc, sh):
    L = x.shape[0]
    tm = min(L, NORM_TM)
    vec = pl.BlockSpec((1, D), lambda i: (0, 0))
    return pl.pallas_call(
        _norm_kernel,
        grid=(L // tm,),
        in_specs=[pl.BlockSpec((tm, D), lambda i: (i, 0)), vec, vec, vec],
        out_specs=pl.BlockSpec((tm, D), lambda i: (i, 0)),
        out_shape=jax.ShapeDtypeStruct((L, D), bf16),
        compiler_params=_cp(("parallel",)),
        name="norm_mod",
    )(x, nw, sc, sh)


def _proj_kernel(u_ref, w_ref, *rest, bw, tn, rope, scale_from, relu2):
    *rest, wb = rest
    if len(w_ref.shape) == 3:
        w_ref = w_ref.at[0]

    @pl.when(pl.program_id(1) == 0)
    def _():
        lane = lax.broadcasted_iota(jnp.int32, (CAST_ROWS, tn), 1) % HD
        for rb in range(D // CAST_ROWS):
            rows = slice(rb * CAST_ROWS, (rb + 1) * CAST_ROWS)
            w = w_ref[rows, :]
            if rope:
                qd = HD // 4
                w = jnp.where((lane >= qd) & (lane < 2 * qd), pltpu.roll(w, tn - qd, 1),
                              jnp.where((lane >= 2 * qd) & (lane < 3 * qd), pltpu.roll(w, qd, 1), w))
            wb[rows, :] = w.astype(bf16)

    if rope:
        rc_ref, rs_ref, cc_ref, cs_ref, o_ref = rest
        tm = u_ref.shape[0]
        scale = jnp.where(pl.program_id(0) >= scale_from, HD ** -0.5, 1.0).astype(f32)
        cos = _rope_rows(rc_ref[...], cc_ref[...], tm) * scale
        sin = _rope_rows(rs_ref[...], cs_ref[...], tm) * scale
    else:
        (o_ref,) = rest
    sw = min(tn, MXU_N)
    for nb in range(tn // sw):
        acc = _dot(u_ref[...], wb[:, nb * sw:(nb + 1) * sw])
        if relu2:
            acc = jnp.maximum(acc, 0.0)
            acc = acc * acc
        if rope:
            for hh in range(sw // HD):
                t = acc[:, hh * HD:(hh + 1) * HD]
                o_ref[nb * (sw // HD) + hh] = (t * cos + pltpu.roll(t, HD // 2, 1) * sin).astype(o_ref.dtype)
        elif bw and bw <= sw:
            for hh in range(sw // bw):
                o_ref[nb * (sw // bw) + hh] = acc[:, hh * bw:(hh + 1) * bw].astype(o_ref.dtype)
        elif bw:
            per = bw // sw
            o_ref[nb // per, :, (nb % per) * sw:(nb % per + 1) * sw] = acc.astype(o_ref.dtype)
        else:
            o_ref[:, nb * sw:(nb + 1) * sw] = acc.astype(o_ref.dtype)


def _project(u, w, *, col0, ncols, bw, tn, out_dtype=bf16, rope=None, scale_from=0, relu2=False, name):
    M = u.shape[0]
    N = ncols
    assert w.dtype == f32 and col0 % LANES == 0 and ncols % tn == 0
    jb = col0 // tn
    tm = min(M, PROJ_TM)
    if col0 % tn:
        w_spec = pl.BlockSpec((pl.Element(1), pl.Element(D), pl.Element(tn)),
                              lambda j, i: (0, 0, pl.multiple_of(col0 + j * tn, LANES)))
    else:
        w_spec = pl.BlockSpec((None, D, tn), lambda j, i: (0, 0, jb + j))
    in_specs = [pl.BlockSpec((tm, D), lambda j, i: (i, 0)), w_spec]
    args = [u, w]
    if rope is not None:
        in_specs += [pl.BlockSpec((tm // GRID_W, LANES), lambda j, i: (i, 0))] * 2
        in_specs += [pl.BlockSpec((GRID_W, LANES), lambda j, i: (0, 0))] * 2
        args += list(rope)
    if bw:
        out_spec = pl.BlockSpec((tn // bw, tm, bw), lambda j, i: (j, i, 0))
        out_shape = jax.ShapeDtypeStruct((N // bw, M, bw), out_dtype)
    else:
        out_spec = pl.BlockSpec((tm, tn), lambda j, i: (i, j))
        out_shape = jax.ShapeDtypeStruct((M, N), out_dtype)
    return pl.pallas_call(
        functools.partial(_proj_kernel, bw=bw, tn=tn, rope=rope is not None, scale_from=scale_from, relu2=relu2),
        grid=(N // tn, M // tm),
        in_specs=in_specs,
        out_specs=out_spec,
        out_shape=out_shape,
        scratch_shapes=[pltpu.VMEM((D, tn), bf16)],
        compiler_params=_cp(("parallel", "arbitrary")),
        name=name,
    )(*args)


def _shift_matrix():
    rows = lax.broadcasted_iota(jnp.int32, (4 * T, T + 2 * HALO), 0)
    cols = lax.broadcasted_iota(jnp.int32, (4 * T, T + 2 * HALO), 1)
    blk = rows // T
    tap = jnp.where(blk < KC // 2, blk, blk + 1)
    return jnp.where(cols == (rows - blk * T) + HALO + tap - KC // 2, 1.0, 0.0).astype(bf16)


def _conv_shift(shift, main, prev, nxt, has_prev, has_next):
    zero = jnp.zeros_like(prev)
    ext = jnp.concatenate([jnp.where(has_prev, prev, zero), main, jnp.where(has_next, nxt, zero)], axis=0)
    sh = _dot(shift, ext)
    return [sh[0:T], sh[T:2 * T], main.astype(f32), sh[2 * T:3 * T], sh[3 * T:4 * T]]


def _conv_taps(taps, cw, cb):
    acc = cb
    for j in range(KC):
        acc = acc + taps[j] * cw[j:j + 1, :]
    return _silu(acc)


def _decay_prologue(dt_ref, bias_ref, arow_ref):
    dt = _softplus(dt_ref[...] + bias_ref[...])
    la = dt * arow_ref[...]
    acs = _cumsum_rows(la, _tri_incl())
    tot = acs[T - 1:T, :]
    return dt, la, acs, tot


def _transpose_blocks(x):
    n = x.shape[1] // LANES
    return jnp.concatenate([x[:, b * LANES:(b + 1) * LANES].T for b in range(n)], axis=0)


def _ssd_state_update(hs_ref, g, xc, bc16, w_rows, dec_rows):
    xT = _transpose_blocks(xc)
    lhs = jnp.concatenate(
        [(xT[r * P:(r + 1) * P, :] * w_rows[r:r + 1, :]).astype(bf16) for r in range(R)], axis=0)
    upd = _dot(lhs, bc16)
    old = hs_ref[g]
    hs_ref[g] = jnp.concatenate(
        [old[r * P:(r + 1) * P, :] * dec_rows[r:r + 1, :] + upd[r * P:(r + 1) * P, :] for r in range(R)], axis=0)


def _ret_state_update(hr_ref, h, v16, k16, wcol, dec_row):
    vw = (v16.astype(f32) * wcol).T.astype(bf16)
    hr_ref[h] = hr_ref[h] * dec_row + _dot(vw, k16)


def _state_kernel(*refs, nc, fwd, emit, write_conv):
    it = iter(refs)
    k_ref, v_ref = next(it), next(it)
    xs_ref, xs_p, xs_n = next(it), next(it), next(it)
    b_ref, b_p, b_n = next(it), next(it), next(it)
    if write_conv:
        c_ref, c_p, c_n = next(it), next(it), next(it)
    dt_ref = next(it)
    cwx, cbx, cwb, cbb = next(it), next(it), next(it), next(it)
    if write_conv:
        cwc, cbc = next(it), next(it)
    bias_ref, arow_ref = next(it), next(it)
    rw_ref, rdec_ref = next(it), next(it)
    h0r_ref, h0s_ref = next(it), next(it)
    if emit:
        gr_ref, gs_ref = next(it), next(it)
    if write_conv:
        xo_ref, bco_ref = next(it), next(it)
    hfr_ref, hfs_ref = next(it), next(it)
    hr, hs, wT_s, decT_s = next(it), next(it), next(it), next(it)

    s = pl.program_id(0)
    c = s if fwd else nc - 1 - s
    has_prev = c > 0
    has_next = c < nc - 1

    @pl.when(s == 0)
    def _():
        hr[...] = h0r_ref[...]
        hs[...] = h0s_ref[...]

    if emit:
        gr_ref[0] = hr[...].astype(bf16)
        gs_ref[0] = hs[...].astype(bf16)

    dt, la, acs, tot = _decay_prologue(dt_ref, bias_ref, arow_ref)
    wexp = jnp.exp(tot - acs) if fwd else jnp.exp(acs - la)
    wT_s[...] = (wexp * dt).T
    decT_s[...] = jnp.broadcast_to(jnp.exp(tot), (T, 2 * SH)).T
    off = 0 if fwd else SH

    shift = _shift_matrix()

    def cat(a, b):
        return jnp.concatenate([a, b], axis=1)

    def shifted(g):
        tx = _conv_shift(shift, xs_ref[g], xs_p[g], xs_n[g], has_prev, has_next)
        if write_conv:
            tb = _conv_shift(shift, cat(b_ref[g], c_ref[g]), cat(b_p[g], c_p[g]), cat(b_n[g], c_n[g]),
                             has_prev, has_next)
        else:
            tb = _conv_shift(shift, b_ref[g], b_p[g], b_n[g], has_prev, has_next)
        return tx, tb

    nxt_taps = shifted(0)
    for g in range(NG):
        tx, tb = nxt_taps
        if g + 1 < NG:
            nxt_taps = shifted(g + 1)
        xc16 = _conv_taps(tx, cwx[g], cbx[g]).astype(bf16)
        if write_conv:
            bcc = _conv_taps(tb, cat(cwb[g], cwc[g]), cat(cbb[g], cbc[g])).astype(bf16)
            bc16 = bcc[:, :NS]
            xo_ref[g] = xc16
            bco_ref[g] = bcc
        else:
            bc16 = _conv_taps(tb, cwb[g], cbb[g]).astype(bf16)
        r0 = off + g * R
        _ssd_state_update(hs, g, xc16.astype(f32), bc16, wT_s[r0:r0 + R, :], decT_s[r0:r0 + R, :])

    dsel = 0 if fwd else 1

    for h in range(RH):
        _ret_state_update(hr, h, v_ref[h], k_ref[h], rw_ref[h], rdec_ref[h][dsel:dsel + 1, :])

    @pl.when(s == nc - 1)
    def _():
        hfr_ref[...] = hr[...]
        hfs_ref[...] = hs[...]


def _state_sweep(L, *, fwd, emit, write_conv, k, k_blk, v, v_blk, xs, xs_blk, bsrc, b_blk, c_blk, dt,
                 convw, bias, arow, rw, rdec, h0r, h0s):
    nc = L // T
    tb = T // HALO
    nrb = L // HALO

    def cidx(s):
        return s if fwd else nc - 1 - s

    def main(lead, n, width):
        return pl.BlockSpec((n, T, width), lambda s: (lead, cidx(s), 0))

    def prev(lead, n, width):
        return pl.BlockSpec((n, HALO, width), lambda s: (lead, jnp.maximum(cidx(s) * tb - 1, 0), 0))

    def nxt(lead, n, width):
        return pl.BlockSpec((n, HALO, width), lambda s: (lead, jnp.minimum((cidx(s) + 1) * tb, nrb - 1), 0))

    def whole(a):
        nd = a.ndim
        return pl.BlockSpec(a.shape, lambda s: (0,) * nd)

    cwx, cbx, cwb, cbb, cwc, cbc = convw
    in_specs = [main(k_blk, RH, HD), main(v_blk, RH, HD),
                main(xs_blk, NG, GW), prev(xs_blk, NG, GW), nxt(xs_blk, NG, GW),
                main(b_blk, NG, NS), prev(b_blk, NG, NS), nxt(b_blk, NG, NS)]
    args = [k, v, xs, xs, xs, bsrc, bsrc, bsrc]
    if write_conv:
        in_specs += [main(c_blk, NG, NS), prev(c_blk, NG, NS), nxt(c_blk, NG, NS)]
        args += [bsrc, bsrc, bsrc]
    in_specs += [pl.BlockSpec((T, 2 * SH), lambda s: (cidx(s), 0))]
    args += [dt]
    small = [cwx, cbx, cwb, cbb] + ([cwc, cbc] if write_conv else []) + [bias, arow, rw, rdec, h0r, h0s]
    in_specs += [whole(a) for a in small]
    args += small

    out_specs, out_shape = [], []
    if emit:
        out_specs += [pl.BlockSpec((1, RH, HD, HD), lambda s: (cidx(s), 0, 0, 0)),
                      pl.BlockSpec((1, NG, GW, NS), lambda s: (cidx(s), 0, 0, 0))]
        out_shape += [jax.ShapeDtypeStruct((nc, RH, HD, HD), bf16), jax.ShapeDtypeStruct((nc, NG, GW, NS), bf16)]
    if write_conv:
        out_specs += [pl.BlockSpec((NG, T, GW), lambda s: (0, cidx(s), 0)),
                      pl.BlockSpec((NG, T, 2 * NS), lambda s: (0, cidx(s), 0))]
        out_shape += [jax.ShapeDtypeStruct((NG, L, GW), bf16), jax.ShapeDtypeStruct((NG, L, 2 * NS), bf16)]
    out_specs += [pl.BlockSpec((RH, HD, HD), lambda s: (0, 0, 0)), pl.BlockSpec((NG, GW, NS), lambda s: (0, 0, 0))]
    out_shape += [jax.ShapeDtypeStruct((RH, HD, HD), f32), jax.ShapeDtypeStruct((NG, GW, NS), f32)]

    return pl.pallas_call(
        functools.partial(_state_kernel, nc=nc, fwd=fwd, emit=emit, write_conv=write_conv),
        grid=(nc,),
        in_specs=in_specs,
        out_specs=out_specs,
        out_shape=out_shape,
        scratch_shapes=[pltpu.VMEM((RH, HD, HD), f32), pltpu.VMEM((NG, GW, NS), f32),
                        pltpu.VMEM((2 * SH, T), f32), pltpu.VMEM((2 * SH, T), f32)],
        compiler_params=_cp(("arbitrary",)),
        name="state_sweep_" + ("f" if fwd else "b") + ("_emit" if emit else ""),
    )(*args)


def _fwd_kernel(qk_ref, vg_ref, z_ref, xs_ref, bc_ref, dt_ref, gret_ref, gssd_ref,
                wret_ref, ef_ref, eb_ref, wf_ref, rdec_ref, bias_ref, arow_ref, dexp_ref, snw_ref, rnw_ref,
                h0r_ref, h0s_ref, dsel_ref, esel_ref, yr_ref, ys_ref,
                hr, hs, ypre, ap_s, ep_s, apT_s, wT_s, decT_s):
    s = pl.program_id(0)

    @pl.when(s == 0)
    def _():
        hr[...] = h0r_ref[...]
        hs[...] = h0s_ref[...]

    dt, la, acs, tot = _decay_prologue(dt_ref, bias_ref, arow_ref)
    lane = lax.broadcasted_iota(jnp.int32, (T, 2 * SH), 1)
    is_f = lane < SH
    a1 = jnp.where(is_f, acs, acs - la)
    e1 = jnp.exp(jnp.where(is_f, acs, tot - (acs - la)))
    a2 = a1 * LOG2E
    ldt = jnp.maximum(jnp.log2(dt), LOG2_FLOOR)
    for k, part in enumerate(_split3(a2)):
        ap_s[k] = part
    for k, part in enumerate(_split3(jnp.where(is_f, ldt - a2, ldt + a2))):
        apT_s[k] = part.T
    for k, part in enumerate(_split3(e1)):
        ep_s[k] = part
    wT_s[...] = (jnp.exp(tot - acs) * dt).T
    decT_s[...] = jnp.broadcast_to(jnp.exp(tot), (T, 2 * SH)).T

    ii = lax.broadcasted_iota(jnp.int32, (T, T), 0)
    jj = lax.broadcasted_iota(jnp.int32, (T, T), 1)
    lower = jj <= ii
    part_masks = [((lane >= k * R) & (lane < (k + 1) * R)) | ((lane >= SH + k * R) & (lane < SH + (k + 1) * R))
                  for k in range(3)]
    ones_lanes = (lane >= ONES_LANE) & (lane < ONES_LANE + 3)
    row16 = lax.broadcasted_iota(jnp.int32, (16, T), 0)
    lane_p = lax.broadcasted_iota(jnp.int32, (T, 2 * P), 1)

    def pack(parts_ref, g, with_ones):
        acc = jnp.where(ones_lanes, 1.0, 0.0) if with_ones else jnp.zeros((T, 2 * SH), f32)
        for k in range(3):
            shift = (2 * SH - g * R + k * R) % (2 * SH)
            acc = acc + jnp.where(part_masks[k], pltpu.roll(parts_ref[k], shift, 1), 0.0)
        return acc.astype(bf16)

    def front(g):
        b16 = bc_ref[g, :, 0:NS]
        c16 = bc_ref[g, :, NS:2 * NS]
        sc = _nt(c16, b16)
        r0 = g * R
        lhs_a = pack(ap_s, g, True)
        lhs_e = pack(ep_s, g, False)
        cf_rows = [apT_s[k, r0:r0 + R, :] for k in range(3)]
        cb_rows = [apT_s[k, SH + r0:SH + r0 + R, :] for k in range(3)]
        hcat = jnp.concatenate([hs[g].astype(bf16), gssd_ref[0, g]], axis=0)
        ycross = _nt(c16, hcat)
        efb = _dot(lhs_e, esel_ref[...])
        dmats = []
        for r in range(R):
            lo = jnp.zeros((16, T), f32)
            up = jnp.zeros((16, T), f32)
            for k in range(3):
                lo = jnp.where(row16 == k, cf_rows[k][r:r + 1, :], lo)
                up = jnp.where(row16 == k, cb_rows[k][r:r + 1, :], up)
            dyn = jnp.concatenate([lo, up], axis=1).astype(bf16)
            rhs = jnp.concatenate([dsel_ref[r, 0:ONES_LANE, :], dyn, dsel_ref[r, ONES_LANE + 16:, :]], axis=0)
            dmats.append(_dot(lhs_a, rhs))
        return b16, sc, ycross, efb, dmats

    def back(g, staged, ssq):
        b16, sc, ycross, efb, dmats = staged
        r0 = g * R
        xg = xs_ref[g]
        zg = z_ref[g].astype(f32)
        dg = dexp_ref[g]
        ms = [(sc * jnp.exp2(jnp.where(lower, dm[:, :T], dm[:, T:]))).astype(bf16) for dm in dmats]
        ys_parts = []
        for t in range(R // 2):
            xp = xg[:, t * 2 * P:(t + 1) * 2 * P]
            zero = jnp.zeros_like(xp)
            rhs = jnp.concatenate([jnp.where(lane_p < P, xp, zero), jnp.where(lane_p >= P, xp, zero)], axis=0)
            ys_parts.append(_dot(jnp.concatenate([ms[2 * t], ms[2 * t + 1]], axis=1), rhs))
        yg = jnp.concatenate(ys_parts, axis=1)
        yg = yg + ycross[:, :GW] * efb[:, :GW] + ycross[:, GW:] * efb[:, GW:]
        yg = (yg + xg.astype(f32) * dg) * _silu(zg)
        ypre[g] = yg
        ssq = ssq + jnp.sum(yg * yg, axis=1, keepdims=True)
        _ssd_state_update(hs, g, xg.astype(f32), b16, wT_s[r0:r0 + R, :], decT_s[r0:r0 + R, :])
        return ssq

    ssq = jnp.zeros((T, 1), f32)
    staged = front(0)
    for g in range(NG):
        cur = staged
        if g + 1 < NG:
            staged = front(g + 1)
        ssq = back(g, cur, ssq)
    rs = lax.rsqrt(ssq * (1.0 / SI) + EPS)
    for g in range(NG):
        ys_ref[:, g * GW:(g + 1) * GW] = ((ypre[g] * rs) * snw_ref[g]).astype(ys_ref.dtype)

    zero_h = jnp.zeros((T, HD), bf16)

    def blockdiag(a, b):
        return jnp.concatenate([jnp.concatenate([a, zero_h], axis=1), jnp.concatenate([zero_h, b], axis=1)], axis=0)

    def ret_front(t):
        pair = (2 * t, 2 * t + 1)
        q16 = [qk_ref[h] for h in pair]
        k16 = [qk_ref[RH + h] for h in pair]
        s2 = _nt(jnp.concatenate(q16, axis=1), blockdiag(*k16))
        cross = []
        for n, h in enumerate(pair):
            qf = q16[n].astype(f32)
            lc = jnp.concatenate([(qf * ef_ref[h]).astype(bf16), (qf * eb_ref[h]).astype(bf16)], axis=1)
            hcat = jnp.concatenate([hr[h].astype(bf16), gret_ref[0, h]], axis=1)
            cross.append(_nt(lc, hcat))
        return k16, s2, cross

    ret_staged = ret_front(0)
    for t in range(RH // 2):
        pair = (2 * t, 2 * t + 1)
        k16, s2, cross = ret_staged
        if t + 1 < RH // 2:
            ret_staged = ret_front(t + 1)
        v16 = [vg_ref[h] for h in pair]
        m2 = (s2 * jnp.concatenate([wret_ref[h] for h in pair], axis=1)).astype(bf16)
        y2 = _dot(m2, blockdiag(*v16))
        for n, h in enumerate(pair):
            y = y2[:, n * HD:(n + 1) * HD] + cross[n]
            mu = jnp.mean(y, axis=-1, keepdims=True)
            d = y - mu
            yn = d * lax.rsqrt(jnp.mean(d * d, axis=-1, keepdims=True) + EPS)
            gg = vg_ref[RH + h].astype(f32)
            yr_ref[:, h * HD:(h + 1) * HD] = ((yn * rnw_ref[h]) * _silu(gg)).astype(yr_ref.dtype)
            _ret_state_update(hr, h, v16[n], k16[n], wf_ref[h], rdec_ref[h][0:1, :])


def _fwd_sweep(L, qk, vg, zx, xc, bcc, dt, gret, gssd, tabs, bias, arow, dexp, snw, rnw, h0r, h0s):
    nc = L // T
    wret, ef, eb, wf, _, rdec = tabs

    def blk(lead, n, width):
        return pl.BlockSpec((n, T, width), lambda s: (lead, s, 0))

    def whole(a):
        nd = a.ndim
        return pl.BlockSpec(a.shape, lambda s: (0,) * nd)

    dsel, esel = _selector_constants()
    small = [wret, ef, eb, wf, rdec, bias, arow, dexp, snw, rnw, h0r, h0s, dsel, esel]
    in_specs = [blk(0, 2 * RH, HD), blk(0, 2 * RH, HD), blk(0, NG, GW), blk(0, NG, GW), blk(0, NG, 2 * NS),
                pl.BlockSpec((T, 2 * SH), lambda s: (s, 0)),
                pl.BlockSpec((1, RH, HD, HD), lambda s: (s, 0, 0, 0)),
                pl.BlockSpec((1, NG, GW, NS), lambda s: (s, 0, 0, 0))] + [whole(a) for a in small]
    return pl.pallas_call(
        _fwd_kernel,
        grid=(nc,),
        in_specs=in_specs,
        out_specs=[pl.BlockSpec((T, D), lambda s: (s, 0)), pl.BlockSpec((T, SI), lambda s: (s, 0))],
        out_shape=[jax.ShapeDtypeStruct((L, D), bf16), jax.ShapeDtypeStruct((L, SI), bf16)],
        scratch_shapes=[pltpu.VMEM((RH, HD, HD), f32), pltpu.VMEM((NG, GW, NS), f32), pltpu.VMEM((NG, T, GW), f32),
                        pltpu.VMEM((3, T, 2 * SH), f32), pltpu.VMEM((3, T, 2 * SH), f32),
                        pltpu.VMEM((3, 2 * SH, T), f32)]
        + [pltpu.VMEM((2 * SH, T), f32)] * 2,
        compiler_params=_cp(("arbitrary",)),
        name="fwd_sweep",
    )(qk, vg, zx, xc, bcc, dt, gret, gssd, *small)


def _branch_out_kernel(yr_ref, ys_ref, wr_ref, ws_ref, gr_ref, gs_ref, o_ref):
    for nb in range(o_ref.shape[1] // MXU_N):
        cols = slice(nb * MXU_N, (nb + 1) * MXU_N)
        acc_r = _dot(yr_ref[...], wr_ref[:, cols])
        acc_s = _dot(ys_ref[...], ws_ref[:, cols])
        m = _sigmoid(gr_ref[:, cols].astype(f32)) * acc_r + _sigmoid(gs_ref[:, cols].astype(f32)) * acc_s
        o_ref[:, cols] = m.astype(o_ref.dtype)


def _branch_out(yr, ys, wr, ws, gates):
    L = yr.shape[0]
    tm, tn = min(L, BRANCH_TM), BRANCH_TN
    nj = D // tn
    return pl.pallas_call(
        _branch_out_kernel,
        grid=(L // tm, nj),
        in_specs=[pl.BlockSpec((tm, D), lambda i, j: (i, 0)),
                  pl.BlockSpec((tm, SI), lambda i, j: (i, 0)),
                  pl.BlockSpec((D, tn), lambda i, j: (0, j)),
                  pl.BlockSpec((SI, tn), lambda i, j: (0, j)),
                  pl.BlockSpec((tm, tn), lambda i, j: (i, j)),
                  pl.BlockSpec((tm, tn), lambda i, j: (i, nj + j))],
        out_specs=pl.BlockSpec((tm, tn), lambda i, j: (i, j)),
        out_shape=jax.ShapeDtypeStruct((L, D), bf16),
        compiler_params=_cp(("parallel", "parallel")),
        name="branch_out",
    )(yr, ys, wr, ws, gates, gates)


def _resid_kernel(m_ref, w_ref, x_ref, g_ref, nw_ref, sc_ref, sh_ref, h_ref, f_ref):
    h = x_ref[...] + g_ref[...] * _dot(m_ref[...], w_ref[...])
    h_ref[...] = h
    f_ref[...] = _rms_mod(h, nw_ref[...], sc_ref[...], sh_ref[...]).astype(f_ref.dtype)


def _mix_residual(m, w, x, gate, nw, sc, sh):
    L = m.shape[0]
    tm = min(L, MIX_TM)
    vec = pl.BlockSpec((1, D), lambda i: (0, 0))
    row = pl.BlockSpec((tm, D), lambda i: (i, 0))
    return pl.pallas_call(
        _resid_kernel,
        grid=(L // tm,),
        in_specs=[row, pl.BlockSpec((D, D), lambda i: (0, 0)), row, vec, vec, vec, vec],
        out_specs=[row, row],
        out_shape=[jax.ShapeDtypeStruct((L, D), f32), jax.ShapeDtypeStruct((L, D), bf16)],
        compiler_params=_cp(("parallel",)),
        name="mix_residual",
    )(m, w, x, gate, nw, sc, sh)


def _mlp2_kernel(a_ref, w_ref, h_ref, g_ref, fw_ref, o_ref, *, nk):
    kk = pl.program_id(1)

    @pl.when(kk == 0)
    def _():
        o_ref[...] = jnp.zeros_like(o_ref)

    for nb in range(D // MLP2_STRIP):
        cols = slice(nb * MLP2_STRIP, (nb + 1) * MLP2_STRIP)
        o_ref[:, cols] += _dot(a_ref[...], w_ref[:, cols])

    @pl.when(kk == nk - 1)
    def _():
        h2 = h_ref[...] + g_ref[...] * o_ref[...]
        y = h2 * lax.rsqrt(jnp.mean(h2 * h2, axis=-1, keepdims=True) + EPS)
        o_ref[...] = y * fw_ref[...]


def _mlp2_final(a, w, h, gate, fw):
    L = a.shape[0]
    tm, tk = min(L, MLP2_TM), MLP2_TK
    nk = DFF // tk
    vec = pl.BlockSpec((1, D), lambda i, k: (0, 0))
    return pl.pallas_call(
        functools.partial(_mlp2_kernel, nk=nk),
        grid=(L // tm, nk),
        in_specs=[pl.BlockSpec((tm, tk), lambda i, k: (i, k)),
                  pl.BlockSpec((tk, D), lambda i, k: (k, 0)),
                  pl.BlockSpec((tm, D), lambda i, k: (i, 0)), vec, vec],
        out_specs=pl.BlockSpec((tm, D), lambda i, k: (i, 0)),
        out_shape=jax.ShapeDtypeStruct((L, D), f32),
        compiler_params=_cp(("parallel", "arbitrary")),
        name="mlp2_final",
    )(a, w, h, gate, fw)


def _split_conv(conv_w, conv_b):
    def grp(w, b, width):
        return (jnp.transpose(w.reshape(KC, NG, width), (1, 0, 2)), b.reshape(NG, 1, width))
    cwx, cbx = grp(conv_w[:, :SI], conv_b[:SI], GW)
    cwb, cbb = grp(conv_w[:, SI:SI + NG * NS], conv_b[SI:SI + NG * NS], NS)
    cwc, cbc = grp(conv_w[:, SI + NG * NS:], conv_b[SI + NG * NS:], NS)
    return cwx, cbx, cwb, cbb, cwc, cbc


def kernel(x, c, ctx, c_ctx, w_mod, b_mod, norm1_w, w_in, conv_w, conv_b, ret_decay_logit, ret_norm_w,
           ssd_a_log, ssd_dt_bias, ssd_d, ssd_norm_w, w_ret_out, w_ssd_out, w_o, norm2_w, w_mlp1, w_mlp2,
           final_norm_w):
    depth = w_mod.shape[0]
    assert depth == 1 and x.shape[0] == 1 and x.shape[2] == D
    L = x.shape[1]
    Lc = ctx.shape[1]
    assert L % T == 0 and Lc % T == 0 and L % GRID_W == 0
    xl = x[0]
    xcx = ctx[0]
    ly = 0

    a8 = jnp.zeros((8, D), f32).at[0].set(c[0]).at[1].set(c_ctx)
    mod = _modulation(a8, w_mod[ly], b_mod[ly][None, :])
    sh_a, sc_a, g_a, sh_f, sc_f, g_f = [mod[0:1, i * D:(i + 1) * D] for i in range(6)]
    csh_a, csc_a = mod[1:2, 0:D], mod[1:2, D:2 * D]

    o_q, o_k, o_v, o_z, o_x = 0, D, 2 * D, 4 * D, 4 * D + SI
    o_b = o_x + SI
    o_dt = o_b + 2 * NG * NS
    o_gate = o_dt + 2 * SH

    def proj(src, col, n, **kw):
        return _project(src, w_in, col0=col, ncols=n, **kw)

    convw = _split_conv(conv_w[ly], conv_b[ly])
    bias = ssd_dt_bias[ly].reshape(1, 2 * SH)
    arow = (-jnp.exp(ssd_a_log[ly])).reshape(1, 2 * SH)
    dexp = jnp.repeat(ssd_d[ly], P).reshape(NG, 1, GW)
    snw = ssd_norm_w[ly].reshape(NG, 1, GW)
    rnw = ret_norm_w[ly].reshape(RH, 1, HD)
    nw1 = norm1_w[ly][None, :]

    tabs = _ret_tables(ret_decay_logit[ly])
    _, _, _, wf_t, wb_t, rdec = tabs
    zr = jnp.zeros((RH, HD, HD), f32)
    zs = jnp.zeros((NG, GW, NS), f32)

    uc = _norm_mod(xcx, nw1, csc_a, csh_a)
    ones_c, zeros_c = jnp.ones((Lc // GRID_W, LANES), f32), jnp.zeros((Lc // GRID_W, LANES), f32)
    ident = (ones_c, zeros_c, jnp.ones((GRID_W, LANES), f32), jnp.zeros((GRID_W, LANES), f32))
    kc = proj(uc, o_k, D, bw=HD, tn=CTX_TN, rope=ident, scale_from=0, name="ctx_k")
    vc = proj(uc, o_v, D, bw=HD, tn=CTX_TN, name="ctx_v")
    bcx = proj(uc, o_b, NG * NS, bw=NS, tn=CTX_TN, name="ctx_b")
    xsc = proj(uc, o_x, SI, bw=GW, tn=CTX_TN, name="ctx_x")
    dtc = proj(uc, o_dt, 2 * SH, bw=0, tn=2 * SH, out_dtype=f32, name="ctx_dt")
    common = dict(k=kc, k_blk=0, v=vc, v_blk=0, xs=xsc, xs_blk=0, bsrc=bcx, b_blk=0, c_blk=0, dt=dtc,
                  convw=convw, bias=bias, arow=arow, rdec=rdec, h0r=zr, h0s=zs)
    cfr, cfs = _state_sweep(Lc, fwd=True, emit=False, write_conv=False, rw=wf_t, **common)
    cbr, cbs = _state_sweep(Lc, fwd=False, emit=False, write_conv=False, rw=wb_t, **common)

    u = _norm_mod(xl, nw1, sc_a, sh_a)
    rope = _rope_tables(L)
    qk = proj(u, o_q, 2 * D, bw=HD, tn=PROJ_TN, rope=rope, scale_from=D // PROJ_TN, name="lat_qk")
    vg = proj(u, o_v, 2 * D, bw=HD, tn=PROJ_TN, name="lat_vg")
    zx = proj(u, o_z, 2 * SI, bw=GW, tn=PROJ_TN, name="lat_zx")
    bcl = proj(u, o_b, 2 * NG * NS, bw=NS, tn=PROJ_TN, name="lat_bc")
    dtl = proj(u, o_dt, 2 * SH, bw=0, tn=2 * SH, out_dtype=f32, name="lat_dt")
    gates = proj(u, o_gate, 2 * D, bw=0, tn=PROJ_TN, name="lat_gates")

    gret, gssd, xcv, bccv, _, _ = _state_sweep(
        L, fwd=False, emit=True, write_conv=True, k=qk, k_blk=1, v=vg, v_blk=0, xs=zx, xs_blk=1, bsrc=bcl,
        b_blk=0, c_blk=1, dt=dtl, convw=convw, bias=bias, arow=arow, rw=wb_t, rdec=rdec,
        h0r=cbr, h0s=cbs)
    yr, ys = _fwd_sweep(L, qk, vg, zx, xcv, bccv, dtl, gret, gssd, tabs, bias, arow, dexp, snw, rnw, cfr, cfs)

    m = _branch_out(yr, ys, w_ret_out[ly].astype(bf16), w_ssd_out[ly].astype(bf16), gates)
    h1, f = _mix_residual(m, w_o[ly].astype(bf16), xl, g_a, norm2_w[ly][None, :], sc_f, sh_f)
    a = _project(f, w_mlp1, col0=0, ncols=DFF, bw=0, tn=PROJ_TN, relu2=True, name="mlp1")
    out = _mlp2_final(a, w_mlp2[ly].astype(bf16), h1, g_f, final_norm_w[None, :])
    return out[None]
```

```python
import functools
import math

import numpy as np
import jax
import jax.numpy as jnp
from jax import lax
from jax.experimental import pallas as pl
from jax.experimental.pallas import tpu as pltpu

f32 = jnp.float32
bf16 = jnp.bfloat16

D = 2048
T = 128
GRID_W = 64
HD = 128
RH = D // HD
SI = 2 * D
P = 64
SH = SI // P
NS = 128
NG = 8
R = SH // NG
GW = R * P
KC = 5
DFF = 4 * D
ROPE_BASE = 10000.0
EPS = 1e-6
LANES = 128
MXU_N = 256
CAST_ROWS = 256
HALO = 16
LOG2E = 1.0 / math.log(2.0)
LOG2_FLOOR = -300.0
ONES_LANE = 96
VMEM_LIMIT = 56 * 1024 * 1024

PROJ_TM, PROJ_TN = 2048, 1024
CTX_TN = 512
NORM_TM = 1024
MOD_TN = 1024
BRANCH_TM, BRANCH_TN = 1024, 512
MIX_TM = 512
MLP2_TM, MLP2_TK, MLP2_STRIP = 512, 2048, 512


def _cp(sem):
    return pltpu.CompilerParams(dimension_semantics=sem, vmem_limit_bytes=VMEM_LIMIT)


def _nt(a, b):
    return lax.dot_general(a, b, (((1,), (1,)), ((), ())), preferred_element_type=f32)


def _dot(a, b):
    return jnp.dot(a, b, preferred_element_type=f32)


def _sigmoid(x):
    return 0.5 + 0.5 * jnp.tanh(0.5 * x)


def _silu(x):
    h = 0.5 * x
    return h + h * jnp.tanh(h)


def _softplus(x):
    return jnp.maximum(x, 0.0) + jnp.log1p(jnp.exp(-jnp.abs(x)))


def _log_sigmoid(x):
    return jnp.minimum(x, 0.0) - jnp.log1p(jnp.exp(-jnp.abs(x)))


def _split3(x):
    p1 = x.astype(bf16).astype(f32)
    r1 = x - p1
    p2 = r1.astype(bf16).astype(f32)
    p3 = (r1 - p2).astype(bf16).astype(f32)
    return p1, p2, p3


def _cumsum_rows(la, tri):
    a1, a2, a3 = _split3(la)
    return _dot(tri, a1.astype(bf16)) + _dot(tri, a2.astype(bf16)) + _dot(tri, a3.astype(bf16))


def _selector_constants():
    dsel = np.zeros((R, 2 * SH, 2 * T), np.float32)
    esel = np.zeros((2 * SH, 2 * GW), np.float32)
    for r in range(R):
        for k in range(3):
            dsel[r, k * R + r, :T] = 1.0
            dsel[r, SH + k * R + r, T:] = -1.0
            esel[k * R + r, r * P:(r + 1) * P] = 1.0
            esel[SH + k * R + r, GW + r * P:GW + (r + 1) * P] = 1.0
    return jnp.asarray(dsel, bf16), jnp.asarray(esel, bf16)


def _tri_incl():
    ii = lax.broadcasted_iota(jnp.int32, (T, T), 0)
    jj = lax.broadcasted_iota(jnp.int32, (T, T), 1)
    return jnp.where(jj <= ii, 1.0, 0.0).astype(bf16)


def _rope_tab_kernel(rc_ref, rs_ref, cc_ref, cs_ref):
    def tab(n):
        idx = lax.broadcasted_iota(jnp.int32, (n, LANES), 0).astype(f32)
        lane = lax.broadcasted_iota(jnp.int32, (n, LANES), 1)
        freqs = jnp.exp((lane % (HD // 4)).astype(f32) * (-math.log(ROPE_BASE) / (HD // 4)))
        ang = idx * freqs
        s = jnp.sin(ang)
        return jnp.cos(ang), jnp.where(lane < HD // 2, -s, s)

    rc_ref[...], rs_ref[...] = tab(rc_ref.shape[0])
    cc_ref[...], cs_ref[...] = tab(GRID_W)


def _rope_tables(L):
    rows = L // GRID_W
    shapes = [jax.ShapeDtypeStruct((rows, LANES), f32)] * 2 + [jax.ShapeDtypeStruct((GRID_W, LANES), f32)] * 2
    return pl.pallas_call(_rope_tab_kernel, out_shape=shapes, name="rope_tables")()


def _rope_rows(row_tab, col_tab, tm):
    lane = lax.broadcasted_iota(jnp.int32, (GRID_W, LANES), 1)
    use_row = (lane % (HD // 2)) < HD // 4
    return jnp.concatenate([jnp.where(use_row, row_tab[r:r + 1, :], col_tab) for r in range(tm // GRID_W)], axis=0)


def _ret_tab_kernel(lg_ref, w_ref, ef_ref, eb_ref, wf_ref, wb_ref, dec_ref):
    lf = _log_sigmoid(lg_ref[0])
    lb = _log_sigmoid(lg_ref[1])
    ii = lax.broadcasted_iota(jnp.int32, (T, T), 0)
    jj = lax.broadcasted_iota(jnp.int32, (T, T), 1)
    dl = (ii - jj).astype(f32)
    w_ref[...] = jnp.exp(jnp.where(jj <= ii, dl * lf, -dl * lb))
    idx = lax.broadcasted_iota(jnp.int32, (T, LANES), 0).astype(f32)
    ef_ref[...] = jnp.exp((idx + 1.0) * lf)
    eb_ref[...] = jnp.exp((T - idx) * lb)
    wf_ref[...] = jnp.exp((T - 1.0 - idx) * lf)
    wb_ref[...] = jnp.exp(idx * lb)
    srow = lax.broadcasted_iota(jnp.int32, (8, LANES), 0)
    dec_ref[...] = jnp.where(srow == 0, jnp.exp(T * lf), jnp.exp(T * lb))


def _ret_tables(logit):
    lg = jnp.broadcast_to(logit[:, :, None, None], (2, RH, 1, LANES))
    tab = pl.BlockSpec((None, T, LANES), lambda h: (h, 0, 0))
    return pl.pallas_call(
        _ret_tab_kernel,
        grid=(RH,),
        in_specs=[pl.BlockSpec((2, None, 1, LANES), lambda h: (0, h, 0, 0))],
        out_specs=[tab, tab, tab, tab, tab, pl.BlockSpec((None, 8, LANES), lambda h: (h, 0, 0))],
        out_shape=[jax.ShapeDtypeStruct((RH, T, T), f32)] + [jax.ShapeDtypeStruct((RH, T, LANES), f32)] * 4
        + [jax.ShapeDtypeStruct((RH, 8, LANES), f32)],
        compiler_params=_cp(("parallel",)),
        name="ret_tables",
    )(lg)


def _mod_kernel(a_ref, w_ref, b_ref, o_ref):
    a = _silu(a_ref[...])
    o_ref[...] = _dot(a, w_ref[...]) + b_ref[...]


def _modulation(a8, w, b):
    tn = MOD_TN
    n = w.shape[1]
    return pl.pallas_call(
        _mod_kernel,
        grid=(n // tn,),
        in_specs=[pl.BlockSpec((8, D), lambda j: (0, 0)),
                  pl.BlockSpec((D, tn), lambda j: (0, j)),
                  pl.BlockSpec((1, tn), lambda j: (0, j))],
        out_specs=pl.BlockSpec((8, tn), lambda j: (0, j)),
        out_shape=jax.ShapeDtypeStruct((8, n), f32),
        compiler_params=_cp(("parallel",)),
        name="modulation",
    )(a8, w, b)


def _rms_mod(x, nw, sc, sh):
    y = x * lax.rsqrt(jnp.mean(x * x, axis=-1, keepdims=True) + EPS)
    return (y * nw) * (1.0 + sc) + sh


def _norm_kernel(x_ref, nw_ref, sc_ref, sh_ref, o_ref):
    o_ref[...] = _rms_mod(x_ref[...], nw_ref[...], sc_ref[...], sh_ref[...]).astype(o_ref.dtype)


def _norm_mod(x, nw, sc, sh):
    L = x.shape[0]
    tm = min(L, NORM_TM)
    vec = pl.BlockSpec((1, D), lambda i: (0, 0))
    return pl.pallas_call(
        _norm_kernel,
        grid=(L // tm,),
        in_specs=[pl.BlockSpec((tm, D), lambda i: (i, 0)), vec, vec, vec],
        out_specs=pl.BlockSpec((tm, D), lambda i: (i, 0)),
        out_shape=jax.ShapeDtypeStruct((L, D), bf16),
        compiler_params=_cp(("parallel",)),
        name="norm_mod",
    )(x, nw, sc, sh)


def _proj_kernel(u_ref, w_ref, *rest, bw, tn, rope, scale_from, relu2):
    *rest, wb = rest
    if len(w_ref.shape) == 3:
        w_ref = w_ref.at[0]

    @pl.when(pl.program_id(1) == 0)
    def _():
        lane = lax.broadcasted_iota(jnp.int32, (CAST_ROWS, tn), 1) % HD
        for rb in range(D // CAST_ROWS):
            rows = slice(rb * CAST_ROWS, (rb + 1) * CAST_ROWS)
            w = w_ref[rows, :]
            if rope:
                qd = HD // 4
                w = jnp.where((lane >= qd) & (lane < 2 * qd), pltpu.roll(w, tn - qd, 1),
                              jnp.where((lane >= 2 * qd) & (lane < 3 * qd), pltpu.roll(w, qd, 1), w))
            wb[rows, :] = w.astype(bf16)

    if rope:
        rc_ref, rs_ref, cc_ref, cs_ref, o_ref = rest
        tm = u_ref.shape[0]
        scale = jnp.where(pl.program_id(0) >= scale_from, HD ** -0.5, 1.0).astype(f32)
        cos = _rope_rows(rc_ref[...], cc_ref[...], tm) * scale
        sin = _rope_rows(rs_ref[...], cs_ref[...], tm) * scale
    else:
        (o_ref,) = rest
    sw = min(tn, MXU_N)
    for nb in range(tn // sw):
        acc = _dot(u_ref[...], wb[:, nb * sw:(nb + 1) * sw])
        if relu2:
            acc = jnp.maximum(acc, 0.0)
            acc = acc * acc
        if rope:
            for hh in range(sw // HD):
                t = acc[:, hh * HD:(hh + 1) * HD]
                o_ref[nb * (sw // HD) + hh] = (t * cos + pltpu.roll(t, HD // 2, 1) * sin).astype(o_ref.dtype)
        elif bw and bw <= sw:
            for hh in range(sw // bw):
                o_ref[nb * (sw // bw) + hh] = acc[:, hh * bw:(hh + 1) * bw].astype(o_ref.dtype)
        elif bw:
            per = bw // sw
            o_ref[nb // per, :, (nb % per) * sw:(nb % per + 1) * sw] = acc.astype(o_ref.dtype)
        else:
            o_ref[:, nb * sw:(nb + 1) * sw] = acc.astype(o_ref.dtype)


def _project(u, w, *, col0, ncols, bw, tn, out_dtype=bf16, rope=None, scale_from=0, relu2=False, name):
    M = u.shape[0]
    N = ncols
    assert w.dtype == f32 and col0 % LANES == 0 and ncols % tn == 0
    jb = col0 // tn
    tm = min(M, PROJ_TM)
    if col0 % tn:
        w_spec = pl.BlockSpec((pl.Element(1), pl.Element(D), pl.Element(tn)),
                              lambda j, i: (0, 0, pl.multiple_of(col0 + j * tn, LANES)))
    else:
        w_spec = pl.BlockSpec((None, D, tn), lambda j, i: (0, 0, jb + j))
    in_specs = [pl.BlockSpec((tm, D), lambda j, i: (i, 0)), w_spec]
    args = [u, w]
    if rope is not None:
        in_specs += [pl.BlockSpec((tm // GRID_W, LANES), lambda j, i: (i, 0))] * 2
        in_specs += [pl.BlockSpec((GRID_W, LANES), lambda j, i: (0, 0))] * 2
        args += list(rope)
    if bw:
        out_spec = pl.BlockSpec((tn // bw, tm, bw), lambda j, i: (j, i, 0))
        out_shape = jax.ShapeDtypeStruct((N // bw, M, bw), out_dtype)
    else:
        out_spec = pl.BlockSpec((tm, tn), lambda j, i: (i, j))
        out_shape = jax.ShapeDtypeStruct((M, N), out_dtype)
    return pl.pallas_call(
        functools.partial(_proj_kernel, bw=bw, tn=tn, rope=rope is not None, scale_from=scale_from, relu2=relu2),
        grid=(N // tn, M // tm),
        in_specs=in_specs,
        out_specs=out_spec,
        out_shape=out_shape,
        scratch_shapes=[pltpu.VMEM((D, tn), bf16)],
        compiler_params=_cp(("parallel", "arbitrary")),
        name=name,
    )(*args)


def _shift_matrix():
    rows = lax.broadcasted_iota(jnp.int32, (4 * T, T + 2 * HALO), 0)
    cols = lax.broadcasted_iota(jnp.int32, (4 * T, T + 2 * HALO), 1)
    blk = rows // T
    tap = jnp.where(blk < KC // 2, blk, blk + 1)
    return jnp.where(cols == (rows - blk * T) + HALO + tap - KC // 2, 1.0, 0.0).astype(bf16)


def _conv_shift(shift, main, prev, nxt, has_prev, has_next):
    zero = jnp.zeros_like(prev)
    ext = jnp.concatenate([jnp.where(has_prev, prev, zero), main, jnp.where(has_next, nxt, zero)], axis=0)
    sh = _dot(shift, ext)
    return [sh[0:T], sh[T:2 * T], main.astype(f32), sh[2 * T:3 * T], sh[3 * T:4 * T]]


def _conv_taps(taps, cw, cb):
    acc = cb
    for j in range(KC):
        acc = acc + taps[j] * cw[j:j + 1, :]
    return _silu(acc)


def _decay_prologue(dt_ref, bias_ref, arow_ref):
    dt = _softplus(dt_ref[...] + bias_ref[...])
    la = dt * arow_ref[...]
    acs = _cumsum_rows(la, _tri_incl())
    tot = acs[T - 1:T, :]
    return dt, la, acs, tot


def _transpose_blocks(x):
    n = x.shape[1] // LANES
    return jnp.concatenate([x[:, b * LANES:(b + 1) * LANES].T for b in range(n)], axis=0)


def _ssd_state_update(hs_ref, g, xc, bc16, w_rows, dec_rows):
    xT = _transpose_blocks(xc)
    lhs = jnp.concatenate(
        [(xT[r * P:(r + 1) * P, :] * w_rows[r:r + 1, :]).astype(bf16) for r in range(R)], axis=0)
    upd = _dot(lhs, bc16)
    old = hs_ref[g]
    hs_ref[g] = jnp.concatenate(
        [old[r * P:(r + 1) * P, :] * dec_rows[r:r + 1, :] + upd[r * P:(r + 1) * P, :] for r in range(R)], axis=0)


def _ret_state_update(hr_ref, h, v16, k16, wcol, dec_row):
    vw = (v16.astype(f32) * wcol).T.astype(bf16)
    hr_ref[h] = hr_ref[h] * dec_row + _dot(vw, k16)


def _state_kernel(*refs, nc, fwd, emit, write_conv):
    it = iter(refs)
    k_ref, v_ref = next(it), next(it)
    xs_ref, xs_p, xs_n = next(it), next(it), next(it)
    b_ref, b_p, b_n = next(it), next(it), next(it)
    if write_conv:
        c_ref, c_p, c_n = next(it), next(it), next(it)
    dt_ref = next(it)
    cwx, cbx, cwb, cbb = next(it), next(it), next(it), next(it)
    if write_conv:
        cwc, cbc = next(it), next(it)
    bias_ref, arow_ref = next(it), next(it)
    rw_ref, rdec_ref = next(it), next(it)
    h0r_ref, h0s_ref = next(it), next(it)
    if emit:
        gr_ref, gs_ref = next(it), next(it)
    if write_conv:
        xo_ref, bco_ref = next(it), next(it)
    hfr_ref, hfs_ref = next(it), next(it)
    hr, hs, wT_s, decT_s = next(it), next(it), next(it), next(it)

    s = pl.program_id(0)
    c = s if fwd else nc - 1 - s
    has_prev = c > 0
    has_next = c < nc - 1

    @pl.when(s == 0)
    def _():
        hr[...] = h0r_ref[...]
        hs[...] = h0s_ref[...]

    if emit:
        gr_ref[0] = hr[...].astype(bf16)
        gs_ref[0] = hs[...].astype(bf16)

    dt, la, acs, tot = _decay_prologue(dt_ref, bias_ref, arow_ref)
    wexp = jnp.exp(tot - acs) if fwd else jnp.exp(acs - la)
    wT_s[...] = (wexp * dt).T
    decT_s[...] = jnp.broadcast_to(jnp.exp(tot), (T, 2 * SH)).T
    off = 0 if fwd else SH

    shift = _shift_matrix()

    def cat(a, b):
        return jnp.concatenate([a, b], axis=1)

    def shifted(g):
        tx = _conv_shift(shift, xs_ref[g], xs_p[g], xs_n[g], has_prev, has_next)
        if write_conv:
            tb = _conv_shift(shift, cat(b_ref[g], c_ref[g]), cat(b_p[g], c_p[g]), cat(b_n[g], c_n[g]),
                             has_prev, has_next)
        else:
            tb = _conv_shift(shift, b_ref[g], b_p[g], b_n[g], has_prev, has_next)
        return tx, tb

    nxt_taps = shifted(0)
    for g in range(NG):
        tx, tb = nxt_taps
        if g + 1 < NG:
            nxt_taps = shifted(g + 1)
        xc16 = _conv_taps(tx, cwx[g], cbx[g]).astype(bf16)
        if write_conv:
            bcc = _conv_taps(tb, cat(cwb[g], cwc[g]), cat(cbb[g], cbc[g])).astype(bf16)
            bc16 = bcc[:, :NS]
            xo_ref[g] = xc16
            bco_ref[g] = bcc
        else:
            bc16 = _conv_taps(tb, cwb[g], cbb[g]).astype(bf16)
        r0 = off + g * R
        _ssd_state_update(hs, g, xc16.astype(f32), bc16, wT_s[r0:r0 + R, :], decT_s[r0:r0 + R, :])

    dsel = 0 if fwd else 1

    for h in range(RH):
        _ret_state_update(hr, h, v_ref[h], k_ref[h], rw_ref[h], rdec_ref[h][dsel:dsel + 1, :])

    @pl.when(s == nc - 1)
    def _():
        hfr_ref[...] = hr[...]
        hfs_ref[...] = hs[...]


def _state_sweep(L, *, fwd, emit, write_conv, k, k_blk, v, v_blk, xs, xs_blk, bsrc, b_blk, c_blk, dt,
                 convw, bias, arow, rw, rdec, h0r, h0s):
    nc = L // T
    tb = T // HALO
    nrb = L // HALO

    def cidx(s):
        return s if fwd else nc - 1 - s

    def main(lead, n, width):
        return pl.BlockSpec((n, T, width), lambda s: (lead, cidx(s), 0))

    def prev(lead, n, width):
        return pl.BlockSpec((n, HALO, width), lambda s: (lead, jnp.maximum(cidx(s) * tb - 1, 0), 0))

    def nxt(lead, n, width):
        return pl.BlockSpec((n, HALO, width), lambda s: (lead, jnp.minimum((cidx(s) + 1) * tb, nrb - 1), 0))

    def whole(a):
        nd = a.ndim
        return pl.BlockSpec(a.shape, lambda s: (0,) * nd)

    cwx, cbx, cwb, cbb, cwc, cbc = convw
    in_specs = [main(k_blk, RH, HD), main(v_blk, RH, HD),
                main(xs_blk, NG, GW), prev(xs_blk, NG, GW), nxt(xs_blk, NG, GW),
                main(b_blk, NG, NS), prev(b_blk, NG, NS), nxt(b_blk, NG, NS)]
    args = [k, v, xs, xs, xs, bsrc, bsrc, bsrc]
    if write_conv:
        in_specs += [main(c_blk, NG, NS), prev(c_blk, NG, NS), nxt(c_blk, NG, NS)]
        args += [bsrc, bsrc, bsrc]
    in_specs += [pl.BlockSpec((T, 2 * SH), lambda s: (cidx(s), 0))]
    args += [dt]
    small = [cwx, cbx, cwb, cbb] + ([cwc, cbc] if write_conv else []) + [bias, arow, rw, rdec, h0r, h0s]
    in_specs += [whole(a) for a in small]
    args += small

    out_specs, out_shape = [], []
    if emit:
        out_specs += [pl.BlockSpec((1, RH, HD, HD), lambda s: (cidx(s), 0, 0, 0)),
                      pl.BlockSpec((1, NG, GW, NS), lambda s: (cidx(s), 0, 0, 0))]
        out_shape += [jax.ShapeDtypeStruct((nc, RH, HD, HD), bf16), jax.ShapeDtypeStruct((nc, NG, GW, NS), bf16)]
    if write_conv:
        out_specs += [pl.BlockSpec((NG, T, GW), lambda s: (0, cidx(s), 0)),
                      pl.BlockSpec((NG, T, 2 * NS), lambda s: (0, cidx(s), 0))]
        out_shape += [jax.ShapeDtypeStruct((NG, L, GW), bf16), jax.ShapeDtypeStruct((NG, L, 2 * NS), bf16)]
    out_specs += [pl.BlockSpec((RH, HD, HD), lambda s: (0, 0, 0)), pl.BlockSpec((NG, GW, NS), lambda s: (0, 0, 0))]
    out_shape += [jax.ShapeDtypeStruct((RH, HD, HD), f32), jax.ShapeDtypeStruct((NG, GW, NS), f32)]

    return pl.pallas_call(
        functools.partial(_state_kernel, nc=nc, fwd=fwd, emit=emit, write_conv=write_conv),
        grid=(nc,),
        in_specs=in_specs,
        out_specs=out_specs,
        out_shape=out_shape,
        scratch_shapes=[pltpu.VMEM((RH, HD, HD), f32), pltpu.VMEM((NG, GW, NS), f32),
                        pltpu.VMEM((2 * SH, T), f32), pltpu.VMEM((2 * SH, T), f32)],
        compiler_params=_cp(("arbitrary",)),
        name="state_sweep_" + ("f" if fwd else "b") + ("_emit" if emit else ""),
    )(*args)


def _fwd_kernel(qk_ref, vg_ref, z_ref, xs_ref, bc_ref, dt_ref, gret_ref, gssd_ref,
                wret_ref, ef_ref, eb_ref, wf_ref, rdec_ref, bias_ref, arow_ref, dexp_ref, snw_ref, rnw_ref,
                h0r_ref, h0s_ref, dsel_ref, esel_ref, yr_ref, ys_ref,
                hr, hs, ypre, ap_s, ep_s, apT_s, wT_s, decT_s):
    s = pl.program_id(0)

    @pl.when(s == 0)
    def _():
        hr[...] = h0r_ref[...]
        hs[...] = h0s_ref[...]

    dt, la, acs, tot = _decay_prologue(dt_ref, bias_ref, arow_ref)
    lane = lax.broadcasted_iota(jnp.int32, (T, 2 * SH), 1)
    is_f = lane < SH
    a1 = jnp.where(is_f, acs, acs - la)
    e1 = jnp.exp(jnp.where(is_f, acs, tot - (acs - la)))
    a2 = a1 * LOG2E
    ldt = jnp.maximum(jnp.log2(dt), LOG2_FLOOR)
    for k, part in enumerate(_split3(a2)):
        ap_s[k] = part
    for k, part in enumerate(_split3(jnp.where(is_f, ldt - a2, ldt + a2))):
        apT_s[k] = part.T
    for k, part in enumerate(_split3(e1)):
        ep_s[k] = part
    wT_s[...] = (jnp.exp(tot - acs) * dt).T
    decT_s[...] = jnp.broadcast_to(jnp.exp(tot), (T, 2 * SH)).T

    ii = lax.broadcasted_iota(jnp.int32, (T, T), 0)
    jj = lax.broadcasted_iota(jnp.int32, (T, T), 1)
    lower = jj <= ii
    part_masks = [((lane >= k * R) & (lane < (k + 1) * R)) | ((lane >= SH + k * R) & (lane < SH + (k + 1) * R))
                  for k in range(3)]
    ones_lanes = (lane >= ONES_LANE) & (lane < ONES_LANE + 3)
    row16 = lax.broadcasted_iota(jnp.int32, (16, T), 0)
    lane_p = lax.broadcasted_iota(jnp.int32, (T, 2 * P), 1)

    def pack(parts_ref, g, with_ones):
        acc = jnp.where(ones_lanes, 1.0, 0.0) if with_ones else jnp.zeros((T, 2 * SH), f32)
        for k in range(3):
            shift = (2 * SH - g * R + k * R) % (2 * SH)
            acc = acc + jnp.where(part_masks[k], pltpu.roll(parts_ref[k], shift, 1), 0.0)
        return acc.astype(bf16)

    def front(g):
        b16 = bc_ref[g, :, 0:NS]
        c16 = bc_ref[g, :, NS:2 * NS]
        sc = _nt(c16, b16)
        r0 = g * R
        lhs_a = pack(ap_s, g, True)
        lhs_e = pack(ep_s, g, False)
        cf_rows = [apT_s[k, r0:r0 + R, :] for k in range(3)]
        cb_rows = [apT_s[k, SH + r0:SH + r0 + R, :] for k in range(3)]
        hcat = jnp.concatenate([hs[g].astype(bf16), gssd_ref[0, g]], axis=0)
        ycross = _nt(c16, hcat)
        efb = _dot(lhs_e, esel_ref[...])
        dmats = []
        for r in range(R):
            lo = jnp.zeros((16, T), f32)
            up = jnp.zeros((16, T), f32)
            for k in range(3):
                lo = jnp.where(row16 == k, cf_rows[k][r:r + 1, :], lo)
                up = jnp.where(row16 == k, cb_rows[k][r:r + 1, :], up)
            dyn = jnp.concatenate([lo, up], axis=1).astype(bf16)
            rhs = jnp.concatenate([dsel_ref[r, 0:ONES_LANE, :], dyn, dsel_ref[r, ONES_LANE + 16:, :]], axis=0)
            dmats.append(_dot(lhs_a, rhs))
        return b16, sc, ycross, efb, dmats

    def back(g, staged, ssq):
        b16, sc, ycross, efb, dmats = staged
        r0 = g * R
        xg = xs_ref[g]
        zg = z_ref[g].astype(f32)
        dg = dexp_ref[g]
        ms = [(sc * jnp.exp2(jnp.where(lower, dm[:, :T], dm[:, T:]))).astype(bf16) for dm in dmats]
        ys_parts = []
        for t in range(R // 2):
            xp = xg[:, t * 2 * P:(t + 1) * 2 * P]
            zero = jnp.zeros_like(xp)
            rhs = jnp.concatenate([jnp.where(lane_p < P, xp, zero), jnp.where(lane_p >= P, xp, zero)], axis=0)
            ys_parts.append(_dot(jnp.concatenate([ms[2 * t], ms[2 * t + 1]], axis=1), rhs))
        yg = jnp.concatenate(ys_parts, axis=1)
        yg = yg + ycross[:, :GW] * efb[:, :GW] + ycross[:, GW:] * efb[:, GW:]
        yg = (yg + xg.astype(f32) * dg) * _silu(zg)
        ypre[g] = yg
        ssq = ssq + jnp.sum(yg * yg, axis=1, keepdims=True)
        _ssd_state_update(hs, g, xg.astype(f32), b16, wT_s[r0:r0 + R, :], decT_s[r0:r0 + R, :])
        return ssq

    ssq = jnp.zeros((T, 1), f32)
    staged = front(0)
    for g in range(NG):
        cur = staged
        if g + 1 < NG:
            staged = front(g + 1)
        ssq = back(g, cur, ssq)
    rs = lax.rsqrt(ssq * (1.0 / SI) + EPS)
    for g in range(NG):
        ys_ref[:, g * GW:(g + 1) * GW] = ((ypre[g] * rs) * snw_ref[g]).astype(ys_ref.dtype)

    zero_h = jnp.zeros((T, HD), bf16)

    def blockdiag(a, b):
        return jnp.concatenate([jnp.concatenate([a, zero_h], axis=1), jnp.concatenate([zero_h, b], axis=1)], axis=0)

    def ret_front(t):
        pair = (2 * t, 2 * t + 1)
        q16 = [qk_ref[h] for h in pair]
        k16 = [qk_ref[RH + h] for h in pair]
        s2 = _nt(jnp.concatenate(q16, axis=1), blockdiag(*k16))
        cross = []
        for n, h in enumerate(pair):
            qf = q16[n].astype(f32)
            lc = jnp.concatenate([(qf * ef_ref[h]).astype(bf16), (qf * eb_ref[h]).astype(bf16)], axis=1)
            hcat = jnp.concatenate([hr[h].astype(bf16), gret_ref[0, h]], axis=1)
            cross.append(_nt(lc, hcat))
        return k16, s2, cross

    ret_staged = ret_front(0)
    for t in range(RH // 2):
        pair = (2 * t, 2 * t + 1)
        k16, s2, cross = ret_staged
        if t + 1 < RH // 2:
            ret_staged = ret_front(t + 1)
        v16 = [vg_ref[h] for h in pair]
        m2 = (s2 * jnp.concatenate([wret_ref[h] for h in pair], axis=1)).astype(bf16)
        y2 = _dot(m2, blockdiag(*v16))
        for n, h in enumerate(pair):
            y = y2[:, n * HD:(n + 1) * HD] + cross[n]
            mu = jnp.mean(y, axis=-1, keepdims=True)
            d = y - mu
            yn = d * lax.rsqrt(jnp.mean(d * d, axis=-1, keepdims=True) + EPS)
            gg = vg_ref[RH + h].astype(f32)
            yr_ref[:, h * HD:(h + 1) * HD] = ((yn * rnw_ref[h]) * _silu(gg)).astype(yr_ref.dtype)
            _ret_state_update(hr, h, v16[n], k16[n], wf_ref[h], rdec_ref[h][0:1, :])


def _fwd_sweep(L, qk, vg, zx, xc, bcc, dt, gret, gssd, tabs, bias, arow, dexp, snw, rnw, h0r, h0s):
    nc = L // T
    wret, ef, eb, wf, _, rdec = tabs

    def blk(lead, n, width):
        return pl.BlockSpec((n, T, width), lambda s: (lead, s, 0))

    def whole(a):
        nd = a.ndim
        return pl.BlockSpec(a.shape, lambda s: (0,) * nd)

    dsel, esel = _selector_constants()
    small = [wret, ef, eb, wf, rdec, bias, arow, dexp, snw, rnw, h0r, h0s, dsel, esel]
    in_specs = [blk(0, 2 * RH, HD), blk(0, 2 * RH, HD), blk(0, NG, GW), blk(0, NG, GW), blk(0, NG, 2 * NS),
                pl.BlockSpec((T, 2 * SH), lambda s: (s, 0)),
                pl.BlockSpec((1, RH, HD, HD), lambda s: (s, 0, 0, 0)),
                pl.BlockSpec((1, NG, GW, NS), lambda s: (s, 0, 0, 0))] + [whole(a) for a in small]
    return pl.pallas_call(
        _fwd_kernel,
        grid=(nc,),
        in_specs=in_specs,
        out_specs=[pl.BlockSpec((T, D), lambda s: (s, 0)), pl.BlockSpec((T, SI), lambda s: (s, 0))],
        out_shape=[jax.ShapeDtypeStruct((L, D), bf16), jax.ShapeDtypeStruct((L, SI), bf16)],
        scratch_shapes=[pltpu.VMEM((RH, HD, HD), f32), pltpu.VMEM((NG, GW, NS), f32), pltpu.VMEM((NG, T, GW), f32),
                        pltpu.VMEM((3, T, 2 * SH), f32), pltpu.VMEM((3, T, 2 * SH), f32),
                        pltpu.VMEM((3, 2 * SH, T), f32)]
        + [pltpu.VMEM((2 * SH, T), f32)] * 2,
        compiler_params=_cp(("arbitrary",)),
        name="fwd_sweep",
    )(qk, vg, zx, xc, bcc, dt, gret, gssd, *small)


def _branch_out_kernel(yr_ref, ys_ref, wr_ref, ws_ref, gr_ref, gs_ref, o_ref):
    for nb in range(o_ref.shape[1] // MXU_N):
        cols = slice(nb * MXU_N, (nb + 1) * MXU_N)
        acc_r = _dot(yr_ref[...], wr_ref[:, cols])
        acc_s = _dot(ys_ref[...], ws_ref[:, cols])
        m = _sigmoid(gr_ref[:, cols].astype(f32)) * acc_r + _sigmoid(gs_ref[:, cols].astype(f32)) * acc_s
        o_ref[:, cols] = m.astype(o_ref.dtype)


def _branch_out(yr, ys, wr, ws, gates):
    L = yr.shape[0]
    tm, tn = min(L, BRANCH_TM), BRANCH_TN
    nj = D // tn
    return pl.pallas_call(
        _branch_out_kernel,
        grid=(L // tm, nj),
        in_specs=[pl.BlockSpec((tm, D), lambda i, j: (i, 0)),
                  pl.BlockSpec((tm, SI), lambda i, j: (i, 0)),
                  pl.BlockSpec((D, tn), lambda i, j: (0, j)),
                  pl.BlockSpec((SI, tn), lambda i, j: (0, j)),
                  pl.BlockSpec((tm, tn), lambda i, j: (i, j)),
                  pl.BlockSpec((tm, tn), lambda i, j: (i, nj + j))],
        out_specs=pl.BlockSpec((tm, tn), lambda i, j: (i, j)),
        out_shape=jax.ShapeDtypeStruct((L, D), bf16),
        compiler_params=_cp(("parallel", "parallel")),
        name="branch_out",
    )(yr, ys, wr, ws, gates, gates)


def _resid_kernel(m_ref, w_ref, x_ref, g_ref, nw_ref, sc_ref, sh_ref, h_ref, f_ref):
    h = x_ref[...] + g_ref[...] * _dot(m_ref[...], w_ref[...])
    h_ref[...] = h
    f_ref[...] = _rms_mod(h, nw_ref[...], sc_ref[...], sh_ref[...]).astype(f_ref.dtype)


def _mix_residual(m, w, x, gate, nw, sc, sh):
    L = m.shape[0]
    tm = min(L, MIX_TM)
    vec = pl.BlockSpec((1, D), lambda i: (0, 0))
    row = pl.BlockSpec((tm, D), lambda i: (i, 0))
    return pl.pallas_call(
        _resid_kernel,
        grid=(L // tm,),
        in_specs=[row, pl.BlockSpec((D, D), lambda i: (0, 0)), row, vec, vec, vec, vec],
        out_specs=[row, row],
        out_shape=[jax.ShapeDtypeStruct((L, D), f32), jax.ShapeDtypeStruct((L, D), bf16)],
        compiler_params=_cp(("parallel",)),
        name="mix_residual",
    )(m, w, x, gate, nw, sc, sh)


def _mlp2_kernel(a_ref, w_ref, h_ref, g_ref, fw_ref, o_ref, *, nk):
    kk = pl.program_id(1)

    @pl.when(kk == 0)
    def _():
        o_ref[...] = jnp.zeros_like(o_ref)

    for nb in range(D // MLP2_STRIP):
        cols = slice(nb * MLP2_STRIP, (nb + 1) * MLP2_STRIP)
        o_ref[:, cols] += _dot(a_ref[...], w_ref[:, cols])

    @pl.when(kk == nk - 1)
    def _():
        h2 = h_ref[...] + g_ref[...] * o_ref[...]
        y = h2 * lax.rsqrt(jnp.mean(h2 * h2, axis=-1, keepdims=True) + EPS)
        o_ref[...] = y * fw_ref[...]


def _mlp2_final(a, w, h, gate, fw):
    L = a.shape[0]
    tm, tk = min(L, MLP2_TM), MLP2_TK
    nk = DFF // tk
    vec = pl.BlockSpec((1, D), lambda i, k: (0, 0))
    return pl.pallas_call(
        functools.partial(_mlp2_kernel, nk=nk),
        grid=(L // tm, nk),
        in_specs=[pl.BlockSpec((tm, tk), lambda i, k: (i, k)),
                  pl.BlockSpec((tk, D), lambda i, k: (k, 0)),
                  pl.BlockSpec((tm, D), lambda i, k: (i, 0)), vec, vec],
        out_specs=pl.BlockSpec((tm, D), lambda i, k: (i, 0)),
        out_shape=jax.ShapeDtypeStruct((L, D), f32),
        compiler_params=_cp(("parallel", "arbitrary")),
        name="mlp2_final",
    )(a, w, h, gate, fw)


def _split_conv(conv_w, conv_b):
    def grp(w, b, width):
        return (jnp.transpose(w.reshape(KC, NG, width), (1, 0, 2)), b.reshape(NG, 1, width))
    cwx, cbx = grp(conv_w[:, :SI], conv_b[:SI], GW)
    cwb, cbb = grp(conv_w[:, SI:SI + NG * NS], conv_b[SI:SI + NG * NS], NS)
    cwc, cbc = grp(conv_w[:, SI + NG * NS:], conv_b[SI + NG * NS:], NS)
    return cwx, cbx, cwb, cbb, cwc, cbc


def kernel(x, c, ctx, c_ctx, w_mod, b_mod, norm1_w, w_in, conv_w, conv_b, ret_decay_logit, ret_norm_w,
           ssd_a_log, ssd_dt_bias, ssd_d, ssd_norm_w, w_ret_out, w_ssd_out, w_o, norm2_w, w_mlp1, w_mlp2,
           final_norm_w):
    depth = w_mod.shape[0]
    assert depth == 1 and x.shape[0] == 1 and x.shape[2] == D
    L = x.shape[1]
    Lc = ctx.shape[1]
    assert L % T == 0 and Lc % T == 0 and L % GRID_W == 0
    xl = x[0]
    xcx = ctx[0]
    ly = 0

    a8 = jnp.zeros((8, D), f32).at[0].set(c[0]).at[1].set(c_ctx)
    mod = _modulation(a8, w_mod[ly], b_mod[ly][None, :])
    sh_a, sc_a, g_a, sh_f, sc_f, g_f = [mod[0:1, i * D:(i + 1) * D] for i in range(6)]
    csh_a, csc_a = mod[1:2, 0:D], mod[1:2, D:2 * D]

    o_q, o_k, o_v, o_z, o_x = 0, D, 2 * D, 4 * D, 4 * D + SI
    o_b = o_x + SI
    o_dt = o_b + 2 * NG * NS
    o_gate = o_dt + 2 * SH

    def proj(src, col, n, **kw):
        return _project(src, w_in, col0=col, ncols=n, **kw)

    convw = _split_conv(conv_w[ly], conv_b[ly])
    bias = ssd_dt_bias[ly].reshape(1, 2 * SH)
    arow = (-jnp.exp(ssd_a_log[ly])).reshape(1, 2 * SH)
    dexp = jnp.repeat(ssd_d[ly], P).reshape(NG, 1, GW)
    snw = ssd_norm_w[ly].reshape(NG, 1, GW)
    rnw = ret_norm_w[ly].reshape(RH, 1, HD)
    nw1 = norm1_w[ly][None, :]

    tabs = _ret_tables(ret_decay_logit[ly])
    _, _, _, wf_t, wb_t, rdec = tabs
    zr = jnp.zeros((RH, HD, HD), f32)
    zs = jnp.zeros((NG, GW, NS), f32)

    uc = _norm_mod(xcx, nw1, csc_a, csh_a)
    ones_c, zeros_c = jnp.ones((Lc // GRID_W, LANES), f32), jnp.zeros((Lc // GRID_W, LANES), f32)
    ident = (ones_c, zeros_c, jnp.ones((GRID_W, LANES), f32), jnp.zeros((GRID_W, LANES), f32))
    kc = proj(uc, o_k, D, bw=HD, tn=CTX_TN, rope=ident, scale_from=0, name="ctx_k")
    vc = proj(uc, o_v, D, bw=HD, tn=CTX_TN, name="ctx_v")
    bcx = proj(uc, o_b, NG * NS, bw=NS, tn=CTX_TN, name="ctx_b")
    xsc = proj(uc, o_x, SI, bw=GW, tn=CTX_TN, name="ctx_x")
    dtc = proj(uc, o_dt, 2 * SH, bw=0, tn=2 * SH, out_dtype=f32, name="ctx_dt")
    common = dict(k=kc, k_blk=0, v=vc, v_blk=0, xs=xsc, xs_blk=0, bsrc=bcx, b_blk=0, c_blk=0, dt=dtc,
                  convw=convw, bias=bias, arow=arow, rdec=rdec, h0r=zr, h0s=zs)
    cfr, cfs = _state_sweep(Lc, fwd=True, emit=False, write_conv=False, rw=wf_t, **common)
    cbr, cbs = _state_sweep(Lc, fwd=False, emit=False, write_conv=False, rw=wb_t, **common)

    u = _norm_mod(xl, nw1, sc_a, sh_a)
    rope = _rope_tables(L)
    qk = proj(u, o_q, 2 * D, bw=HD, tn=PROJ_TN, rope=rope, scale_from=D // PROJ_TN, name="lat_qk")
    vg = proj(u, o_v, 2 * D, bw=HD, tn=PROJ_TN, name="lat_vg")
    zx = proj(u, o_z, 2 * SI, bw=GW, tn=PROJ_TN, name="lat_zx")
    bcl = proj(u, o_b, 2 * NG * NS, bw=NS, tn=PROJ_TN, name="lat_bc")
    dtl = proj(u, o_dt, 2 * SH, bw=0, tn=2 * SH, out_dtype=f32, name="lat_dt")
    gates = proj(u, o_gate, 2 * D, bw=0, tn=PROJ_TN, name="lat_gates")

    gret, gssd, xcv, bccv, _, _ = _state_sweep(
        L, fwd=False, emit=True, write_conv=True, k=qk, k_blk=1, v=vg, v_blk=0, xs=zx, xs_blk=1, bsrc=bcl,
        b_blk=0, c_blk=1, dt=dtl, convw=convw, bias=bias, arow=arow, rw=wb_t, rdec=rdec,
        h0r=cbr, h0s=cbs)
    yr, ys = _fwd_sweep(L, qk, vg, zx, xcv, bccv, dtl, gret, gssd, tabs, bias, arow, dexp, snw, rnw, cfr, cfs)

    m = _branch_out(yr, ys, w_ret_out[ly].astype(bf16), w_ssd_out[ly].astype(bf16), gates)
    h1, f = _mix_residual(m, w_o[ly].astype(bf16), xl, g_a, norm2_w[ly][None, :], sc_f, sh_f)
    a = _project(f, w_mlp1, col0=0, ncols=DFF, bw=0, tn=PROJ_TN, relu2=True, name="mlp1")
    out = _mlp2_final(a, w_mlp2[ly].astype(bf16), h1, g_f, final_norm_w[None, :])
    return out[None]
```

```python
import functools
import math

import numpy as np
import jax
import jax.numpy as jnp
from jax import lax
from jax.experimental import pallas as pl
from jax.experimental.pallas import tpu as pltpu

f32 = jnp.float32
bf16 = jnp.bfloat16

D = 2048
T = 128
GRID_W = 64
HD = 128
RH = D // HD
SI = 2 * D
P = 64
SH = SI // P
NS = 128
NG = 8
R = SH // NG
GW = R * P
KC = 5
DFF = 4 * D
ROPE_BASE = 10000.0
EPS = 1e-6
LANES = 128
MXU_N = 256
CAST_ROWS = 256
HALO = 16
LOG2E = 1.0 / math.log(2.0)
LOG2_FLOOR = -300.0
ONES_LANE = 96
VMEM_LIMIT = 56 * 1024 * 1024

PROJ_TM, PROJ_TN = 2048, 1024
CTX_TN = 512
NORM_TM = 1024
MOD_TN = 1024
BRANCH_TM, BRANCH_TN = 1024, 512
MIX_TM = 512
MLP2_TM, MLP2_TK, MLP2_STRIP = 1024, 1024, 512


def _cp(sem):
    return pltpu.CompilerParams(dimension_semantics=sem, vmem_limit_bytes=VMEM_LIMIT)


def _nt(a, b):
    return lax.dot_general(a, b, (((1,), (1,)), ((), ())), preferred_element_type=f32)


def _dot(a, b):
    return jnp.dot(a, b, preferred_element_type=f32)


def _sigmoid(x):
    return 0.5 + 0.5 * jnp.tanh(0.5 * x)


def _silu(x):
    h = 0.5 * x
    return h + h * jnp.tanh(h)


def _softplus(x):
    return jnp.maximum(x, 0.0) + jnp.log1p(jnp.exp(-jnp.abs(x)))


def _log_sigmoid(x):
    return jnp.minimum(x, 0.0) - jnp.log1p(jnp.exp(-jnp.abs(x)))


def _split3(x):
    p1 = x.astype(bf16).astype(f32)
    r1 = x - p1
    p2 = r1.astype(bf16).astype(f32)
    p3 = (r1 - p2).astype(bf16).astype(f32)
    return p1, p2, p3


def _cumsum_rows(la, tri):
    a1, a2, a3 = _split3(la)
    return _dot(tri, a1.astype(bf16)) + _dot(tri, a2.astype(bf16)) + _dot(tri, a3.astype(bf16))


def _selector_constants():
    dsel = np.zeros((R, 2 * SH, 2 * T), np.float32)
    esel = np.zeros((2 * SH, 2 * GW), np.float32)
    for r in range(R):
        for k in range(3):
            dsel[r, k * R + r, :T] = 1.0
            dsel[r, SH + k * R + r, T:] = -1.0
            esel[k * R + r, r * P:(r + 1) * P] = 1.0
            esel[SH + k * R + r, GW + r * P:GW + (r + 1) * P] = 1.0
    return jnp.asarray(dsel, bf16), jnp.asarray(esel, bf16)


def _tri_incl():
    ii = lax.broadcasted_iota(jnp.int32, (T, T), 0)
    jj = lax.broadcasted_iota(jnp.int32, (T, T), 1)
    return jnp.where(jj <= ii, 1.0, 0.0).astype(bf16)


def _rope_tab_kernel(rc_ref, rs_ref, cc_ref, cs_ref):
    def tab(n):
        idx = lax.broadcasted_iota(jnp.int32, (n, LANES), 0).astype(f32)
        lane = lax.broadcasted_iota(jnp.int32, (n, LANES), 1)
        freqs = jnp.exp((lane % (HD // 4)).astype(f32) * (-math.log(ROPE_BASE) / (HD // 4)))
        ang = idx * freqs
        s = jnp.sin(ang)
        return jnp.cos(ang), jnp.where(lane < HD // 2, -s, s)

    rc_ref[...], rs_ref[...] = tab(rc_ref.shape[0])
    cc_ref[...], cs_ref[...] = tab(GRID_W)


def _rope_tables(L):
    rows = L // GRID_W
    shapes = [jax.ShapeDtypeStruct((rows, LANES), f32)] * 2 + [jax.ShapeDtypeStruct((GRID_W, LANES), f32)] * 2
    return pl.pallas_call(_rope_tab_kernel, out_shape=shapes, name="rope_tables")()


def _rope_rows(row_tab, col_tab, tm):
    lane = lax.broadcasted_iota(jnp.int32, (GRID_W, LANES), 1)
    use_row = (lane % (HD // 2)) < HD // 4
    return jnp.concatenate([jnp.where(use_row, row_tab[r:r + 1, :], col_tab) for r in range(tm // GRID_W)], axis=0)


def _ret_tab_kernel(lg_ref, w_ref, ef_ref, eb_ref, wf_ref, wb_ref, dec_ref):
    lf = _log_sigmoid(lg_ref[0])
    lb = _log_sigmoid(lg_ref[1])
    ii = lax.broadcasted_iota(jnp.int32, (T, T), 0)
    jj = lax.broadcasted_iota(jnp.int32, (T, T), 1)
    dl = (ii - jj).astype(f32)
    w_ref[...] = jnp.exp(jnp.where(jj <= ii, dl * lf, -dl * lb))
    idx = lax.broadcasted_iota(jnp.int32, (T, LANES), 0).astype(f32)
    ef_ref[...] = jnp.exp((idx + 1.0) * lf)
    eb_ref[...] = jnp.exp((T - idx) * lb)
    wf_ref[...] = jnp.exp((T - 1.0 - idx) * lf)
    wb_ref[...] = jnp.exp(idx * lb)
    srow = lax.broadcasted_iota(jnp.int32, (8, LANES), 0)
    dec_ref[...] = jnp.where(srow == 0, jnp.exp(T * lf), jnp.exp(T * lb))


def _ret_tables(logit):
    lg = jnp.broadcast_to(logit[:, :, None, None], (2, RH, 1, LANES))
    tab = pl.BlockSpec((None, T, LANES), lambda h: (h, 0, 0))
    return pl.pallas_call(
        _ret_tab_kernel,
        grid=(RH,),
        in_specs=[pl.BlockSpec((2, None, 1, LANES), lambda h: (0, h, 0, 0))],
        out_specs=[tab, tab, tab, tab, tab, pl.BlockSpec((None, 8, LANES), lambda h: (h, 0, 0))],
        out_shape=[jax.ShapeDtypeStruct((RH, T, T), f32)] + [jax.ShapeDtypeStruct((RH, T, LANES), f32)] * 4
        + [jax.ShapeDtypeStruct((RH, 8, LANES), f32)],
        compiler_params=_cp(("parallel",)),
        name="ret_tables",
    )(lg)


def _mod_kernel(a_ref, w_ref, b_ref, o_ref):
    a = _silu(a_ref[...])
    o_ref[...] = _dot(a, w_ref[...]) + b_ref[...]


def _modulation(a8, w, b):
    tn = MOD_TN
    n = w.shape[1]
    return pl.pallas_call(
        _mod_kernel,
        grid=(n // tn,),
        in_specs=[pl.BlockSpec((8, D), lambda j: (0, 0)),
                  pl.BlockSpec((D, tn), lambda j: (0, j)),
                  pl.BlockSpec((1, tn), lambda j: (0, j))],
        out_specs=pl.BlockSpec((8, tn), lambda j: (0, j)),
        out_shape=jax.ShapeDtypeStruct((8, n), f32),
        compiler_params=_cp(("parallel",)),
        name="modulation",
    )(a8, w, b)


def _rms_mod(x, nw, sc, sh):
    y = x * lax.rsqrt(jnp.mean(x * x, axis=-1, keepdims=True) + EPS)
    return (y * nw) * (1.0 + sc) + sh


def _norm_kernel(x_ref, nw_ref, sc_ref, sh_ref, o_ref):
    o_ref[...] = _rms_mod(x_ref[...], nw_ref[...], sc_ref[...], sh_ref[...]).astype(o_ref.dtype)


def _norm_mod(x, nw, sc, sh):
    L = x.shape[0]
    tm = min(L, NORM_TM)
    vec = pl.BlockSpec((1, D), lambda i: (0, 0))
    return pl.pallas_call(
        _norm_kernel,
        grid=(L // tm,),
        in_specs=[pl.BlockSpec((tm, D), lambda i: (i, 0)), vec, vec, vec],
        out_specs=pl.BlockSpec((tm, D), lambda i: (i, 0)),
        out_shape=jax.ShapeDtypeStruct((L, D), bf16),
        compiler_params=_cp(("parallel",)),
        name="norm_mod",
    )(x, nw, sc, sh)


def _proj_kernel(u_ref, w_ref, *rest, bw, tn, rope, scale_from, relu2):
    *rest, wb = rest
    if len(w_ref.shape) == 3:
        w_ref = w_ref.at[0]

    @pl.when(pl.program_id(1) == 0)
    def _():
        lane = lax.broadcasted_iota(jnp.int32, (CAST_ROWS, tn), 1) % HD
        for rb in range(D // CAST_ROWS):
            rows = slice(rb * CAST_ROWS, (rb + 1) * CAST_ROWS)
            w = w_ref[rows, :]
            if rope:
                qd = HD // 4
                w = jnp.where((lane >= qd) & (lane < 2 * qd), pltpu.roll(w, tn - qd, 1),
                              jnp.where((lane >= 2 * qd) & (lane < 3 * qd), pltpu.roll(w, qd, 1), w))
            wb[rows, :] = w.astype(bf16)

    if rope:
        rc_ref, rs_ref, cc_ref, cs_ref, o_ref = rest
        tm = u_ref.shape[0]
        scale = jnp.where(pl.program_id(0) >= scale_from, HD ** -0.5, 1.0).astype(f32)
        cos = _rope_rows(rc_ref[...], cc_ref[...], tm) * scale
        sin = _rope_rows(rs_ref[...], cs_ref[...], tm) * scale
    else:
        (o_ref,) = rest
    sw = min(tn, MXU_N)
    for nb in range(tn // sw):
        acc = _dot(u_ref[...], wb[:, nb * sw:(nb + 1) * sw])
        if relu2:
            acc = jnp.maximum(acc, 0.0)
            acc = acc * acc
        if rope:
            for hh in range(sw // HD):
                t = acc[:, hh * HD:(hh + 1) * HD]
                o_ref[nb * (sw // HD) + hh] = (t * cos + pltpu.roll(t, HD // 2, 1) * sin).astype(o_ref.dtype)
        elif bw and bw <= sw:
            for hh in range(sw // bw):
                o_ref[nb * (sw // bw) + hh] = acc[:, hh * bw:(hh + 1) * bw].astype(o_ref.dtype)
        elif bw:
            per = bw // sw
            o_ref[nb // per, :, (nb % per) * sw:(nb % per + 1) * sw] = acc.astype(o_ref.dtype)
        else:
            o_ref[:, nb * sw:(nb + 1) * sw] = acc.astype(o_ref.dtype)


def _project(u, w, *, col0, ncols, bw, tn, out_dtype=bf16, rope=None, scale_from=0, relu2=False, name):
    M = u.shape[0]
    N = ncols
    assert w.dtype == f32 and col0 % LANES == 0 and ncols % tn == 0
    jb = col0 // tn
    tm = min(M, PROJ_TM)
    if col0 % tn:
        w_spec = pl.BlockSpec((pl.Element(1), pl.Element(D), pl.Element(tn)),
                              lambda j, i: (0, 0, pl.multiple_of(col0 + j * tn, LANES)))
    else:
        w_spec = pl.BlockSpec((None, D, tn), lambda j, i: (0, 0, jb + j))
    in_specs = [pl.BlockSpec((tm, D), lambda j, i: (i, 0)), w_spec]
    args = [u, w]
    if rope is not None:
        in_specs += [pl.BlockSpec((tm // GRID_W, LANES), lambda j, i: (i, 0))] * 2
        in_specs += [pl.BlockSpec((GRID_W, LANES), lambda j, i: (0, 0))] * 2
        args += list(rope)
    if bw:
        out_spec = pl.BlockSpec((tn // bw, tm, bw), lambda j, i: (j, i, 0))
        out_shape = jax.ShapeDtypeStruct((N // bw, M, bw), out_dtype)
    else:
        out_spec = pl.BlockSpec((tm, tn), lambda j, i: (i, j))
        out_shape = jax.ShapeDtypeStruct((M, N), out_dtype)
    return pl.pallas_call(
        functools.partial(_proj_kernel, bw=bw, tn=tn, rope=rope is not None, scale_from=scale_from, relu2=relu2),
        grid=(N // tn, M // tm),
        in_specs=in_specs,
        out_specs=out_spec,
        out_shape=out_shape,
        scratch_shapes=[pltpu.VMEM((D, tn), bf16)],
        compiler_params=_cp(("parallel", "arbitrary")),
        name=name,
    )(*args)


def _shift_matrix():
    rows = lax.broadcasted_iota(jnp.int32, (4 * T, T + 2 * HALO), 0)
    cols = lax.broadcasted_iota(jnp.int32, (4 * T, T + 2 * HALO), 1)
    blk = rows // T
    tap = jnp.where(blk < KC // 2, blk, blk + 1)
    return jnp.where(cols == (rows - blk * T) + HALO + tap - KC // 2, 1.0, 0.0).astype(bf16)


def _conv_shift(shift, main, prev, nxt, has_prev, has_next):
    zero = jnp.zeros_like(prev)
    ext = jnp.concatenate([jnp.where(has_prev, prev, zero), main, jnp.where(has_next, nxt, zero)], axis=0)
    sh = _dot(shift, ext)
    return [sh[0:T], sh[T:2 * T], main.astype(f32), sh[2 * T:3 * T], sh[3 * T:4 * T]]


def _conv_taps(taps, cw, cb):
    acc = cb
    for j in range(KC):
        acc = acc + taps[j] * cw[j:j + 1, :]
    return _silu(acc)


def _decay_prologue(dt_ref, bias_ref, arow_ref):
    dt = _softplus(dt_ref[...] + bias_ref[...])
    la = dt * arow_ref[...]
    acs = _cumsum_rows(la, _tri_incl())
    tot = acs[T - 1:T, :]
    return dt, la, acs, tot


def _transpose_blocks(x):
    n = x.shape[1] // LANES
    return jnp.concatenate([x[:, b * LANES:(b + 1) * LANES].T for b in range(n)], axis=0)


def _ssd_state_update(hs_ref, g, xc, bc16, w_rows, dec_rows):
    xT = _transpose_blocks(xc)
    lhs = jnp.concatenate(
        [(xT[r * P:(r + 1) * P, :] * w_rows[r:r + 1, :]).astype(bf16) for r in range(R)], axis=0)
    upd = _dot(lhs, bc16)
    old = hs_ref[g]
    hs_ref[g] = jnp.concatenate(
        [old[r * P:(r + 1) * P, :] * dec_rows[r:r + 1, :] + upd[r * P:(r + 1) * P, :] for r in range(R)], axis=0)


def _ret_state_update(hr_ref, h, v16, k16, wcol, dec_row):
    vw = (v16.astype(f32) * wcol).T.astype(bf16)
    hr_ref[h] = hr_ref[h] * dec_row + _dot(vw, k16)


def _state_kernel(*refs, nc, fwd, emit, write_conv):
    it = iter(refs)
    k_ref, v_ref = next(it), next(it)
    xs_ref, xs_p, xs_n = next(it), next(it), next(it)
    b_ref, b_p, b_n = next(it), next(it), next(it)
    if write_conv:
        c_ref, c_p, c_n = next(it), next(it), next(it)
    dt_ref = next(it)
    cwx, cbx, cwb, cbb = next(it), next(it), next(it), next(it)
    if write_conv:
        cwc, cbc = next(it), next(it)
    bias_ref, arow_ref = next(it), next(it)
    rw_ref, rdec_ref = next(it), next(it)
    h0r_ref, h0s_ref = next(it), next(it)
    if emit:
        gr_ref, gs_ref = next(it), next(it)
    if write_conv:
        xo_ref, bco_ref = next(it), next(it)
    hfr_ref, hfs_ref = next(it), next(it)
    hr, hs, wT_s, decT_s = next(it), next(it), next(it), next(it)

    s = pl.program_id(0)
    c = s if fwd else nc - 1 - s
    has_prev = c > 0
    has_next = c < nc - 1

    @pl.when(s == 0)
    def _():
        hr[...] = h0r_ref[...]
        hs[...] = h0s_ref[...]

    if emit:
        gr_ref[0] = hr[...].astype(bf16)
        gs_ref[0] = hs[...].astype(bf16)

    dt, la, acs, tot = _decay_prologue(dt_ref, bias_ref, arow_ref)
    wexp = jnp.exp(tot - acs) if fwd else jnp.exp(acs - la)
    wT_s[...] = (wexp * dt).T
    decT_s[...] = jnp.broadcast_to(jnp.exp(tot), (T, 2 * SH)).T
    off = 0 if fwd else SH

    shift = _shift_matrix()

    def cat(a, b):
        return jnp.concatenate([a, b], axis=1)

    def shifted(g):
        tx = _conv_shift(shift, xs_ref[g], xs_p[g], xs_n[g], has_prev, has_next)
        if write_conv:
            tb = _conv_shift(shift, cat(b_ref[g], c_ref[g]), cat(b_p[g], c_p[g]), cat(b_n[g], c_n[g]),
                             has_prev, has_next)
        else:
            tb = _conv_shift(shift, b_ref[g], b_p[g], b_n[g], has_prev, has_next)
        return tx, tb

    nxt_taps = shifted(0)
    for g in range(NG):
        tx, tb = nxt_taps
        if g + 1 < NG:
            nxt_taps = shifted(g + 1)
        xc16 = _conv_taps(tx, cwx[g], cbx[g]).astype(bf16)
        if write_conv:
            bcc = _conv_taps(tb, cat(cwb[g], cwc[g]), cat(cbb[g], cbc[g])).astype(bf16)
            bc16 = bcc[:, :NS]
            xo_ref[g] = xc16
            bco_ref[g] = bcc
        else:
            bc16 = _conv_taps(tb, cwb[g], cbb[g]).astype(bf16)
        r0 = off + g * R
        _ssd_state_update(hs, g, xc16.astype(f32), bc16, wT_s[r0:r0 + R, :], decT_s[r0:r0 + R, :])

    dsel = 0 if fwd else 1

    for h in range(RH):
        _ret_state_update(hr, h, v_ref[h], k_ref[h], rw_ref[h], rdec_ref[h][dsel:dsel + 1, :])

    @pl.when(s == nc - 1)
    def _():
        hfr_ref[...] = hr[...]
        hfs_ref[...] = hs[...]


def _state_sweep(L, *, fwd, emit, write_conv, k, k_blk, v, v_blk, xs, xs_blk, bsrc, b_blk, c_blk, dt,
                 convw, bias, arow, rw, rdec, h0r, h0s):
    nc = L // T
    tb = T // HALO
    nrb = L // HALO

    def cidx(s):
        return s if fwd else nc - 1 - s

    def main(lead, n, width):
        return pl.BlockSpec((n, T, width), lambda s: (lead, cidx(s), 0))

    def prev(lead, n, width):
        return pl.BlockSpec((n, HALO, width), lambda s: (lead, jnp.maximum(cidx(s) * tb - 1, 0), 0))

    def nxt(lead, n, width):
        return pl.BlockSpec((n, HALO, width), lambda s: (lead, jnp.minimum((cidx(s) + 1) * tb, nrb - 1), 0))

    def whole(a):
        nd = a.ndim
        return pl.BlockSpec(a.shape, lambda s: (0,) * nd)

    cwx, cbx, cwb, cbb, cwc, cbc = convw
    in_specs = [main(k_blk, RH, HD), main(v_blk, RH, HD),
                main(xs_blk, NG, GW), prev(xs_blk, NG, GW), nxt(xs_blk, NG, GW),
                main(b_blk, NG, NS), prev(b_blk, NG, NS), nxt(b_blk, NG, NS)]
    args = [k, v, xs, xs, xs, bsrc, bsrc, bsrc]
    if write_conv:
        in_specs += [main(c_blk, NG, NS), prev(c_blk, NG, NS), nxt(c_blk, NG, NS)]
        args += [bsrc, bsrc, bsrc]
    in_specs += [pl.BlockSpec((T, 2 * SH), lambda s: (cidx(s), 0))]
    args += [dt]
    small = [cwx, cbx, cwb, cbb] + ([cwc, cbc] if write_conv else []) + [bias, arow, rw, rdec, h0r, h0s]
    in_specs += [whole(a) for a in small]
    args += small

    out_specs, out_shape = [], []
    if emit:
        out_specs += [pl.BlockSpec((1, RH, HD, HD), lambda s: (cidx(s), 0, 0, 0)),
                      pl.BlockSpec((1, NG, GW, NS), lambda s: (cidx(s), 0, 0, 0))]
        out_shape += [jax.ShapeDtypeStruct((nc, RH, HD, HD), bf16), jax.ShapeDtypeStruct((nc, NG, GW, NS), bf16)]
    if write_conv:
        out_specs += [pl.BlockSpec((NG, T, GW), lambda s: (0, cidx(s), 0)),
                      pl.BlockSpec((NG, T, 2 * NS), lambda s: (0, cidx(s), 0))]
        out_shape += [jax.ShapeDtypeStruct((NG, L, GW), bf16), jax.ShapeDtypeStruct((NG, L, 2 * NS), bf16)]
    out_specs += [pl.BlockSpec((RH, HD, HD), lambda s: (0, 0, 0)), pl.BlockSpec((NG, GW, NS), lambda s: (0, 0, 0))]
    out_shape += [jax.ShapeDtypeStruct((RH, HD, HD), f32), jax.ShapeDtypeStruct((NG, GW, NS), f32)]

    return pl.pallas_call(
        functools.partial(_state_kernel, nc=nc, fwd=fwd, emit=emit, write_conv=write_conv),
        grid=(nc,),
        in_specs=in_specs,
        out_specs=out_specs,
        out_shape=out_shape,
        scratch_shapes=[pltpu.VMEM((RH, HD, HD), f32), pltpu.VMEM((NG, GW, NS), f32),
                        pltpu.VMEM((2 * SH, T), f32), pltpu.VMEM((2 * SH, T), f32)],
        compiler_params=_cp(("arbitrary",)),
        name="state_sweep_" + ("f" if fwd else "b") + ("_emit" if emit else ""),
    )(*args)


def _fwd_kernel(qk_ref, vg_ref, z_ref, xs_ref, bc_ref, dt_ref, gret_ref, gssd_ref,
                wret_ref, ef_ref, eb_ref, wf_ref, rdec_ref, bias_ref, arow_ref, dexp_ref, snw_ref, rnw_ref,
                h0r_ref, h0s_ref, dsel_ref, esel_ref, yr_ref, ys_ref,
                hr, hs, ypre, ap_s, ep_s, apT_s, wT_s, decT_s):
    s = pl.program_id(0)

    @pl.when(s == 0)
    def _():
        hr[...] = h0r_ref[...]
        hs[...] = h0s_ref[...]

    dt, la, acs, tot = _decay_prologue(dt_ref, bias_ref, arow_ref)
    lane = lax.broadcasted_iota(jnp.int32, (T, 2 * SH), 1)
    is_f = lane < SH
    a1 = jnp.where(is_f, acs, acs - la)
    e1 = jnp.exp(jnp.where(is_f, acs, tot - (acs - la)))
    a2 = a1 * LOG2E
    ldt = jnp.maximum(jnp.log2(dt), LOG2_FLOOR)
    for k, part in enumerate(_split3(a2)):
        ap_s[k] = part
    for k, part in enumerate(_split3(jnp.where(is_f, ldt - a2, ldt + a2))):
        apT_s[k] = part.T
    for k, part in enumerate(_split3(e1)):
        ep_s[k] = part
    wT_s[...] = (jnp.exp(tot - acs) * dt).T
    decT_s[...] = jnp.broadcast_to(jnp.exp(tot), (T, 2 * SH)).T

    ii = lax.broadcasted_iota(jnp.int32, (T, T), 0)
    jj = lax.broadcasted_iota(jnp.int32, (T, T), 1)
    lower = jj <= ii
    part_masks = [((lane >= k * R) & (lane < (k + 1) * R)) | ((lane >= SH + k * R) & (lane < SH + (k + 1) * R))
                  for k in range(3)]
    ones_lanes = (lane >= ONES_LANE) & (lane < ONES_LANE + 3)
    row16 = lax.broadcasted_iota(jnp.int32, (16, T), 0)
    lane_p = lax.broadcasted_iota(jnp.int32, (T, 2 * P), 1)

    def pack(parts_ref, g, with_ones):
        acc = jnp.where(ones_lanes, 1.0, 0.0) if with_ones else jnp.zeros((T, 2 * SH), f32)
        for k in range(3):
            shift = (2 * SH - g * R + k * R) % (2 * SH)
            acc = acc + jnp.where(part_masks[k], pltpu.roll(parts_ref[k], shift, 1), 0.0)
        return acc.astype(bf16)

    def front(g):
        b16 = bc_ref[g, :, 0:NS]
        c16 = bc_ref[g, :, NS:2 * NS]
        sc = _nt(c16, b16)
        r0 = g * R
        lhs_a = pack(ap_s, g, True)
        lhs_e = pack(ep_s, g, False)
        cf_rows = [apT_s[k, r0:r0 + R, :] for k in range(3)]
        cb_rows = [apT_s[k, SH + r0:SH + r0 + R, :] for k in range(3)]
        hcat = jnp.concatenate([hs[g].astype(bf16), gssd_ref[0, g]], axis=0)
        ycross = _nt(c16, hcat)
        efb = _dot(lhs_e, esel_ref[...])
        dmats = []
        for r in range(R):
            lo = jnp.zeros((16, T), f32)
            up = jnp.zeros((16, T), f32)
            for k in range(3):
                lo = jnp.where(row16 == k, cf_rows[k][r:r + 1, :], lo)
                up = jnp.where(row16 == k, cb_rows[k][r:r + 1, :], up)
            dyn = jnp.concatenate([lo, up], axis=1).astype(bf16)
            rhs = jnp.concatenate([dsel_ref[r, 0:ONES_LANE, :], dyn, dsel_ref[r, ONES_LANE + 16:, :]], axis=0)
            dmats.append(_dot(lhs_a, rhs))
        return b16, sc, ycross, efb, dmats

    def back(g, staged, ssq):
        b16, sc, ycross, efb, dmats = staged
        r0 = g * R
        xg = xs_ref[g]
        zg = z_ref[g].astype(f32)
        dg = dexp_ref[g]
        ms = [(sc * jnp.exp2(jnp.where(lower, dm[:, :T], dm[:, T:]))).astype(bf16) for dm in dmats]
        ys_parts = []
        for t in range(R // 2):
            xp = xg[:, t * 2 * P:(t + 1) * 2 * P]
            zero = jnp.zeros_like(xp)
            rhs = jnp.concatenate([jnp.where(lane_p < P, xp, zero), jnp.where(lane_p >= P, xp, zero)], axis=0)
            ys_parts.append(_dot(jnp.concatenate([ms[2 * t], ms[2 * t + 1]], axis=1), rhs))
        yg = jnp.concatenate(ys_parts, axis=1)
        yg = yg + ycross[:, :GW] * efb[:, :GW] + ycross[:, GW:] * efb[:, GW:]
        yg = (yg + xg.astype(f32) * dg) * _silu(zg)
        ypre[g] = yg
        ssq = ssq + jnp.sum(yg * yg, axis=1, keepdims=True)
        _ssd_state_update(hs, g, xg.astype(f32), b16, wT_s[r0:r0 + R, :], decT_s[r0:r0 + R, :])
        return ssq

    ssq = jnp.zeros((T, 1), f32)
    staged = front(0)
    for g in range(NG):
        cur = staged
        if g + 1 < NG:
            staged = front(g + 1)
        ssq = back(g, cur, ssq)
    rs = lax.rsqrt(ssq * (1.0 / SI) + EPS)
    for g in range(NG):
        ys_ref[:, g * GW:(g + 1) * GW] = ((ypre[g] * rs) * snw_ref[g]).astype(ys_ref.dtype)

    zero_h = jnp.zeros((T, HD), bf16)

    def blockdiag(a, b):
        return jnp.concatenate([jnp.concatenate([a, zero_h], axis=1), jnp.concatenate([zero_h, b], axis=1)], axis=0)

    def ret_front(t):
        pair = (2 * t, 2 * t + 1)
        q16 = [qk_ref[h] for h in pair]
        k16 = [qk_ref[RH + h] for h in pair]
        s2 = _nt(jnp.concatenate(q16, axis=1), blockdiag(*k16))
        cross = []
        for n, h in enumerate(pair):
            qf = q16[n].astype(f32)
            lc = jnp.concatenate([(qf * ef_ref[h]).astype(bf16), (qf * eb_ref[h]).astype(bf16)], axis=1)
            hcat = jnp.concatenate([hr[h].astype(bf16), gret_ref[0, h]], axis=1)
            cross.append(_nt(lc, hcat))
        return k16, s2, cross

    ret_staged = ret_front(0)
    for t in range(RH // 2):
        pair = (2 * t, 2 * t + 1)
        k16, s2, cross = ret_staged
        if t + 1 < RH // 2:
            ret_staged = ret_front(t + 1)
        v16 = [vg_ref[h] for h in pair]
        m2 = (s2 * jnp.concatenate([wret_ref[h] for h in pair], axis=1)).astype(bf16)
        y2 = _dot(m2, blockdiag(*v16))
        for n, h in enumerate(pair):
            y = y2[:, n * HD:(n + 1) * HD] + cross[n]
            mu = jnp.mean(y, axis=-1, keepdims=True)
            d = y - mu
            yn = d * lax.rsqrt(jnp.mean(d * d, axis=-1, keepdims=True) + EPS)
            gg = vg_ref[RH + h].astype(f32)
            yr_ref[:, h * HD:(h + 1) * HD] = ((yn * rnw_ref[h]) * _silu(gg)).astype(yr_ref.dtype)
            _ret_state_update(hr, h, v16[n], k16[n], wf_ref[h], rdec_ref[h][0:1, :])


def _fwd_sweep(L, qk, vg, zx, xc, bcc, dt, gret, gssd, tabs, bias, arow, dexp, snw, rnw, h0r, h0s):
    nc = L // T
    wret, ef, eb, wf, _, rdec = tabs

    def blk(lead, n, width):
        return pl.BlockSpec((n, T, width), lambda s: (lead, s, 0))

    def whole(a):
        nd = a.ndim
        return pl.BlockSpec(a.shape, lambda s: (0,) * nd)

    dsel, esel = _selector_constants()
    small = [wret, ef, eb, wf, rdec, bias, arow, dexp, snw, rnw, h0r, h0s, dsel, esel]
    in_specs = [blk(0, 2 * RH, HD), blk(0, 2 * RH, HD), blk(0, NG, GW), blk(0, NG, GW), blk(0, NG, 2 * NS),
                pl.BlockSpec((T, 2 * SH), lambda s: (s, 0)),
                pl.BlockSpec((1, RH, HD, HD), lambda s: (s, 0, 0, 0)),
                pl.BlockSpec((1, NG, GW, NS), lambda s: (s, 0, 0, 0))] + [whole(a) for a in small]
    return pl.pallas_call(
        _fwd_kernel,
        grid=(nc,),
        in_specs=in_specs,
        out_specs=[pl.BlockSpec((T, D), lambda s: (s, 0)), pl.BlockSpec((T, SI), lambda s: (s, 0))],
        out_shape=[jax.ShapeDtypeStruct((L, D), bf16), jax.ShapeDtypeStruct((L, SI), bf16)],
        scratch_shapes=[pltpu.VMEM((RH, HD, HD), f32), pltpu.VMEM((NG, GW, NS), f32), pltpu.VMEM((NG, T, GW), f32),
                        pltpu.VMEM((3, T, 2 * SH), f32), pltpu.VMEM((3, T, 2 * SH), f32),
                        pltpu.VMEM((3, 2 * SH, T), f32)]
        + [pltpu.VMEM((2 * SH, T), f32)] * 2,
        compiler_params=_cp(("arbitrary",)),
        name="fwd_sweep",
    )(qk, vg, zx, xc, bcc, dt, gret, gssd, *small)


def _branch_out_kernel(yr_ref, ys_ref, wr_ref, ws_ref, gr_ref, gs_ref, o_ref):
    for nb in range(o_ref.shape[1] // MXU_N):
        cols = slice(nb * MXU_N, (nb + 1) * MXU_N)
        acc_r = _dot(yr_ref[...], wr_ref[:, cols])
        acc_s = _dot(ys_ref[...], ws_ref[:, cols])
        m = _sigmoid(gr_ref[:, cols].astype(f32)) * acc_r + _sigmoid(gs_ref[:, cols].astype(f32)) * acc_s
        o_ref[:, cols] = m.astype(o_ref.dtype)


def _branch_out(yr, ys, wr, ws, gates):
    L = yr.shape[0]
    tm, tn = min(L, BRANCH_TM), BRANCH_TN
    nj = D // tn
    return pl.pallas_call(
        _branch_out_kernel,
        grid=(L // tm, nj),
        in_specs=[pl.BlockSpec((tm, D), lambda i, j: (i, 0)),
                  pl.BlockSpec((tm, SI), lambda i, j: (i, 0)),
                  pl.BlockSpec((D, tn), lambda i, j: (0, j)),
                  pl.BlockSpec((SI, tn), lambda i, j: (0, j)),
                  pl.BlockSpec((tm, tn), lambda i, j: (i, j)),
                  pl.BlockSpec((tm, tn), lambda i, j: (i, nj + j))],
        out_specs=pl.BlockSpec((tm, tn), lambda i, j: (i, j)),
        out_shape=jax.ShapeDtypeStruct((L, D), bf16),
        compiler_params=_cp(("parallel", "parallel")),
        name="branch_out",
    )(yr, ys, wr, ws, gates, gates)


def _resid_kernel(m_ref, w_ref, x_ref, g_ref, nw_ref, sc_ref, sh_ref, h_ref, f_ref):
    h = x_ref[...] + g_ref[...] * _dot(m_ref[...], w_ref[...])
    h_ref[...] = h
    f_ref[...] = _rms_mod(h, nw_ref[...], sc_ref[...], sh_ref[...]).astype(f_ref.dtype)


def _mix_residual(m, w, x, gate, nw, sc, sh):
    L = m.shape[0]
    tm = min(L, MIX_TM)
    vec = pl.BlockSpec((1, D), lambda i: (0, 0))
    row = pl.BlockSpec((tm, D), lambda i: (i, 0))
    return pl.pallas_call(
        _resid_kernel,
        grid=(L // tm,),
        in_specs=[row, pl.BlockSpec((D, D), lambda i: (0, 0)), row, vec, vec, vec, vec],
        out_specs=[row, row],
        out_shape=[jax.ShapeDtypeStruct((L, D), f32), jax.ShapeDtypeStruct((L, D), bf16)],
        compiler_params=_cp(("parallel",)),
        name="mix_residual",
    )(m, w, x, gate, nw, sc, sh)


def _mlp2_kernel(a_ref, w_ref, h_ref, g_ref, fw_ref, o_ref, *, nk):
    kk = pl.program_id(1)

    @pl.when(kk == 0)
    def _():
        o_ref[...] = jnp.zeros_like(o_ref)

    for nb in range(D // MLP2_STRIP):
        cols = slice(nb * MLP2_STRIP, (nb + 1) * MLP2_STRIP)
        o_ref[:, cols] += _dot(a_ref[...], w_ref[:, cols])

    @pl.when(kk == nk - 1)
    def _():
        h2 = h_ref[...] + g_ref[...] * o_ref[...]
        y = h2 * lax.rsqrt(jnp.mean(h2 * h2, axis=-1, keepdims=True) + EPS)
        o_ref[...] = y * fw_ref[...]


def _mlp2_final(a, w, h, gate, fw):
    L = a.shape[0]
    tm, tk = min(L, MLP2_TM), MLP2_TK
    nk = DFF // tk
    vec = pl.BlockSpec((1, D), lambda i, k: (0, 0))
    return pl.pallas_call(
        functools.partial(_mlp2_kernel, nk=nk),
        grid=(L // tm, nk),
        in_specs=[pl.BlockSpec((tm, tk), lambda i, k: (i, k)),
                  pl.BlockSpec((tk, D), lambda i, k: (k, 0)),
                  pl.BlockSpec((tm, D), lambda i, k: (i, 0)), vec, vec],
        out_specs=pl.BlockSpec((tm, D), lambda i, k: (i, 0)),
        out_shape=jax.ShapeDtypeStruct((L, D), f32),
        compiler_params=_cp(("parallel", "arbitrary")),
        name="mlp2_final",
    )(a, w, h, gate, fw)


def _split_conv(conv_w, conv_b):
    def grp(w, b, width):
        return (jnp.transpose(w.reshape(KC, NG, width), (1, 0, 2)), b.reshape(NG, 1, width))
    cwx, cbx = grp(conv_w[:, :SI], conv_b[:SI], GW)
    cwb, cbb = grp(conv_w[:, SI:SI + NG * NS], conv_b[SI:SI + NG * NS], NS)
    cwc, cbc = grp(conv_w[:, SI + NG * NS:], conv_b[SI + NG * NS:], NS)
    return cwx, cbx, cwb, cbb, cwc, cbc


def kernel(x, c, ctx, c_ctx, w_mod, b_mod, norm1_w, w_in, conv_w, conv_b, ret_decay_logit, ret_norm_w,
           ssd_a_log, ssd_dt_bias, ssd_d, ssd_norm_w, w_ret_out, w_ssd_out, w_o, norm2_w, w_mlp1, w_mlp2,
           final_norm_w):
    depth = w_mod.shape[0]
    assert depth == 1 and x.shape[0] == 1 and x.shape[2] == D
    L = x.shape[1]
    Lc = ctx.shape[1]
    assert L % T == 0 and Lc % T == 0 and L % GRID_W == 0
    xl = x[0]
    xcx = ctx[0]
    ly = 0

    a8 = jnp.zeros((8, D), f32).at[0].set(c[0]).at[1].set(c_ctx)
    mod = _modulation(a8, w_mod[ly], b_mod[ly][None, :])
    sh_a, sc_a, g_a, sh_f, sc_f, g_f = [mod[0:1, i * D:(i + 1) * D] for i in range(6)]
    csh_a, csc_a = mod[1:2, 0:D], mod[1:2, D:2 * D]

    o_q, o_k, o_v, o_z, o_x = 0, D, 2 * D, 4 * D, 4 * D + SI
    o_b = o_x + SI
    o_dt = o_b + 2 * NG * NS
    o_gate = o_dt + 2 * SH

    def proj(src, col, n, **kw):
        return _project(src, w_in, col0=col, ncols=n, **kw)

    convw = _split_conv(conv_w[ly], conv_b[ly])
    bias = ssd_dt_bias[ly].reshape(1, 2 * SH)
    arow = (-jnp.exp(ssd_a_log[ly])).reshape(1, 2 * SH)
    dexp = jnp.repeat(ssd_d[ly], P).reshape(NG, 1, GW)
    snw = ssd_norm_w[ly].reshape(NG, 1, GW)
    rnw = ret_norm_w[ly].reshape(RH, 1, HD)
    nw1 = norm1_w[ly][None, :]

    tabs = _ret_tables(ret_decay_logit[ly])
    _, _, _, wf_t, wb_t, rdec = tabs
    zr = jnp.zeros((RH, HD, HD), f32)
    zs = jnp.zeros((NG, GW, NS), f32)

    uc = _norm_mod(xcx, nw1, csc_a, csh_a)
    ones_c, zeros_c = jnp.ones((Lc // GRID_W, LANES), f32), jnp.zeros((Lc // GRID_W, LANES), f32)
    ident = (ones_c, zeros_c, jnp.ones((GRID_W, LANES), f32), jnp.zeros((GRID_W, LANES), f32))
    kc = proj(uc, o_k, D, bw=HD, tn=CTX_TN, rope=ident, scale_from=0, name="ctx_k")
    vc = proj(uc, o_v, D, bw=HD, tn=CTX_TN, name="ctx_v")
    bcx = proj(uc, o_b, NG * NS, bw=NS, tn=CTX_TN, name="ctx_b")
    xsc = proj(uc, o_x, SI, bw=GW, tn=CTX_TN, name="ctx_x")
    dtc = proj(uc, o_dt, 2 * SH, bw=0, tn=2 * SH, out_dtype=f32, name="ctx_dt")
    common = dict(k=kc, k_blk=0, v=vc, v_blk=0, xs=xsc, xs_blk=0, bsrc=bcx, b_blk=0, c_blk=0, dt=dtc,
                  convw=convw, bias=bias, arow=arow, rdec=rdec, h0r=zr, h0s=zs)
    cfr, cfs = _state_sweep(Lc, fwd=True, emit=False, write_conv=False, rw=wf_t, **common)
    cbr, cbs = _state_sweep(Lc, fwd=False, emit=False, write_conv=False, rw=wb_t, **common)

    u = _norm_mod(xl, nw1, sc_a, sh_a)
    rope = _rope_tables(L)
    qk = proj(u, o_q, 2 * D, bw=HD, tn=PROJ_TN, rope=rope, scale_from=D // PROJ_TN, name="lat_qk")
    vg = proj(u, o_v, 2 * D, bw=HD, tn=PROJ_TN, name="lat_vg")
    zx = proj(u, o_z, 2 * SI, bw=GW, tn=PROJ_TN, name="lat_zx")
    bcl = proj(u, o_b, 2 * NG * NS, bw=NS, tn=PROJ_TN, name="lat_bc")
    dtl = proj(u, o_dt, 2 * SH, bw=0, tn=2 * SH, out_dtype=f32, name="lat_dt")
    gates = proj(u, o_gate, 2 * D, bw=0, tn=PROJ_TN, name="lat_gates")

    gret, gssd, xcv, bccv, _, _ = _state_sweep(
        L, fwd=False, emit=True, write_conv=True, k=qk, k_blk=1, v=vg, v_blk=0, xs=zx, xs_blk=1, bsrc=bcl,
        b_blk=0, c_blk=1, dt=dtl, convw=convw, bias=bias, arow=arow, rw=wb_t, rdec=rdec,
        h0r=cbr, h0s=cbs)
    yr, ys = _fwd_sweep(L, qk, vg, zx, xcv, bccv, dtl, gret, gssd, tabs, bias, arow, dexp, snw, rnw, cfr, cfs)

    m = _branch_out(yr, ys, w_ret_out[ly].astype(bf16), w_ssd_out[ly].astype(bf16), gates)
    h1, f = _mix_residual(m, w_o[ly].astype(bf16), xl, g_a, norm2_w[ly][None, :], sc_f, sh_f)
    a = _project(f, w_mlp1, col0=0, ncols=DFF, bw=0, tn=PROJ_TN, relu2=True, name="mlp1")
    out = _mlp2_final(a, w_mlp2[ly].astype(bf16), h1, g_f, final_norm_w[None, :])
    return out[None]
```

```python
import functools
import math

import numpy as np
import jax
import jax.numpy as jnp
from jax import lax
from jax.experimental import pallas as pl
from jax.experimental.pallas import tpu as pltpu

f32 = jnp.float32
bf16 = jnp.bfloat16

D = 2048
T = 128
GRID_W = 64
HD = 128
RH = D // HD
SI = 2 * D
P = 64
SH = SI // P
NS = 128
NG = 8
R = SH // NG
GW = R * P
KC = 5
DFF = 4 * D
ROPE_BASE = 10000.0
EPS = 1e-6
LANES = 128
MXU_N = 256
CAST_ROWS = 256
HALO = 16
LOG2E = 1.0 / math.log(2.0)
LOG2_FLOOR = -300.0
ONES_LANE = 96
VMEM_LIMIT = 56 * 1024 * 1024
SWEEP_VMEM_LIMIT = 58 * 1024 * 1024

PROJ_TM, PROJ_TN = 2048, 1024
CTX_TN = 512
NORM_TM = 1024
MOD_TN = 1024
BRANCH_TM, BRANCH_TN = 1024, 512
MIX_TM = 512
MLP2_TM, MLP2_TK, MLP2_STRIP = 1024, 1024, 512


def _cp(sem, limit=VMEM_LIMIT):
    return pltpu.CompilerParams(dimension_semantics=sem, vmem_limit_bytes=limit)


def _nt(a, b):
    return lax.dot_general(a, b, (((1,), (1,)), ((), ())), preferred_element_type=f32)


def _dot(a, b):
    return jnp.dot(a, b, preferred_element_type=f32)


def _sigmoid(x):
    return 0.5 + 0.5 * jnp.tanh(0.5 * x)


def _silu(x):
    h = 0.5 * x
    return h + h * jnp.tanh(h)


def _softplus(x):
    return jnp.maximum(x, 0.0) + jnp.log1p(jnp.exp(-jnp.abs(x)))


def _log_sigmoid(x):
    return jnp.minimum(x, 0.0) - jnp.log1p(jnp.exp(-jnp.abs(x)))


def _split3(x):
    p1 = x.astype(bf16).astype(f32)
    r1 = x - p1
    p2 = r1.astype(bf16).astype(f32)
    p3 = (r1 - p2).astype(bf16).astype(f32)
    return p1, p2, p3


def _cumsum_rows(la, tri):
    a1, a2, a3 = _split3(la)
    return _dot(tri, a1.astype(bf16)) + _dot(tri, a2.astype(bf16)) + _dot(tri, a3.astype(bf16))


def _selector_constants():
    dsel = np.zeros((R, 2 * SH, 2 * T), np.float32)
    esel = np.zeros((2 * SH, 2 * GW), np.float32)
    for r in range(R):
        for k in range(3):
            dsel[r, k * R + r, :T] = 1.0
            dsel[r, SH + k * R + r, T:] = -1.0
            esel[k * R + r, r * P:(r + 1) * P] = 1.0
            esel[SH + k * R + r, GW + r * P:GW + (r + 1) * P] = 1.0
    return jnp.asarray(dsel, bf16), jnp.asarray(esel, bf16)


def _tri_incl():
    ii = lax.broadcasted_iota(jnp.int32, (T, T), 0)
    jj = lax.broadcasted_iota(jnp.int32, (T, T), 1)
    return jnp.where(jj <= ii, 1.0, 0.0).astype(bf16)


def _rope_tab_kernel(rc_ref, rs_ref, cc_ref, cs_ref):
    def tab(n):
        idx = lax.broadcasted_iota(jnp.int32, (n, LANES), 0).astype(f32)
        lane = lax.broadcasted_iota(jnp.int32, (n, LANES), 1)
        freqs = jnp.exp((lane % (HD // 4)).astype(f32) * (-math.log(ROPE_BASE) / (HD // 4)))
        ang = idx * freqs
        s = jnp.sin(ang)
        return jnp.cos(ang), jnp.where(lane < HD // 2, -s, s)

    rc_ref[...], rs_ref[...] = tab(rc_ref.shape[0])
    cc_ref[...], cs_ref[...] = tab(GRID_W)


def _rope_tables(L):
    rows = L // GRID_W
    shapes = [jax.ShapeDtypeStruct((rows, LANES), f32)] * 2 + [jax.ShapeDtypeStruct((GRID_W, LANES), f32)] * 2
    return pl.pallas_call(_rope_tab_kernel, out_shape=shapes, name="rope_tables")()


def _rope_rows(row_tab, col_tab, tm):
    lane = lax.broadcasted_iota(jnp.int32, (GRID_W, LANES), 1)
    use_row = (lane % (HD // 2)) < HD // 4
    return jnp.concatenate([jnp.where(use_row, row_tab[r:r + 1, :], col_tab) for r in range(tm // GRID_W)], axis=0)


def _ret_tab_kernel(lg_ref, w_ref, ef_ref, eb_ref, wf_ref, wb_ref, dec_ref):
    lf = _log_sigmoid(lg_ref[0])
    lb = _log_sigmoid(lg_ref[1])
    ii = lax.broadcasted_iota(jnp.int32, (T, T), 0)
    jj = lax.broadcasted_iota(jnp.int32, (T, T), 1)
    dl = (ii - jj).astype(f32)
    w_ref[...] = jnp.exp(jnp.where(jj <= ii, dl * lf, -dl * lb))
    idx = lax.broadcasted_iota(jnp.int32, (T, LANES), 0).astype(f32)
    ef_ref[...] = jnp.exp((idx + 1.0) * lf)
    eb_ref[...] = jnp.exp((T - idx) * lb)
    wf_ref[...] = jnp.exp((T - 1.0 - idx) * lf)
    wb_ref[...] = jnp.exp(idx * lb)
    srow = lax.broadcasted_iota(jnp.int32, (8, LANES), 0)
    dec_ref[...] = jnp.where(srow == 0, jnp.exp(T * lf), jnp.exp(T * lb))


def _ret_tables(logit):
    lg = jnp.broadcast_to(logit[:, :, None, None], (2, RH, 1, LANES))
    tab = pl.BlockSpec((None, T, LANES), lambda h: (h, 0, 0))
    return pl.pallas_call(
        _ret_tab_kernel,
        grid=(RH,),
        in_specs=[pl.BlockSpec((2, None, 1, LANES), lambda h: (0, h, 0, 0))],
        out_specs=[tab, tab, tab, tab, tab, pl.BlockSpec((None, 8, LANES), lambda h: (h, 0, 0))],
        out_shape=[jax.ShapeDtypeStruct((RH, T, T), f32)] + [jax.ShapeDtypeStruct((RH, T, LANES), f32)] * 4
        + [jax.ShapeDtypeStruct((RH, 8, LANES), f32)],
        compiler_params=_cp(("parallel",)),
        name="ret_tables",
    )(lg)


def _mod_kernel(a_ref, w_ref, b_ref, o_ref):
    a = _silu(a_ref[...])
    o_ref[...] = _dot(a, w_ref[...]) + b_ref[...]


def _modulation(a8, w, b):
    tn = MOD_TN
    n = w.shape[1]
    return pl.pallas_call(
        _mod_kernel,
        grid=(n // tn,),
        in_specs=[pl.BlockSpec((8, D), lambda j: (0, 0)),
                  pl.BlockSpec((D, tn), lambda j: (0, j)),
                  pl.BlockSpec((1, tn), lambda j: (0, j))],
        out_specs=pl.BlockSpec((8, tn), lambda j: (0, j)),
        out_shape=jax.ShapeDtypeStruct((8, n), f32),
        compiler_params=_cp(("parallel",)),
        name="modulation",
    )(a8, w, b)


def _rms_mod(x, nw, sc, sh):
    y = x * lax.rsqrt(jnp.mean(x * x, axis=-1, keepdims=True) + EPS)
    return (y * nw) * (1.0 + sc) + sh


def _norm_kernel(x_ref, nw_ref, sc_ref, sh_ref, o_ref):
    o_ref[...] = _rms_mod(x_ref[...], nw_ref[...], sc_ref[...], sh_ref[...]).astype(o_ref.dtype)


def _norm_mod(x, nw, sc, sh):
    L = x.shape[0]
    tm = min(L, NORM_TM)
    vec = pl.BlockSpec((1, D), lambda i: (0, 0))
    return pl.pallas_call(
        _norm_kernel,
        grid=(L // tm,),
        in_specs=[pl.BlockSpec((tm, D), lambda i: (i, 0)), vec, vec, vec],
        out_specs=pl.BlockSpec((tm, D), lambda i: (i, 0)),
        out_shape=jax.ShapeDtypeStruct((L, D), bf16),
        compiler_params=_cp(("parallel",)),
        name="norm_mod",
    )(x, nw, sc, sh)


def _proj_kernel(u_ref, w_ref, *rest, bw, tn, rope, scale_from, relu2):
    *rest, wb = rest
    if len(w_ref.shape) == 3:
        w_ref = w_ref.at[0]

    @pl.when(pl.program_id(1) == 0)
    def _():
        lane = lax.broadcasted_iota(jnp.int32, (CAST_ROWS, tn), 1) % HD
        for rb in range(D // CAST_ROWS):
            rows = slice(rb * CAST_ROWS, (rb + 1) * CAST_ROWS)
            w = w_ref[rows, :]
            if rope:
                qd = HD // 4
                w = jnp.where((lane >= qd) & (lane < 2 * qd), pltpu.roll(w, tn - qd, 1),
                              jnp.where((lane >= 2 * qd) & (lane < 3 * qd), pltpu.roll(w, qd, 1), w))
            wb[rows, :] = w.astype(bf16)

    if rope:
        rc_ref, rs_ref, cc_ref, cs_ref, o_ref = rest
        tm = u_ref.shape[0]
        scale = jnp.where(pl.program_id(0) >= scale_from, HD ** -0.5, 1.0).astype(f32)
        cos = _rope_rows(rc_ref[...], cc_ref[...], tm) * scale
        sin = _rope_rows(rs_ref[...], cs_ref[...], tm) * scale
    else:
        (o_ref,) = rest
    sw = min(tn, MXU_N)
    for nb in range(tn // sw):
        acc = _dot(u_ref[...], wb[:, nb * sw:(nb + 1) * sw])
        if relu2:
            acc = jnp.maximum(acc, 0.0)
            acc = acc * acc
        if rope:
            for hh in range(sw // HD):
                t = acc[:, hh * HD:(hh + 1) * HD]
                o_ref[nb * (sw // HD) + hh] = (t * cos + pltpu.roll(t, HD // 2, 1) * sin).astype(o_ref.dtype)
        elif bw and bw <= sw:
            for hh in range(sw // bw):
                o_ref[nb * (sw // bw) + hh] = acc[:, hh * bw:(hh + 1) * bw].astype(o_ref.dtype)
        elif bw:
            per = bw // sw
            o_ref[nb // per, :, (nb % per) * sw:(nb % per + 1) * sw] = acc.astype(o_ref.dtype)
        else:
            o_ref[:, nb * sw:(nb + 1) * sw] = acc.astype(o_ref.dtype)


def _project(u, w, *, col0, ncols, bw, tn, out_dtype=bf16, rope=None, scale_from=0, relu2=False, name):
    M = u.shape[0]
    N = ncols
    assert w.dtype == f32 and col0 % LANES == 0 and ncols % tn == 0
    jb = col0 // tn
    tm = min(M, PROJ_TM)
    if col0 % tn:
        w_spec = pl.BlockSpec((pl.Element(1), pl.Element(D), pl.Element(tn)),
                              lambda j, i: (0, 0, pl.multiple_of(col0 + j * tn, LANES)))
    else:
        w_spec = pl.BlockSpec((None, D, tn), lambda j, i: (0, 0, jb + j))
    in_specs = [pl.BlockSpec((tm, D), lambda j, i: (i, 0)), w_spec]
    args = [u, w]
    if rope is not None:
        in_specs += [pl.BlockSpec((tm // GRID_W, LANES), lambda j, i: (i, 0))] * 2
        in_specs += [pl.BlockSpec((GRID_W, LANES), lambda j, i: (0, 0))] * 2
        args += list(rope)
    if bw:
        out_spec = pl.BlockSpec((tn // bw, tm, bw), lambda j, i: (j, i, 0))
        out_shape = jax.ShapeDtypeStruct((N // bw, M, bw), out_dtype)
    else:
        out_spec = pl.BlockSpec((tm, tn), lambda j, i: (i, j))
        out_shape = jax.ShapeDtypeStruct((M, N), out_dtype)
    return pl.pallas_call(
        functools.partial(_proj_kernel, bw=bw, tn=tn, rope=rope is not None, scale_from=scale_from, relu2=relu2),
        grid=(N // tn, M // tm),
        in_specs=in_specs,
        out_specs=out_spec,
        out_shape=out_shape,
        scratch_shapes=[pltpu.VMEM((D, tn), bf16)],
        compiler_params=_cp(("parallel", "arbitrary")),
        name=name,
    )(*args)


def _shift_matrix():
    rows = lax.broadcasted_iota(jnp.int32, (4 * T, T + 2 * HALO), 0)
    cols = lax.broadcasted_iota(jnp.int32, (4 * T, T + 2 * HALO), 1)
    blk = rows // T
    tap = jnp.where(blk < KC // 2, blk, blk + 1)
    return jnp.where(cols == (rows - blk * T) + HALO + tap - KC // 2, 1.0, 0.0).astype(bf16)


def _conv_shift(shift, main, prev, nxt, has_prev, has_next):
    zero = jnp.zeros_like(prev)
    ext = jnp.concatenate([jnp.where(has_prev, prev, zero), main, jnp.where(has_next, nxt, zero)], axis=0)
    sh = _dot(shift, ext)
    return [sh[0:T], sh[T:2 * T], main.astype(f32), sh[2 * T:3 * T], sh[3 * T:4 * T]]


def _conv_taps(taps, cw, cb):
    acc = cb
    for j in range(KC):
        acc = acc + taps[j] * cw[j:j + 1, :]
    return _silu(acc)


def _decay_prologue(dt_ref, bias_ref, arow_ref):
    dt = _softplus(dt_ref[...] + bias_ref[...])
    la = dt * arow_ref[...]
    acs = _cumsum_rows(la, _tri_incl())
    tot = acs[T - 1:T, :]
    return dt, la, acs, tot


def _transpose_blocks(x):
    n = x.shape[1] // LANES
    return jnp.concatenate([x[:, b * LANES:(b + 1) * LANES].T for b in range(n)], axis=0)


def _ssd_state_update(hs_ref, g, xc, bc16, w_rows, dec_rows):
    xT = _transpose_blocks(xc)
    lhs = jnp.concatenate(
        [(xT[r * P:(r + 1) * P, :] * w_rows[r:r + 1, :]).astype(bf16) for r in range(R)], axis=0)
    upd = _dot(lhs, bc16)
    old = hs_ref[g]
    hs_ref[g] = jnp.concatenate(
        [old[r * P:(r + 1) * P, :] * dec_rows[r:r + 1, :] + upd[r * P:(r + 1) * P, :] for r in range(R)], axis=0)


def _ret_state_update(hr_ref, h, v16, k16, wcol, dec_row):
    vw = (v16.astype(f32) * wcol).T.astype(bf16)
    hr_ref[h] = hr_ref[h] * dec_row + _dot(vw, k16)


def _state_kernel(*refs, nc, fwd, emit, write_conv):
    it = iter(refs)
    k_ref, v_ref = next(it), next(it)
    xs_ref, xs_p, xs_n = next(it), next(it), next(it)
    b_ref, b_p, b_n = next(it), next(it), next(it)
    if write_conv:
        c_ref, c_p, c_n = next(it), next(it), next(it)
    dt_ref = next(it)
    cwx, cbx, cwb, cbb = next(it), next(it), next(it), next(it)
    if write_conv:
        cwc, cbc = next(it), next(it)
    bias_ref, arow_ref = next(it), next(it)
    rw_ref, rdec_ref = next(it), next(it)
    h0r_ref, h0s_ref = next(it), next(it)
    if emit:
        gr_ref, gs_ref = next(it), next(it)
    if write_conv:
        xo_ref, bco_ref = next(it), next(it)
    hfr_ref, hfs_ref = next(it), next(it)
    hr, hs, wT_s, decT_s = next(it), next(it), next(it), next(it)

    s = pl.program_id(0)
    c = s if fwd else nc - 1 - s
    has_prev = c > 0
    has_next = c < nc - 1

    @pl.when(s == 0)
    def _():
        hr[...] = h0r_ref[...]
        hs[...] = h0s_ref[...]

    if emit:
        gr_ref[0] = hr[...].astype(bf16)
        gs_ref[0] = hs[...].astype(bf16)

    dt, la, acs, tot = _decay_prologue(dt_ref, bias_ref, arow_ref)
    wexp = jnp.exp(tot - acs) if fwd else jnp.exp(acs - la)
    wT_s[...] = (wexp * dt).T
    decT_s[...] = jnp.broadcast_to(jnp.exp(tot), (T, 2 * SH)).T
    off = 0 if fwd else SH

    shift = _shift_matrix()

    def cat(a, b):
        return jnp.concatenate([a, b], axis=1)

    def shifted(g):
        tx = _conv_shift(shift, xs_ref[g], xs_p[g], xs_n[g], has_prev, has_next)
        if write_conv:
            tb = _conv_shift(shift, cat(b_ref[g], c_ref[g]), cat(b_p[g], c_p[g]), cat(b_n[g], c_n[g]),
                             has_prev, has_next)
        else:
            tb = _conv_shift(shift, b_ref[g], b_p[g], b_n[g], has_prev, has_next)
        return tx, tb

    nxt_taps = shifted(0)
    for g in range(NG):
        tx, tb = nxt_taps
        if g + 1 < NG:
            nxt_taps = shifted(g + 1)
        xc16 = _conv_taps(tx, cwx[g], cbx[g]).astype(bf16)
        if write_conv:
            bcc = _conv_taps(tb, cat(cwb[g], cwc[g]), cat(cbb[g], cbc[g])).astype(bf16)
            bc16 = bcc[:, :NS]
            xo_ref[g] = xc16
            bco_ref[g] = bcc
        else:
            bc16 = _conv_taps(tb, cwb[g], cbb[g]).astype(bf16)
        r0 = off + g * R
        _ssd_state_update(hs, g, xc16.astype(f32), bc16, wT_s[r0:r0 + R, :], decT_s[r0:r0 + R, :])

    dsel = 0 if fwd else 1

    for h in range(RH):
        _ret_state_update(hr, h, v_ref[h], k_ref[h], rw_ref[h], rdec_ref[h][dsel:dsel + 1, :])

    @pl.when(s == nc - 1)
    def _():
        hfr_ref[...] = hr[...]
        hfs_ref[...] = hs[...]


def _state_sweep(L, *, fwd, emit, write_conv, k, k_blk, v, v_blk, xs, xs_blk, bsrc, b_blk, c_blk, dt,
                 convw, bias, arow, rw, rdec, h0r, h0s):
    nc = L // T
    tb = T // HALO
    nrb = L // HALO

    def cidx(s):
        return s if fwd else nc - 1 - s

    def main(lead, n, width):
        return pl.BlockSpec((n, T, width), lambda s: (lead, cidx(s), 0))

    def prev(lead, n, width):
        return pl.BlockSpec((n, HALO, width), lambda s: (lead, jnp.maximum(cidx(s) * tb - 1, 0), 0))

    def nxt(lead, n, width):
        return pl.BlockSpec((n, HALO, width), lambda s: (lead, jnp.minimum((cidx(s) + 1) * tb, nrb - 1), 0))

    def whole(a):
        nd = a.ndim
        return pl.BlockSpec(a.shape, lambda s: (0,) * nd)

    cwx, cbx, cwb, cbb, cwc, cbc = convw
    in_specs = [main(k_blk, RH, HD), main(v_blk, RH, HD),
                main(xs_blk, NG, GW), prev(xs_blk, NG, GW), nxt(xs_blk, NG, GW),
                main(b_blk, NG, NS), prev(b_blk, NG, NS), nxt(b_blk, NG, NS)]
    args = [k, v, xs, xs, xs, bsrc, bsrc, bsrc]
    if write_conv:
        in_specs += [main(c_blk, NG, NS), prev(c_blk, NG, NS), nxt(c_blk, NG, NS)]
        args += [bsrc, bsrc, bsrc]
    in_specs += [pl.BlockSpec((T, 2 * SH), lambda s: (cidx(s), 0))]
    args += [dt]
    small = [cwx, cbx, cwb, cbb] + ([cwc, cbc] if write_conv else []) + [bias, arow, rw, rdec, h0r, h0s]
    in_specs += [whole(a) for a in small]
    args += small

    out_specs, out_shape = [], []
    if emit:
        out_specs += [pl.BlockSpec((1, RH, HD, HD), lambda s: (cidx(s), 0, 0, 0)),
                      pl.BlockSpec((1, NG, GW, NS), lambda s: (cidx(s), 0, 0, 0))]
        out_shape += [jax.ShapeDtypeStruct((nc, RH, HD, HD), bf16), jax.ShapeDtypeStruct((nc, NG, GW, NS), bf16)]
    if write_conv:
        out_specs += [pl.BlockSpec((NG, T, GW), lambda s: (0, cidx(s), 0)),
                      pl.BlockSpec((NG, T, 2 * NS), lambda s: (0, cidx(s), 0))]
        out_shape += [jax.ShapeDtypeStruct((NG, L, GW), bf16), jax.ShapeDtypeStruct((NG, L, 2 * NS), bf16)]
    out_specs += [pl.BlockSpec((RH, HD, HD), lambda s: (0, 0, 0)), pl.BlockSpec((NG, GW, NS), lambda s: (0, 0, 0))]
    out_shape += [jax.ShapeDtypeStruct((RH, HD, HD), f32), jax.ShapeDtypeStruct((NG, GW, NS), f32)]

    return pl.pallas_call(
        functools.partial(_state_kernel, nc=nc, fwd=fwd, emit=emit, write_conv=write_conv),
        grid=(nc,),
        in_specs=in_specs,
        out_specs=out_specs,
        out_shape=out_shape,
        scratch_shapes=[pltpu.VMEM((RH, HD, HD), f32), pltpu.VMEM((NG, GW, NS), f32),
                        pltpu.VMEM((2 * SH, T), f32), pltpu.VMEM((2 * SH, T), f32)],
        compiler_params=_cp(("arbitrary",), SWEEP_VMEM_LIMIT),
        name="state_sweep_" + ("f" if fwd else "b") + ("_emit" if emit else ""),
    )(*args)


def _fwd_kernel(qk_ref, vg_ref, z_ref, xs_ref, bc_ref, dt_ref, gret_ref, gssd_ref,
                wret_ref, ef_ref, eb_ref, wf_ref, rdec_ref, bias_ref, arow_ref, dexp_ref, snw_ref, rnw_ref,
                h0r_ref, h0s_ref, dsel_ref, esel_ref, yr_ref, ys_ref,
                hr, hs, ypre, ap_s, ep_s, apT_s, wT_s, decT_s):
    s = pl.program_id(0)

    @pl.when(s == 0)
    def _():
        hr[...] = h0r_ref[...]
        hs[...] = h0s_ref[...]

    dt, la, acs, tot = _decay_prologue(dt_ref, bias_ref, arow_ref)
    lane = lax.broadcasted_iota(jnp.int32, (T, 2 * SH), 1)
    is_f = lane < SH
    a1 = jnp.where(is_f, acs, acs - la)
    e1 = jnp.exp(jnp.where(is_f, acs, tot - (acs - la)))
    a2 = a1 * LOG2E
    ldt = jnp.maximum(jnp.log2(dt), LOG2_FLOOR)
    for k, part in enumerate(_split3(a2)):
        ap_s[k] = part
    for k, part in enumerate(_split3(jnp.where(is_f, ldt - a2, ldt + a2))):
        apT_s[k] = part.T
    for k, part in enumerate(_split3(e1)):
        ep_s[k] = part
    wT_s[...] = (jnp.exp(tot - acs) * dt).T
    decT_s[...] = jnp.broadcast_to(jnp.exp(tot), (T, 2 * SH)).T

    ii = lax.broadcasted_iota(jnp.int32, (T, T), 0)
    jj = lax.broadcasted_iota(jnp.int32, (T, T), 1)
    lower = jj <= ii
    part_masks = [((lane >= k * R) & (lane < (k + 1) * R)) | ((lane >= SH + k * R) & (lane < SH + (k + 1) * R))
                  for k in range(3)]
    ones_lanes = (lane >= ONES_LANE) & (lane < ONES_LANE + 3)
    row16 = lax.broadcasted_iota(jnp.int32, (16, T), 0)
    lane_p = lax.broadcasted_iota(jnp.int32, (T, 2 * P), 1)

    def pack(parts_ref, g, with_ones):
        acc = jnp.where(ones_lanes, 1.0, 0.0) if with_ones else jnp.zeros((T, 2 * SH), f32)
        for k in range(3):
            shift = (2 * SH - g * R + k * R) % (2 * SH)
            acc = acc + jnp.where(part_masks[k], pltpu.roll(parts_ref[k], shift, 1), 0.0)
        return acc.astype(bf16)

    def front(g):
        b16 = bc_ref[g, :, 0:NS]
        c16 = bc_ref[g, :, NS:2 * NS]
        sc = _nt(c16, b16)
        r0 = g * R
        lhs_a = pack(ap_s, g, True)
        lhs_e = pack(ep_s, g, False)
        cf_rows = [apT_s[k, r0:r0 + R, :] for k in range(3)]
        cb_rows = [apT_s[k, SH + r0:SH + r0 + R, :] for k in range(3)]
        hcat = jnp.concatenate([hs[g].astype(bf16), gssd_ref[0, g]], axis=0)
        ycross = _nt(c16, hcat)
        efb = _dot(lhs_e, esel_ref[...])
        dmats = []
        for r in range(R):
            lo = jnp.zeros((16, T), f32)
            up = jnp.zeros((16, T), f32)
            for k in range(3):
                lo = jnp.where(row16 == k, cf_rows[k][r:r + 1, :], lo)
                up = jnp.where(row16 == k, cb_rows[k][r:r + 1, :], up)
            dyn = jnp.concatenate([lo, up], axis=1).astype(bf16)
            rhs = jnp.concatenate([dsel_ref[r, 0:ONES_LANE, :], dyn, dsel_ref[r, ONES_LANE + 16:, :]], axis=0)
            dmats.append(_dot(lhs_a, rhs))
        return b16, sc, ycross, efb, dmats

    def back(g, staged, ssq):
        b16, sc, ycross, efb, dmats = staged
        r0 = g * R
        xg = xs_ref[g]
        zg = z_ref[g].astype(f32)
        dg = dexp_ref[g]
        ms = [(sc * jnp.exp2(jnp.where(lower, dm[:, :T], dm[:, T:]))).astype(bf16) for dm in dmats]
        ys_parts = []
        for t in range(R // 2):
            xp = xg[:, t * 2 * P:(t + 1) * 2 * P]
            zero = jnp.zeros_like(xp)
            rhs = jnp.concatenate([jnp.where(lane_p < P, xp, zero), jnp.where(lane_p >= P, xp, zero)], axis=0)
            ys_parts.append(_dot(jnp.concatenate([ms[2 * t], ms[2 * t + 1]], axis=1), rhs))
        yg = jnp.concatenate(ys_parts, axis=1)
        yg = yg + ycross[:, :GW] * efb[:, :GW] + ycross[:, GW:] * efb[:, GW:]
        yg = (yg + xg.astype(f32) * dg) * _silu(zg)
        ypre[g] = yg
        ssq = ssq + jnp.sum(yg * yg, axis=1, keepdims=True)
        _ssd_state_update(hs, g, xg.astype(f32), b16, wT_s[r0:r0 + R, :], decT_s[r0:r0 + R, :])
        return ssq

    ssq = jnp.zeros((T, 1), f32)
    staged = front(0)
    for g in range(NG):
        cur = staged
        if g + 1 < NG:
            staged = front(g + 1)
        ssq = back(g, cur, ssq)
    rs = lax.rsqrt(ssq * (1.0 / SI) + EPS)
    for g in range(NG):
        ys_ref[:, g * GW:(g + 1) * GW] = ((ypre[g] * rs) * snw_ref[g]).astype(ys_ref.dtype)

    zero_h = jnp.zeros((T, HD), bf16)

    def blockdiag(a, b):
        return jnp.concatenate([jnp.concatenate([a, zero_h], axis=1), jnp.concatenate([zero_h, b], axis=1)], axis=0)

    def ret_front(t):
        pair = (2 * t, 2 * t + 1)
        q16 = [qk_ref[h] for h in pair]
        k16 = [qk_ref[RH + h] for h in pair]
        s2 = _nt(jnp.concatenate(q16, axis=1), blockdiag(*k16))
        cross = []
        for n, h in enumerate(pair):
            qf = q16[n].astype(f32)
            lc = jnp.concatenate([(qf * ef_ref[h]).astype(bf16), (qf * eb_ref[h]).astype(bf16)], axis=1)
            hcat = jnp.concatenate([hr[h].astype(bf16), gret_ref[0, h]], axis=1)
            cross.append(_nt(lc, hcat))
        return k16, s2, cross

    ret_staged = ret_front(0)
    for t in range(RH // 2):
        pair = (2 * t, 2 * t + 1)
        k16, s2, cross = ret_staged
        if t + 1 < RH // 2:
            ret_staged = ret_front(t + 1)
        v16 = [vg_ref[h] for h in pair]
        m2 = (s2 * jnp.concatenate([wret_ref[h] for h in pair], axis=1)).astype(bf16)
        y2 = _dot(m2, blockdiag(*v16))
        for n, h in enumerate(pair):
            y = y2[:, n * HD:(n + 1) * HD] + cross[n]
            mu = jnp.mean(y, axis=-1, keepdims=True)
            d = y - mu
            yn = d * lax.rsqrt(jnp.mean(d * d, axis=-1, keepdims=True) + EPS)
            gg = vg_ref[RH + h].astype(f32)
            yr_ref[:, h * HD:(h + 1) * HD] = ((yn * rnw_ref[h]) * _silu(gg)).astype(yr_ref.dtype)
            _ret_state_update(hr, h, v16[n], k16[n], wf_ref[h], rdec_ref[h][0:1, :])


def _fwd_sweep(L, qk, vg, zx, xc, bcc, dt, gret, gssd, tabs, bias, arow, dexp, snw, rnw, h0r, h0s):
    nc = L // T
    wret, ef, eb, wf, _, rdec = tabs

    def blk(lead, n, width):
        return pl.BlockSpec((n, T, width), lambda s: (lead, s, 0))

    def whole(a):
        nd = a.ndim
        return pl.BlockSpec(a.shape, lambda s: (0,) * nd)

    dsel, esel = _selector_constants()
    small = [wret, ef, eb, wf, rdec, bias, arow, dexp, snw, rnw, h0r, h0s, dsel, esel]
    in_specs = [blk(0, 2 * RH, HD), blk(0, 2 * RH, HD), blk(0, NG, GW), blk(0, NG, GW), blk(0, NG, 2 * NS),
                pl.BlockSpec((T, 2 * SH), lambda s: (s, 0)),
                pl.BlockSpec((1, RH, HD, HD), lambda s: (s, 0, 0, 0)),
                pl.BlockSpec((1, NG, GW, NS), lambda s: (s, 0, 0, 0))] + [whole(a) for a in small]
    return pl.pallas_call(
        _fwd_kernel,
        grid=(nc,),
        in_specs=in_specs,
        out_specs=[pl.BlockSpec((T, D), lambda s: (s, 0)), pl.BlockSpec((T, SI), lambda s: (s, 0))],
        out_shape=[jax.ShapeDtypeStruct((L, D), bf16), jax.ShapeDtypeStruct((L, SI), bf16)],
        scratch_shapes=[pltpu.VMEM((RH, HD, HD), f32), pltpu.VMEM((NG, GW, NS), f32), pltpu.VMEM((NG, T, GW), f32),
                        pltpu.VMEM((3, T, 2 * SH), f32), pltpu.VMEM((3, T, 2 * SH), f32),
                        pltpu.VMEM((3, 2 * SH, T), f32)]
        + [pltpu.VMEM((2 * SH, T), f32)] * 2,
        compiler_params=_cp(("arbitrary",), SWEEP_VMEM_LIMIT),
        name="fwd_sweep",
    )(qk, vg, zx, xc, bcc, dt, gret, gssd, *small)


def _branch_out_kernel(yr_ref, ys_ref, wr_ref, ws_ref, gr_ref, gs_ref, o_ref):
    for nb in range(o_ref.shape[1] // MXU_N):
        cols = slice(nb * MXU_N, (nb + 1) * MXU_N)
        acc_r = _dot(yr_ref[...], wr_ref[:, cols])
        acc_s = _dot(ys_ref[...], ws_ref[:, cols])
        m = _sigmoid(gr_ref[:, cols].astype(f32)) * acc_r + _sigmoid(gs_ref[:, cols].astype(f32)) * acc_s
        o_ref[:, cols] = m.astype(o_ref.dtype)


def _branch_out(yr, ys, wr, ws, gates):
    L = yr.shape[0]
    tm, tn = min(L, BRANCH_TM), BRANCH_TN
    nj = D // tn
    return pl.pallas_call(
        _branch_out_kernel,
        grid=(L // tm, nj),
        in_specs=[pl.BlockSpec((tm, D), lambda i, j: (i, 0)),
                  pl.BlockSpec((tm, SI), lambda i, j: (i, 0)),
                  pl.BlockSpec((D, tn), lambda i, j: (0, j)),
                  pl.BlockSpec((SI, tn), lambda i, j: (0, j)),
                  pl.BlockSpec((tm, tn), lambda i, j: (i, j)),
                  pl.BlockSpec((tm, tn), lambda i, j: (i, nj + j))],
        out_specs=pl.BlockSpec((tm, tn), lambda i, j: (i, j)),
        out_shape=jax.ShapeDtypeStruct((L, D), bf16),
        compiler_params=_cp(("parallel", "parallel")),
        name="branch_out",
    )(yr, ys, wr, ws, gates, gates)


def _resid_kernel(m_ref, w_ref, x_ref, g_ref, nw_ref, sc_ref, sh_ref, h_ref, f_ref):
    h = x_ref[...] + g_ref[...] * _dot(m_ref[...], w_ref[...])
    h_ref[...] = h
    f_ref[...] = _rms_mod(h, nw_ref[...], sc_ref[...], sh_ref[...]).astype(f_ref.dtype)


def _mix_residual(m, w, x, gate, nw, sc, sh):
    L = m.shape[0]
    tm = min(L, MIX_TM)
    vec = pl.BlockSpec((1, D), lambda i: (0, 0))
    row = pl.BlockSpec((tm, D), lambda i: (i, 0))
    return pl.pallas_call(
        _resid_kernel,
        grid=(L // tm,),
        in_specs=[row, pl.BlockSpec((D, D), lambda i: (0, 0)), row, vec, vec, vec, vec],
        out_specs=[row, row],
        out_shape=[jax.ShapeDtypeStruct((L, D), f32), jax.ShapeDtypeStruct((L, D), bf16)],
        compiler_params=_cp(("parallel",)),
        name="mix_residual",
    )(m, w, x, gate, nw, sc, sh)


def _mlp2_kernel(a_ref, w_ref, h_ref, g_ref, fw_ref, o_ref, *, nk):
    kk = pl.program_id(1)

    @pl.when(kk == 0)
    def _():
        o_ref[...] = jnp.zeros_like(o_ref)

    for nb in range(D // MLP2_STRIP):
        cols = slice(nb * MLP2_STRIP, (nb + 1) * MLP2_STRIP)
        o_ref[:, cols] += _dot(a_ref[...], w_ref[:, cols])

    @pl.when(kk == nk - 1)
    def _():
        h2 = h_ref[...] + g_ref[...] * o_ref[...]
        y = h2 * lax.rsqrt(jnp.mean(h2 * h2, axis=-1, keepdims=True) + EPS)
        o_ref[...] = y * fw_ref[...]


def _mlp2_final(a, w, h, gate, fw):
    L = a.shape[0]
    tm, tk = min(L, MLP2_TM), MLP2_TK
    nk = DFF // tk
    vec = pl.BlockSpec((1, D), lambda i, k: (0, 0))
    return pl.pallas_call(
        functools.partial(_mlp2_kernel, nk=nk),
        grid=(L // tm, nk),
        in_specs=[pl.BlockSpec((tm, tk), lambda i, k: (i, k)),
                  pl.BlockSpec((tk, D), lambda i, k: (k, 0)),
                  pl.BlockSpec((tm, D), lambda i, k: (i, 0)), vec, vec],
        out_specs=pl.BlockSpec((tm, D), lambda i, k: (i, 0)),
        out_shape=jax.ShapeDtypeStruct((L, D), f32),
        compiler_params=_cp(("parallel", "arbitrary")),
        name="mlp2_final",
    )(a, w, h, gate, fw)


def _split_conv(conv_w, conv_b):
    def grp(w, b, width):
        return (jnp.transpose(w.reshape(KC, NG, width), (1, 0, 2)), b.reshape(NG, 1, width))
    cwx, cbx = grp(conv_w[:, :SI], conv_b[:SI], GW)
    cwb, cbb = grp(conv_w[:, SI:SI + NG * NS], conv_b[SI:SI + NG * NS], NS)
    cwc, cbc = grp(conv_w[:, SI + NG * NS:], conv_b[SI + NG * NS:], NS)
    return cwx, cbx, cwb, cbb, cwc, cbc


def kernel(x, c, ctx, c_ctx, w_mod, b_mod, norm1_w, w_in, conv_w, conv_b, ret_decay_logit, ret_norm_w,
           ssd_a_log, ssd_dt_bias, ssd_d, ssd_norm_w, w_ret_out, w_ssd_out, w_o, norm2_w, w_mlp1, w_mlp2,
           final_norm_w):
    depth = w_mod.shape[0]
    assert depth == 1 and x.shape[0] == 1 and x.shape[2] == D
    L = x.shape[1]
    Lc = ctx.shape[1]
    assert L % T == 0 and Lc % T == 0 and L % GRID_W == 0
    xl = x[0]
    xcx = ctx[0]
    ly = 0

    a8 = jnp.zeros((8, D), f32).at[0].set(c[0]).at[1].set(c_ctx)
    mod = _modulation(a8, w_mod[ly], b_mod[ly][None, :])
    sh_a, sc_a, g_a, sh_f, sc_f, g_f = [mod[0:1, i * D:(i + 1) * D] for i in range(6)]
    csh_a, csc_a = mod[1:2, 0:D], mod[1:2, D:2 * D]

    o_q, o_k, o_v, o_z, o_x = 0, D, 2 * D, 4 * D, 4 * D + SI
    o_b = o_x + SI
    o_dt = o_b + 2 * NG * NS
    o_gate = o_dt + 2 * SH

    def proj(src, col, n, **kw):
        return _project(src, w_in, col0=col, ncols=n, **kw)

    convw = _split_conv(conv_w[ly], conv_b[ly])
    bias = ssd_dt_bias[ly].reshape(1, 2 * SH)
    arow = (-jnp.exp(ssd_a_log[ly])).reshape(1, 2 * SH)
    dexp = jnp.repeat(ssd_d[ly], P).reshape(NG, 1, GW)
    snw = ssd_norm_w[ly].reshape(NG, 1, GW)
    rnw = ret_norm_w[ly].reshape(RH, 1, HD)
    nw1 = norm1_w[ly][None, :]

    tabs = _ret_tables(ret_decay_logit[ly])
    _, _, _, wf_t, wb_t, rdec = tabs
    zr = jnp.zeros((RH, HD, HD), f32)
    zs = jnp.zeros((NG, GW, NS), f32)

    uc = _norm_mod(xcx, nw1, csc_a, csh_a)
    ones_c, zeros_c = jnp.ones((Lc // GRID_W, LANES), f32), jnp.zeros((Lc // GRID_W, LANES), f32)
    ident = (ones_c, zeros_c, jnp.ones((GRID_W, LANES), f32), jnp.zeros((GRID_W, LANES), f32))
    kc = proj(uc, o_k, D, bw=HD, tn=CTX_TN, rope=ident, scale_from=0, name="ctx_k")
    vc = proj(uc, o_v, D, bw=HD, tn=CTX_TN, name="ctx_v")
    bcx = proj(uc, o_b, NG * NS, bw=NS, tn=CTX_TN, name="ctx_b")
    xsc = proj(uc, o_x, SI, bw=GW, tn=CTX_TN, name="ctx_x")
    dtc = proj(uc, o_dt, 2 * SH, bw=0, tn=2 * SH, out_dtype=f32, name="ctx_dt")
    common = dict(k=kc, k_blk=0, v=vc, v_blk=0, xs=xsc, xs_blk=0, bsrc=bcx, b_blk=0, c_blk=0, dt=dtc,
                  convw=convw, bias=bias, arow=arow, rdec=rdec, h0r=zr, h0s=zs)
    cfr, cfs = _state_sweep(Lc, fwd=True, emit=False, write_conv=False, rw=wf_t, **common)
    cbr, cbs = _state_sweep(Lc, fwd=False, emit=False, write_conv=False, rw=wb_t, **common)

    u = _norm_mod(xl, nw1, sc_a, sh_a)
    rope = _rope_tables(L)
    qk = proj(u, o_q, 2 * D, bw=HD, tn=PROJ_TN, rope=rope, scale_from=D // PROJ_TN, name="lat_qk")
    vg = proj(u, o_v, 2 * D, bw=HD, tn=PROJ_TN, name="lat_vg")
    zx = proj(u, o_z, 2 * SI, bw=GW, tn=PROJ_TN, name="lat_zx")
    bcl = proj(u, o_b, 2 * NG * NS, bw=NS, tn=PROJ_TN, name="lat_bc")
    dtl = proj(u, o_dt, 2 * SH, bw=0, tn=2 * SH, out_dtype=f32, name="lat_dt")
    gates = proj(u, o_gate, 2 * D, bw=0, tn=PROJ_TN, name="lat_gates")

    gret, gssd, xcv, bccv, _, _ = _state_sweep(
        L, fwd=False, emit=True, write_conv=True, k=qk, k_blk=1, v=vg, v_blk=0, xs=zx, xs_blk=1, bsrc=bcl,
        b_blk=0, c_blk=1, dt=dtl, convw=convw, bias=bias, arow=arow, rw=wb_t, rdec=rdec,
        h0r=cbr, h0s=cbs)
    yr, ys = _fwd_sweep(L, qk, vg, zx, xcv, bccv, dtl, gret, gssd, tabs, bias, arow, dexp, snw, rnw, cfr, cfs)

    m = _branch_out(yr, ys, w_ret_out[ly].astype(bf16), w_ssd_out[ly].astype(bf16), gates)
    h1, f = _mix_residual(m, w_o[ly].astype(bf16), xl, g_a, norm2_w[ly][None, :], sc_f, sh_f)
    a = _project(f, w_mlp1, col0=0, ncols=DFF, bw=0, tn=PROJ_TN, relu2=True, name="mlp1")
    out = _mlp2_final(a, w_mlp2[ly].astype(bf16), h1, g_f, final_norm_w[None, :])
    return out[None]
```

```python
import functools
import math

import numpy as np
import jax
import jax.numpy as jnp
from jax import lax
from jax.experimental import pallas as pl
from jax.experimental.pallas import tpu as pltpu

f32 = jnp.float32
bf16 = jnp.bfloat16

D = 2048
T = 128
GRID_W = 64
HD = 128
RH = D // HD
SI = 2 * D
P = 64
SH = SI // P
NS = 128
NG = 8
R = SH // NG
GW = R * P
KC = 5
DFF = 4 * D
ROPE_BASE = 10000.0
EPS = 1e-6
LANES = 128
MXU_N = 256
CAST_ROWS = 256
HALO = 16
LOG2E = 1.0 / math.log(2.0)
LOG2_FLOOR = -300.0
ONES_LANE = 96
VMEM_LIMIT = 56 * 1024 * 1024

PROJ_TM, PROJ_TN = 2048, 1024
CTX_TN = 512
NORM_TM = 1024
MOD_TN = 1024
BRANCH_TM, BRANCH_TN = 1024, 512
MIX_TM = 512
MLP2_TM, MLP2_TK, MLP2_STRIP = 1024, 1024, 512


def _cp(sem):
    return pltpu.CompilerParams(dimension_semantics=sem, vmem_limit_bytes=VMEM_LIMIT)


def _nt(a, b):
    return lax.dot_general(a, b, (((1,), (1,)), ((), ())), preferred_element_type=f32)


def _dot(a, b):
    return jnp.dot(a, b, preferred_element_type=f32)


def _sigmoid(x):
    return 0.5 + 0.5 * jnp.tanh(0.5 * x)


def _silu(x):
    h = 0.5 * x
    return h + h * jnp.tanh(h)


def _softplus(x):
    return jnp.maximum(x, 0.0) + jnp.log1p(jnp.exp(-jnp.abs(x)))


def _log_sigmoid(x):
    return jnp.minimum(x, 0.0) - jnp.log1p(jnp.exp(-jnp.abs(x)))


def _split3(x):
    p1 = x.astype(bf16).astype(f32)
    r1 = x - p1
    p2 = r1.astype(bf16).astype(f32)
    p3 = (r1 - p2).astype(bf16).astype(f32)
    return p1, p2, p3


def _cumsum_rows(la, tri):
    a1, a2, a3 = _split3(la)
    return _dot(tri, a1.astype(bf16)) + _dot(tri, a2.astype(bf16)) + _dot(tri, a3.astype(bf16))


def _selector_constants():
    dsel = np.zeros((R, 2 * SH, 2 * T), np.float32)
    esel = np.zeros((2 * SH, 2 * GW), np.float32)
    for r in range(R):
        for k in range(3):
            dsel[r, k * R + r, :T] = 1.0
            dsel[r, SH + k * R + r, T:] = -1.0
            esel[k * R + r, r * P:(r + 1) * P] = 1.0
            esel[SH + k * R + r, GW + r * P:GW + (r + 1) * P] = 1.0
    return jnp.asarray(dsel, bf16), jnp.asarray(esel, bf16)


def _tri_incl():
    ii = lax.broadcasted_iota(jnp.int32, (T, T), 0)
    jj = lax.broadcasted_iota(jnp.int32, (T, T), 1)
    return jnp.where(jj <= ii, 1.0, 0.0).astype(bf16)


def _rope_tab_kernel(rc_ref, rs_ref, cc_ref, cs_ref):
    def tab(n):
        idx = lax.broadcasted_iota(jnp.int32, (n, LANES), 0).astype(f32)
        lane = lax.broadcasted_iota(jnp.int32, (n, LANES), 1)
        freqs = jnp.exp((lane % (HD // 4)).astype(f32) * (-math.log(ROPE_BASE) / (HD // 4)))
        ang = idx * freqs
        s = jnp.sin(ang)
        return jnp.cos(ang), jnp.where(lane < HD // 2, -s, s)

    rc_ref[...], rs_ref[...] = tab(rc_ref.shape[0])
    cc_ref[...], cs_ref[...] = tab(GRID_W)


def _rope_tables(L):
    rows = L // GRID_W
    shapes = [jax.ShapeDtypeStruct((rows, LANES), f32)] * 2 + [jax.ShapeDtypeStruct((GRID_W, LANES), f32)] * 2
    return pl.pallas_call(_rope_tab_kernel, out_shape=shapes, name="rope_tables")()


def _rope_rows(row_tab, col_tab, tm):
    lane = lax.broadcasted_iota(jnp.int32, (GRID_W, LANES), 1)
    use_row = (lane % (HD // 2)) < HD // 4
    return jnp.concatenate([jnp.where(use_row, row_tab[r:r + 1, :], col_tab) for r in range(tm // GRID_W)], axis=0)


def _ret_tab_kernel(lg_ref, w_ref, ef_ref, eb_ref, wf_ref, wb_ref, dec_ref):
    lf = _log_sigmoid(lg_ref[0])
    lb = _log_sigmoid(lg_ref[1])
    ii = lax.broadcasted_iota(jnp.int32, (T, T), 0)
    jj = lax.broadcasted_iota(jnp.int32, (T, T), 1)
    dl = (ii - jj).astype(f32)
    w_ref[...] = jnp.exp(jnp.where(jj <= ii, dl * lf, -dl * lb))
    idx = lax.broadcasted_iota(jnp.int32, (T, LANES), 0).astype(f32)
    ef_ref[...] = jnp.exp((idx + 1.0) * lf)
    eb_ref[...] = jnp.exp((T - idx) * lb)
    wf_ref[...] = jnp.exp((T - 1.0 - idx) * lf)
    wb_ref[...] = jnp.exp(idx * lb)
    srow = lax.broadcasted_iota(jnp.int32, (8, LANES), 0)
    dec_ref[...] = jnp.where(srow == 0, jnp.exp(T * lf), jnp.exp(T * lb))


def _ret_tables(logit):
    lg = jnp.broadcast_to(logit[:, :, None, None], (2, RH, 1, LANES))
    tab = pl.BlockSpec((None, T, LANES), lambda h: (h, 0, 0))
    return pl.pallas_call(
        _ret_tab_kernel,
        grid=(RH,),
        in_specs=[pl.BlockSpec((2, None, 1, LANES), lambda h: (0, h, 0, 0))],
        out_specs=[tab, tab, tab, tab, tab, pl.BlockSpec((None, 8, LANES), lambda h: (h, 0, 0))],
        out_shape=[jax.ShapeDtypeStruct((RH, T, T), f32)] + [jax.ShapeDtypeStruct((RH, T, LANES), f32)] * 4
        + [jax.ShapeDtypeStruct((RH, 8, LANES), f32)],
        compiler_params=_cp(("parallel",)),
        name="ret_tables",
    )(lg)


def _mod_kernel(a_ref, w_ref, b_ref, o_ref):
    a = _silu(a_ref[...])
    o_ref[...] = _dot(a, w_ref[...]) + b_ref[...]


def _modulation(a8, w, b):
    tn = MOD_TN
    n = w.shape[1]
    return pl.pallas_call(
        _mod_kernel,
        grid=(n // tn,),
        in_specs=[pl.BlockSpec((8, D), lambda j: (0, 0)),
                  pl.BlockSpec((D, tn), lambda j: (0, j)),
                  pl.BlockSpec((1, tn), lambda j: (0, j))],
        out_specs=pl.BlockSpec((8, tn), lambda j: (0, j)),
        out_shape=jax.ShapeDtypeStruct((8, n), f32),
        compiler_params=_cp(("parallel",)),
        name="modulation",
    )(a8, w, b)


def _rms_mod(x, nw, sc, sh):
    y = x * lax.rsqrt(jnp.mean(x * x, axis=-1, keepdims=True) + EPS)
    return (y * nw) * (1.0 + sc) + sh


def _norm_kernel(x_ref, nw_ref, sc_ref, sh_ref, o_ref):
    o_ref[...] = _rms_mod(x_ref[...], nw_ref[...], sc_ref[...], sh_ref[...]).astype(o_ref.dtype)


def _norm_mod(x, nw, sc, sh):
    L = x.shape[0]
    tm = min(L, NORM_TM)
    vec = pl.BlockSpec((1, D), lambda i: (0, 0))
    return pl.pallas_call(
        _norm_kernel,
        grid=(L // tm,),
        in_specs=[pl.BlockSpec((tm, D), lambda i: (i, 0)), vec, vec, vec],
        out_specs=pl.BlockSpec((tm, D), lambda i: (i, 0)),
        out_shape=jax.ShapeDtypeStruct((L, D), bf16),
        compiler_params=_cp(("parallel",)),
        name="norm_mod",
    )(x, nw, sc, sh)


def _proj_kernel(u_ref, w_ref, *rest, bw, tn, rope, scale_from, relu2):
    *rest, wb = rest
    if len(w_ref.shape) == 3:
        w_ref = w_ref.at[0]

    @pl.when(pl.program_id(1) == 0)
    def _():
        lane = lax.broadcasted_iota(jnp.int32, (CAST_ROWS, tn), 1) % HD
        for rb in range(D // CAST_ROWS):
            rows = slice(rb * CAST_ROWS, (rb + 1) * CAST_ROWS)
            w = w_ref[rows, :]
            if rope:
                qd = HD // 4
                w = jnp.where((lane >= qd) & (lane < 2 * qd), pltpu.roll(w, tn - qd, 1),
                              jnp.where((lane >= 2 * qd) & (lane < 3 * qd), pltpu.roll(w, qd, 1), w))
            wb[rows, :] = w.astype(bf16)

    if rope:
        rc_ref, rs_ref, cc_ref, cs_ref, o_ref = rest
        tm = u_ref.shape[0]
        scale = jnp.where(pl.program_id(0) >= scale_from, HD ** -0.5, 1.0).astype(f32)
        cos = _rope_rows(rc_ref[...], cc_ref[...], tm) * scale
        sin = _rope_rows(rs_ref[...], cs_ref[...], tm) * scale
    else:
        (o_ref,) = rest
    sw = min(tn, MXU_N)
    for nb in range(tn // sw):
        acc = _dot(u_ref[...], wb[:, nb * sw:(nb + 1) * sw])
        if relu2:
            acc = jnp.maximum(acc, 0.0)
            acc = acc * acc
        if rope:
            for hh in range(sw // HD):
                t = acc[:, hh * HD:(hh + 1) * HD]
                o_ref[nb * (sw // HD) + hh] = (t * cos + pltpu.roll(t, HD // 2, 1) * sin).astype(o_ref.dtype)
        elif bw and bw <= sw:
            for hh in range(sw // bw):
                o_ref[nb * (sw // bw) + hh] = acc[:, hh * bw:(hh + 1) * bw].astype(o_ref.dtype)
        elif bw:
            per = bw // sw
            o_ref[nb // per, :, (nb % per) * sw:(nb % per + 1) * sw] = acc.astype(o_ref.dtype)
        else:
            o_ref[:, nb * sw:(nb + 1) * sw] = acc.astype(o_ref.dtype)


def _project(u, w, *, col0, ncols, bw, tn, out_dtype=bf16, rope=None, scale_from=0, relu2=False, name):
    M = u.shape[0]
    N = ncols
    assert w.dtype == f32 and col0 % LANES == 0 and ncols % tn == 0
    jb = col0 // tn
    tm = min(M, PROJ_TM)
    if col0 % tn:
        w_spec = pl.BlockSpec((pl.Element(1), pl.Element(D), pl.Element(tn)),
                              lambda j, i: (0, 0, pl.multiple_of(col0 + j * tn, LANES)))
    else:
        w_spec = pl.BlockSpec((None, D, tn), lambda j, i: (0, 0, jb + j))
    in_specs = [pl.BlockSpec((tm, D), lambda j, i: (i, 0)), w_spec]
    args = [u, w]
    if rope is not None:
        in_specs += [pl.BlockSpec((tm // GRID_W, LANES), lambda j, i: (i, 0))] * 2
        in_specs += [pl.BlockSpec((GRID_W, LANES), lambda j, i: (0, 0))] * 2
        args += list(rope)
    if bw:
        out_spec = pl.BlockSpec((tn // bw, tm, bw), lambda j, i: (j, i, 0))
        out_shape = jax.ShapeDtypeStruct((N // bw, M, bw), out_dtype)
    else:
        out_spec = pl.BlockSpec((tm, tn), lambda j, i: (i, j))
        out_shape = jax.ShapeDtypeStruct((M, N), out_dtype)
    return pl.pallas_call(
        functools.partial(_proj_kernel, bw=bw, tn=tn, rope=rope is not None, scale_from=scale_from, relu2=relu2),
        grid=(N // tn, M // tm),
        in_specs=in_specs,
        out_specs=out_spec,
        out_shape=out_shape,
        scratch_shapes=[pltpu.VMEM((D, tn), bf16)],
        compiler_params=_cp(("parallel", "arbitrary")),
        name=name,
    )(*args)


def _shift_matrix():
    rows = lax.broadcasted_iota(jnp.int32, (4 * T, T + 2 * HALO), 0)
    cols = lax.broadcasted_iota(jnp.int32, (4 * T, T + 2 * HALO), 1)
    blk = rows // T
    tap = jnp.where(blk < KC // 2, blk, blk + 1)
    return jnp.where(cols == (rows - blk * T) + HALO + tap - KC // 2, 1.0, 0.0).astype(bf16)


def _conv_shift(shift, main, prev, nxt, has_prev, has_next):
    zero = jnp.zeros_like(prev)
    ext = jnp.concatenate([jnp.where(has_prev, prev, zero), main, jnp.where(has_next, nxt, zero)], axis=0)
    sh = _dot(shift, ext)
    return [sh[0:T], sh[T:2 * T], main.astype(f32), sh[2 * T:3 * T], sh[3 * T:4 * T]]


def _conv_taps(taps, cw, cb):
    acc = cb
    for j in range(KC):
        acc = acc + taps[j] * cw[j:j + 1, :]
    return _silu(acc)


def _decay_prologue(dt_ref, bias_ref, arow_ref):
    dt = _softplus(dt_ref[...] + bias_ref[...])
    la = dt * arow_ref[...]
    acs = _cumsum_rows(la, _tri_incl())
    tot = acs[T - 1:T, :]
    return dt, la, acs, tot


def _transpose_blocks(x):
    n = x.shape[1] // LANES
    return jnp.concatenate([x[:, b * LANES:(b + 1) * LANES].T for b in range(n)], axis=0)


def _ssd_state_update(hs_ref, g, xc, bc16, w_rows, dec_rows):
    xT = _transpose_blocks(xc)
    lhs = jnp.concatenate(
        [(xT[r * P:(r + 1) * P, :] * w_rows[r:r + 1, :]).astype(bf16) for r in range(R)], axis=0)
    upd = _dot(lhs, bc16)
    old = hs_ref[g]
    hs_ref[g] = jnp.concatenate(
        [old[r * P:(r + 1) * P, :] * dec_rows[r:r + 1, :] + upd[r * P:(r + 1) * P, :] for r in range(R)], axis=0)


def _ret_state_update(hr_ref, h, v16, k16, wcol, dec_row):
    vw = (v16.astype(f32) * wcol).T.astype(bf16)
    hr_ref[h] = hr_ref[h] * dec_row + _dot(vw, k16)


def _state_kernel(*refs, nc, fwd, emit, write_conv):
    it = iter(refs)
    k_ref, v_ref = next(it), next(it)
    xs_ref, xs_p, xs_n = next(it), next(it), next(it)
    b_ref, b_p, b_n = next(it), next(it), next(it)
    if write_conv:
        c_ref, c_p, c_n = next(it), next(it), next(it)
    dt_ref = next(it)
    cwx, cbx, cwb, cbb = next(it), next(it), next(it), next(it)
    if write_conv:
        cwc, cbc = next(it), next(it)
    bias_ref, arow_ref = next(it), next(it)
    rw_ref, rdec_ref = next(it), next(it)
    h0r_ref, h0s_ref = next(it), next(it)
    if emit:
        gr_ref, gs_ref = next(it), next(it)
    if write_conv:
        xo_ref, bco_ref = next(it), next(it)
    hfr_ref, hfs_ref = next(it), next(it)
    hr, hs, wT_s, decT_s = next(it), next(it), next(it), next(it)

    s = pl.program_id(0)
    c = s if fwd else nc - 1 - s
    has_prev = c > 0
    has_next = c < nc - 1

    @pl.when(s == 0)
    def _():
        hr[...] = h0r_ref[...]
        hs[...] = h0s_ref[...]

    if emit:
        gr_ref[0] = hr[...].astype(bf16)
        gs_ref[0] = hs[...].astype(bf16)

    dt, la, acs, tot = _decay_prologue(dt_ref, bias_ref, arow_ref)
    wexp = jnp.exp(tot - acs) if fwd else jnp.exp(acs - la)
    wT_s[...] = (wexp * dt).T
    decT_s[...] = jnp.broadcast_to(jnp.exp(tot), (T, 2 * SH)).T
    off = 0 if fwd else SH

    shift = _shift_matrix()

    def cat(a, b):
        return jnp.concatenate([a, b], axis=1)

    def shifted(g):
        tx = _conv_shift(shift, xs_ref[g], xs_p[g], xs_n[g], has_prev, has_next)
        if write_conv:
            tb = _conv_shift(shift, cat(b_ref[g], c_ref[g]), cat(b_p[g], c_p[g]), cat(b_n[g], c_n[g]),
                             has_prev, has_next)
        else:
            tb = _conv_shift(shift, b_ref[g], b_p[g], b_n[g], has_prev, has_next)
        return tx, tb

    nxt_taps = shifted(0)
    for g in range(NG):
        tx, tb = nxt_taps
        if g + 1 < NG:
            nxt_taps = shifted(g + 1)
        xc16 = _conv_taps(tx, cwx[g], cbx[g]).astype(bf16)
        if write_conv:
            bcc = _conv_taps(tb, cat(cwb[g], cwc[g]), cat(cbb[g], cbc[g])).astype(bf16)
            bc16 = bcc[:, :NS]
            xo_ref[g] = xc16
            bco_ref[g] = bcc
        else:
            bc16 = _conv_taps(tb, cwb[g], cbb[g]).astype(bf16)
        r0 = off + g * R
        _ssd_state_update(hs, g, xc16.astype(f32), bc16, wT_s[r0:r0 + R, :], decT_s[r0:r0 + R, :])

    dsel = 0 if fwd else 1

    for h in range(RH):
        _ret_state_update(hr, h, v_ref[h], k_ref[h], rw_ref[h], rdec_ref[h][dsel:dsel + 1, :])

    @pl.when(s == nc - 1)
    def _():
        hfr_ref[...] = hr[...]
        hfs_ref[...] = hs[...]


def _state_sweep(L, *, fwd, emit, write_conv, k, k_blk, v, v_blk, xs, xs_blk, bsrc, b_blk, c_blk, dt,
                 convw, bias, arow, rw, rdec, h0r, h0s):
    nc = L // T
    tb = T // HALO
    nrb = L // HALO

    def cidx(s):
        return s if fwd else nc - 1 - s

    def main(lead, n, width):
        return pl.BlockSpec((n, T, width), lambda s: (lead, cidx(s), 0))

    def prev(lead, n, width):
        return pl.BlockSpec((n, HALO, width), lambda s: (lead, jnp.maximum(cidx(s) * tb - 1, 0), 0))

    def nxt(lead, n, width):
        return pl.BlockSpec((n, HALO, width), lambda s: (lead, jnp.minimum((cidx(s) + 1) * tb, nrb - 1), 0))

    def whole(a):
        nd = a.ndim
        return pl.BlockSpec(a.shape, lambda s: (0,) * nd)

    cwx, cbx, cwb, cbb, cwc, cbc = convw
    in_specs = [main(k_blk, RH, HD), main(v_blk, RH, HD),
                main(xs_blk, NG, GW), prev(xs_blk, NG, GW), nxt(xs_blk, NG, GW),
                main(b_blk, NG, NS), prev(b_blk, NG, NS), nxt(b_blk, NG, NS)]
    args = [k, v, xs, xs, xs, bsrc, bsrc, bsrc]
    if write_conv:
        in_specs += [main(c_blk, NG, NS), prev(c_blk, NG, NS), nxt(c_blk, NG, NS)]
        args += [bsrc, bsrc, bsrc]
    in_specs += [pl.BlockSpec((T, 2 * SH), lambda s: (cidx(s), 0))]
    args += [dt]
    small = [cwx, cbx, cwb, cbb] + ([cwc, cbc] if write_conv else []) + [bias, arow, rw, rdec, h0r, h0s]
    in_specs += [whole(a) for a in small]
    args += small

    out_specs, out_shape = [], []
    if emit:
        out_specs += [pl.BlockSpec((1, RH, HD, HD), lambda s: (cidx(s), 0, 0, 0)),
                      pl.BlockSpec((1, NG, GW, NS), lambda s: (cidx(s), 0, 0, 0))]
        out_shape += [jax.ShapeDtypeStruct((nc, RH, HD, HD), bf16), jax.ShapeDtypeStruct((nc, NG, GW, NS), bf16)]
    if write_conv:
        out_specs += [pl.BlockSpec((NG, T, GW), lambda s: (0, cidx(s), 0)),
                      pl.BlockSpec((NG, T, 2 * NS), lambda s: (0, cidx(s), 0))]
        out_shape += [jax.ShapeDtypeStruct((NG, L, GW), bf16), jax.ShapeDtypeStruct((NG, L, 2 * NS), bf16)]
    out_specs += [pl.BlockSpec((RH, HD, HD), lambda s: (0, 0, 0)), pl.BlockSpec((NG, GW, NS), lambda s: (0, 0, 0))]
    out_shape += [jax.ShapeDtypeStruct((RH, HD, HD), f32), jax.ShapeDtypeStruct((NG, GW, NS), f32)]

    return pl.pallas_call(
        functools.partial(_state_kernel, nc=nc, fwd=fwd, emit=emit, write_conv=write_conv),
        grid=(nc,),
        in_specs=in_specs,
        out_specs=out_specs,
        out_shape=out_shape,
        scratch_shapes=[pltpu.VMEM((RH, HD, HD), f32), pltpu.VMEM((NG, GW, NS), f32),
                        pltpu.VMEM((2 * SH, T), f32), pltpu.VMEM((2 * SH, T), f32)],
        compiler_params=_cp(("arbitrary",)),
        name="state_sweep_" + ("f" if fwd else "b") + ("_emit" if emit else ""),
    )(*args)


def _fwd_kernel(qk_ref, vg_ref, z_ref, xs_ref, bc_ref, dt_ref, gret_ref, gssd_ref,
                wret_ref, ef_ref, eb_ref, wf_ref, rdec_ref, bias_ref, arow_ref, dexp_ref, snw_ref, rnw_ref,
                h0r_ref, h0s_ref, dsel_ref, esel_ref, yr_ref, ys_ref,
                hr, hs, ypre, ap_s, ep_s, apT_s, wT_s, decT_s):
    s = pl.program_id(0)

    @pl.when(s == 0)
    def _():
        hr[...] = h0r_ref[...]
        hs[...] = h0s_ref[...]

    dt, la, acs, tot = _decay_prologue(dt_ref, bias_ref, arow_ref)
    lane = lax.broadcasted_iota(jnp.int32, (T, 2 * SH), 1)
    is_f = lane < SH
    a1 = jnp.where(is_f, acs, acs - la)
    e1 = jnp.exp(jnp.where(is_f, acs, tot - (acs - la)))
    a2 = a1 * LOG2E
    ldt = jnp.maximum(jnp.log2(dt), LOG2_FLOOR)
    for k, part in enumerate(_split3(a2)):
        ap_s[k] = part
    for k, part in enumerate(_split3(jnp.where(is_f, ldt - a2, ldt + a2))):
        apT_s[k] = part.T
    for k, part in enumerate(_split3(e1)):
        ep_s[k] = part
    wT_s[...] = (jnp.exp(tot - acs) * dt).T
    decT_s[...] = jnp.broadcast_to(jnp.exp(tot), (T, 2 * SH)).T

    ii = lax.broadcasted_iota(jnp.int32, (T, T), 0)
    jj = lax.broadcasted_iota(jnp.int32, (T, T), 1)
    lower = jj <= ii
    part_masks = [((lane >= k * R) & (lane < (k + 1) * R)) | ((lane >= SH + k * R) & (lane < SH + (k + 1) * R))
                  for k in range(3)]
    ones_lanes = (lane >= ONES_LANE) & (lane < ONES_LANE + 3)
    row16 = lax.broadcasted_iota(jnp.int32, (16, T), 0)
    lane_p = lax.broadcasted_iota(jnp.int32, (T, 2 * P), 1)

    def pack(parts_ref, g, with_ones):
        acc = jnp.where(ones_lanes, 1.0, 0.0) if with_ones else jnp.zeros((T, 2 * SH), f32)
        for k in range(3):
            shift = (2 * SH - g * R + k * R) % (2 * SH)
            acc = acc + jnp.where(part_masks[k], pltpu.roll(parts_ref[k], shift, 1), 0.0)
        return acc.astype(bf16)

    def front(g):
        b16 = bc_ref[g, :, 0:NS]
        c16 = bc_ref[g, :, NS:2 * NS]
        sc = _nt(c16, b16)
        r0 = g * R
        lhs_a = pack(ap_s, g, True)
        lhs_e = pack(ep_s, g, False)
        cf_rows = [apT_s[k, r0:r0 + R, :] for k in range(3)]
        cb_rows = [apT_s[k, SH + r0:SH + r0 + R, :] for k in range(3)]
        hcat = jnp.concatenate([hs[g].astype(bf16), gssd_ref[0, g]], axis=0)
        ycross = _nt(c16, hcat)
        efb = _dot(lhs_e, esel_ref[...])
        dmats = []
        for r in range(R):
            lo = jnp.zeros((16, T), f32)
            up = jnp.zeros((16, T), f32)
            for k in range(3):
                lo = jnp.where(row16 == k, cf_rows[k][r:r + 1, :], lo)
                up = jnp.where(row16 == k, cb_rows[k][r:r + 1, :], up)
            dyn = jnp.concatenate([lo, up], axis=1).astype(bf16)
            rhs = jnp.concatenate([dsel_ref[r, 0:ONES_LANE, :], dyn, dsel_ref[r, ONES_LANE + 16:, :]], axis=0)
            dmats.append(_dot(lhs_a, rhs))
        return b16, sc, ycross, efb, dmats

    def back(g, staged, ssq):
        b16, sc, ycross, efb, dmats = staged
        r0 = g * R
        xg = xs_ref[g]
        zg = z_ref[g].astype(f32)
        dg = dexp_ref[g]
        ms = [(sc * jnp.exp2(jnp.where(lower, dm[:, :T], dm[:, T:]))).astype(bf16) for dm in dmats]
        ys_parts = []
        for t in range(R // 2):
            xp = xg[:, t * 2 * P:(t + 1) * 2 * P]
            zero = jnp.zeros_like(xp)
            rhs = jnp.concatenate([jnp.where(lane_p < P, xp, zero), jnp.where(lane_p >= P, xp, zero)], axis=0)
            ys_parts.append(_dot(jnp.concatenate([ms[2 * t], ms[2 * t + 1]], axis=1), rhs))
        yg = jnp.concatenate(ys_parts, axis=1)
        yg = yg + ycross[:, :GW] * efb[:, :GW] + ycross[:, GW:] * efb[:, GW:]
        yg = (yg + xg.astype(f32) * dg) * _silu(zg)
        ypre[g] = yg
        ssq = ssq + jnp.sum(yg * yg, axis=1, keepdims=True)
        _ssd_state_update(hs, g, xg.astype(f32), b16, wT_s[r0:r0 + R, :], decT_s[r0:r0 + R, :])
        return ssq

    ssq = jnp.zeros((T, 1), f32)
    staged = front(0)
    for g in range(NG):
        cur = staged
        if g + 1 < NG:
            staged = front(g + 1)
        ssq = back(g, cur, ssq)
    rs = lax.rsqrt(ssq * (1.0 / SI) + EPS)
    for g in range(NG):
        ys_ref[:, g * GW:(g + 1) * GW] = ((ypre[g] * rs) * snw_ref[g]).astype(ys_ref.dtype)

    zero_h = jnp.zeros((T, HD), bf16)

    def blockdiag(a, b):
        return jnp.concatenate([jnp.concatenate([a, zero_h], axis=1), jnp.concatenate([zero_h, b], axis=1)], axis=0)

    def ret_front(t):
        pair = (2 * t, 2 * t + 1)
        q16 = [qk_ref[h] for h in pair]
        k16 = [qk_ref[RH + h] for h in pair]
        s2 = _nt(jnp.concatenate(q16, axis=1), blockdiag(*k16))
        cross = []
        for n, h in enumerate(pair):
            qf = q16[n].astype(f32)
            lc = jnp.concatenate([(qf * ef_ref[h]).astype(bf16), (qf * eb_ref[h]).astype(bf16)], axis=1)
            hcat = jnp.concatenate([hr[h].astype(bf16), gret_ref[0, h]], axis=1)
            cross.append(_nt(lc, hcat))
        return k16, s2, cross

    ret_staged = ret_front(0)
    for t in range(RH // 2):
        pair = (2 * t, 2 * t + 1)
        k16, s2, cross = ret_staged
        if t + 1 < RH // 2:
            ret_staged = ret_front(t + 1)
        v16 = [vg_ref[h] for h in pair]
        m2 = (s2 * jnp.concatenate([wret_ref[h] for h in pair], axis=1)).astype(bf16)
        y2 = _dot(m2, blockdiag(*v16))
        for n, h in enumerate(pair):
            y = y2[:, n * HD:(n + 1) * HD] + cross[n]
            mu = jnp.mean(y, axis=-1, keepdims=True)
            d = y - mu
            yn = d * lax.rsqrt(jnp.mean(d * d, axis=-1, keepdims=True) + EPS)
            gg = vg_ref[RH + h].astype(f32)
            yr_ref[:, h * HD:(h + 1) * HD] = ((yn * rnw_ref[h]) * _silu(gg)).astype(yr_ref.dtype)
            _ret_state_update(hr, h, v16[n], k16[n], wf_ref[h], rdec_ref[h][0:1, :])


def _fwd_sweep(L, qk, vg, zx, xc, bcc, dt, gret, gssd, tabs, bias, arow, dexp, snw, rnw, h0r, h0s):
    nc = L // T
    wret, ef, eb, wf, _, rdec = tabs

    def blk(lead, n, width):
        return pl.BlockSpec((n, T, width), lambda s: (lead, s, 0))

    def whole(a):
        nd = a.ndim
        return pl.BlockSpec(a.shape, lambda s: (0,) * nd)

    dsel, esel = _selector_constants()
    small = [wret, ef, eb, wf, rdec, bias, arow, dexp, snw, rnw, h0r, h0s, dsel, esel]
    in_specs = [blk(0, 2 * RH, HD), blk(0, 2 * RH, HD), blk(0, NG, GW), blk(0, NG, GW), blk(0, NG, 2 * NS),
                pl.BlockSpec((T, 2 * SH), lambda s: (s, 0)),
                pl.BlockSpec((1, RH, HD, HD), lambda s: (s, 0, 0, 0)),
                pl.BlockSpec((1, NG, GW, NS), lambda s: (s, 0, 0, 0))] + [whole(a) for a in small]
    return pl.pallas_call(
        _fwd_kernel,
        grid=(nc,),
        in_specs=in_specs,
        out_specs=[pl.BlockSpec((T, D), lambda s: (s, 0)), pl.BlockSpec((T, SI), lambda s: (s, 0))],
        out_shape=[jax.ShapeDtypeStruct((L, D), bf16), jax.ShapeDtypeStruct((L, SI), bf16)],
        scratch_shapes=[pltpu.VMEM((RH, HD, HD), f32), pltpu.VMEM((NG, GW, NS), f32), pltpu.VMEM((NG, T, GW), f32),
                        pltpu.VMEM((3, T, 2 * SH), f32), pltpu.VMEM((3, T, 2 * SH), f32),
                        pltpu.VMEM((3, 2 * SH, T), f32)]
        + [pltpu.VMEM((2 * SH, T), f32)] * 2,
        compiler_params=_cp(("arbitrary",)),
        name="fwd_sweep",
    )(qk, vg, zx, xc, bcc, dt, gret, gssd, *small)


def _branch_out_kernel(yr_ref, ys_ref, wr_ref, ws_ref, gr_ref, gs_ref, o_ref):
    for nb in range(o_ref.shape[1] // MXU_N):
        cols = slice(nb * MXU_N, (nb + 1) * MXU_N)
        acc_r = _dot(yr_ref[...], wr_ref[:, cols])
        acc_s = _dot(ys_ref[...], ws_ref[:, cols])
        m = _sigmoid(gr_ref[:, cols].astype(f32)) * acc_r + _sigmoid(gs_ref[:, cols].astype(f32)) * acc_s
        o_ref[:, cols] = m.astype(o_ref.dtype)


def _branch_out(yr, ys, wr, ws, gates):
    L = yr.shape[0]
    tm, tn = min(L, BRANCH_TM), BRANCH_TN
    nj = D // tn
    return pl.pallas_call(
        _branch_out_kernel,
        grid=(L // tm, nj),
        in_specs=[pl.BlockSpec((tm, D), lambda i, j: (i, 0)),
                  pl.BlockSpec((tm, SI), lambda i, j: (i, 0)),
                  pl.BlockSpec((D, tn), lambda i, j: (0, j)),
                  pl.BlockSpec((SI, tn), lambda i, j: (0, j)),
                  pl.BlockSpec((tm, tn), lambda i, j: (i, j)),
                  pl.BlockSpec((tm, tn), lambda i, j: (i, nj + j))],
        out_specs=pl.BlockSpec((tm, tn), lambda i, j: (i, j)),
        out_shape=jax.ShapeDtypeStruct((L, D), bf16),
        compiler_params=_cp(("parallel", "parallel")),
        name="branch_out",
    )(yr, ys, wr, ws, gates, gates)


def _resid_kernel(m_ref, w_ref, x_ref, g_ref, nw_ref, sc_ref, sh_ref, h_ref, f_ref):
    h = x_ref[...] + g_ref[...] * _dot(m_ref[...], w_ref[...])
    h_ref[...] = h
    f_ref[...] = _rms_mod(h, nw_ref[...], sc_ref[...], sh_ref[...]).astype(f_ref.dtype)


def _mix_residual(m, w, x, gate, nw, sc, sh):
    L = m.shape[0]
    tm = min(L, MIX_TM)
    vec = pl.BlockSpec((1, D), lambda i: (0, 0))
    row = pl.BlockSpec((tm, D), lambda i: (i, 0))
    return pl.pallas_call(
        _resid_kernel,
        grid=(L // tm,),
        in_specs=[row, pl.BlockSpec((D, D), lambda i: (0, 0)), row, vec, vec, vec, vec],
        out_specs=[row, row],
        out_shape=[jax.ShapeDtypeStruct((L, D), f32), jax.ShapeDtypeStruct((L, D), bf16)],
        compiler_params=_cp(("parallel",)),
        name="mix_residual",
    )(m, w, x, gate, nw, sc, sh)


def _mlp2_kernel(a_ref, w_ref, h_ref, g_ref, fw_ref, o_ref, *, nk):
    kk = pl.program_id(1)

    @pl.when(kk == 0)
    def _():
        for nb in range(D // MLP2_STRIP):
            cols = slice(nb * MLP2_STRIP, (nb + 1) * MLP2_STRIP)
            o_ref[:, cols] = _dot(a_ref[...], w_ref[:, cols])

    @pl.when(kk > 0)
    def _():
        for nb in range(D // MLP2_STRIP):
            cols = slice(nb * MLP2_STRIP, (nb + 1) * MLP2_STRIP)
            o_ref[:, cols] += _dot(a_ref[...], w_ref[:, cols])

    @pl.when(kk == nk - 1)
    def _():
        h2 = h_ref[...] + g_ref[...] * o_ref[...]
        y = h2 * lax.rsqrt(jnp.mean(h2 * h2, axis=-1, keepdims=True) + EPS)
        o_ref[...] = y * fw_ref[...]


def _mlp2_final(a, w, h, gate, fw):
    L = a.shape[0]
    tm, tk = min(L, MLP2_TM), MLP2_TK
    nk = DFF // tk
    vec = pl.BlockSpec((1, D), lambda i, k: (0, 0))
    return pl.pallas_call(
        functools.partial(_mlp2_kernel, nk=nk),
        grid=(L // tm, nk),
        in_specs=[pl.BlockSpec((tm, tk), lambda i, k: (i, k)),
                  pl.BlockSpec((tk, D), lambda i, k: (k, 0)),
                  pl.BlockSpec((tm, D), lambda i, k: (i, 0)), vec, vec],
        out_specs=pl.BlockSpec((tm, D), lambda i, k: (i, 0)),
        out_shape=jax.ShapeDtypeStruct((L, D), f32),
        compiler_params=_cp(("parallel", "arbitrary")),
        name="mlp2_final",
    )(a, w, h, gate, fw)


def _split_conv(conv_w, conv_b):
    def grp(w, b, width):
        return (jnp.transpose(w.reshape(KC, NG, width), (1, 0, 2)), b.reshape(NG, 1, width))
    cwx, cbx = grp(conv_w[:, :SI], conv_b[:SI], GW)
    cwb, cbb = grp(conv_w[:, SI:SI + NG * NS], conv_b[SI:SI + NG * NS], NS)
    cwc, cbc = grp(conv_w[:, SI + NG * NS:], conv_b[SI + NG * NS:], NS)
    return cwx, cbx, cwb, cbb, cwc, cbc


def kernel(x, c, ctx, c_ctx, w_mod, b_mod, norm1_w, w_in, conv_w, conv_b, ret_decay_logit, ret_norm_w,
           ssd_a_log, ssd_dt_bias, ssd_d, ssd_norm_w, w_ret_out, w_ssd_out, w_o, norm2_w, w_mlp1, w_mlp2,
           final_norm_w):
    depth = w_mod.shape[0]
    assert depth == 1 and x.shape[0] == 1 and x.shape[2] == D
    L = x.shape[1]
    Lc = ctx.shape[1]
    assert L % T == 0 and Lc % T == 0 and L % GRID_W == 0
    xl = x[0]
    xcx = ctx[0]
    ly = 0

    a8 = jnp.zeros((8, D), f32).at[0].set(c[0]).at[1].set(c_ctx)
    mod = _modulation(a8, w_mod[ly], b_mod[ly][None, :])
    sh_a, sc_a, g_a, sh_f, sc_f, g_f = [mod[0:1, i * D:(i + 1) * D] for i in range(6)]
    csh_a, csc_a = mod[1:2, 0:D], mod[1:2, D:2 * D]

    o_q, o_k, o_v, o_z, o_x = 0, D, 2 * D, 4 * D, 4 * D + SI
    o_b = o_x + SI
    o_dt = o_b + 2 * NG * NS
    o_gate = o_dt + 2 * SH

    def proj(src, col, n, **kw):
        return _project(src, w_in, col0=col, ncols=n, **kw)

    convw = _split_conv(conv_w[ly], conv_b[ly])
    bias = ssd_dt_bias[ly].reshape(1, 2 * SH)
    arow = (-jnp.exp(ssd_a_log[ly])).reshape(1, 2 * SH)
    dexp = jnp.repeat(ssd_d[ly], P).reshape(NG, 1, GW)
    snw = ssd_norm_w[ly].reshape(NG, 1, GW)
    rnw = ret_norm_w[ly].reshape(RH, 1, HD)
    nw1 = norm1_w[ly][None, :]

    tabs = _ret_tables(ret_decay_logit[ly])
    _, _, _, wf_t, wb_t, rdec = tabs
    zr = jnp.zeros((RH, HD, HD), f32)
    zs = jnp.zeros((NG, GW, NS), f32)

    uc = _norm_mod(xcx, nw1, csc_a, csh_a)
    ones_c, zeros_c = jnp.ones((Lc // GRID_W, LANES), f32), jnp.zeros((Lc // GRID_W, LANES), f32)
    ident = (ones_c, zeros_c, jnp.ones((GRID_W, LANES), f32), jnp.zeros((GRID_W, LANES), f32))
    kc = proj(uc, o_k, D, bw=HD, tn=CTX_TN, rope=ident, scale_from=0, name="ctx_k")
    vc = proj(uc, o_v, D, bw=HD, tn=CTX_TN, name="ctx_v")
    bcx = proj(uc, o_b, NG * NS, bw=NS, tn=CTX_TN, name="ctx_b")
    xsc = proj(uc, o_x, SI, bw=GW, tn=CTX_TN, name="ctx_x")
    dtc = proj(uc, o_dt, 2 * SH, bw=0, tn=2 * SH, out_dtype=f32, name="ctx_dt")
    common = dict(k=kc, k_blk=0, v=vc, v_blk=0, xs=xsc, xs_blk=0, bsrc=bcx, b_blk=0, c_blk=0, dt=dtc,
                  convw=convw, bias=bias, arow=arow, rdec=rdec, h0r=zr, h0s=zs)
    cfr, cfs = _state_sweep(Lc, fwd=True, emit=False, write_conv=False, rw=wf_t, **common)
    cbr, cbs = _state_sweep(Lc, fwd=False, emit=False, write_conv=False, rw=wb_t, **common)

    u = _norm_mod(xl, nw1, sc_a, sh_a)
    rope = _rope_tables(L)
    qk = proj(u, o_q, 2 * D, bw=HD, tn=PROJ_TN, rope=rope, scale_from=D // PROJ_TN, name="lat_qk")
    vg = proj(u, o_v, 2 * D, bw=HD, tn=PROJ_TN, name="lat_vg")
    zx = proj(u, o_z, 2 * SI, bw=GW, tn=PROJ_TN, name="lat_zx")
    bcl = proj(u, o_b, 2 * NG * NS, bw=NS, tn=PROJ_TN, name="lat_bc")
    dtl = proj(u, o_dt, 2 * SH, bw=0, tn=2 * SH, out_dtype=f32, name="lat_dt")
    gates = proj(u, o_gate, 2 * D, bw=0, tn=PROJ_TN, name="lat_gates")

    gret, gssd, xcv, bccv, _, _ = _state_sweep(
        L, fwd=False, emit=True, write_conv=True, k=qk, k_blk=1, v=vg, v_blk=0, xs=zx, xs_blk=1, bsrc=bcl,
        b_blk=0, c_blk=1, dt=dtl, convw=convw, bias=bias, arow=arow, rw=wb_t, rdec=rdec,
        h0r=cbr, h0s=cbs)
    yr, ys = _fwd_sweep(L, qk, vg, zx, xcv, bccv, dtl, gret, gssd, tabs, bias, arow, dexp, snw, rnw, cfr, cfs)

    m = _branch_out(yr, ys, w_ret_out[ly].astype(bf16), w_ssd_out[ly].astype(bf16), gates)
    h1, f = _mix_residual(m, w_o[ly].astype(bf16), xl, g_a, norm2_w[ly][None, :], sc_f, sh_f)
    a = _project(f, w_mlp1, col0=0, ncols=DFF, bw=0, tn=PROJ_TN, relu2=True, name="mlp1")
    out = _mlp2_final(a, w_mlp2[ly].astype(bf16), h1, g_f, final_norm_w[None, :])
    return out[None]
```
